```python
import math
import jax, jax.numpy as jnp
from jax import lax
import numpy as np

D_MODEL = 1024
BATCH = 2
SEQ = 8192
DEPTH = 2
DEC_BATCH = 16
DEC_SEQ = 32
PAST_LEN = 2048

CHUNK = 64
N_META = 16
D_LRU = 1024
LRU_HEADS = 8
LRU_BLOCK = D_LRU // LRU_HEADS
LRU_C = 8.0
CONV_W = 4
D_POOL = 1024
POOL_WINDOWS = (2, 4, 8, 16)
POOL_GROUPS = len(POOL_WINDOWS)
POOL_GROUP = D_POOL // POOL_GROUPS
POOL_MAX = 16
D_FF = ((8 * D_MODEL // 3 + 255) // 256) * 256
D_IN = D_LRU + D_POOL + 2 * D_MODEL
EPS = 1e-6

kernel_name = 'hawk_pool_hybrid_stream_step'


def _rmsnorm(x, g):
    x32 = x.astype(jnp.float32)
    y = x32 * lax.rsqrt(jnp.mean(x32 * x32, axis=-1, keepdims=True) + EPS)
    return (y * g.astype(jnp.float32)).astype(x.dtype)


def _causal_conv(x, prev, w, b):
    T = x.shape[1]
    xe = jnp.concatenate([prev, x], axis=1)
    y = b + w[0] * xe[:, 0:T]
    for k in range(1, CONV_W):
        y = y + w[k] * xe[:, k:k + T]
    return y, xe[:, -(CONV_W - 1):]


def _lin_combine(left, right):
    a_l, b_l = left
    a_r, b_r = right
    return a_l * a_r, a_r * b_l + b_r


def _rg_lru(x, pos, h0, w_r, b_r, w_i, b_i, lam):
    B, T, C = x.shape
    xh = x.reshape(B, T, LRU_HEADS, LRU_BLOCK)
    r = jax.nn.sigmoid(jnp.einsum('bthi,hij->bthj', xh, w_r).reshape(B, T, C) + b_r)
    i = jax.nn.sigmoid(jnp.einsum('bthi,hij->bthj', xh, w_i).reshape(B, T, C) + b_i)
    log_a = LRU_C * r.astype(jnp.float32) * jax.nn.log_sigmoid(lam.astype(jnp.float32))
    a = jnp.exp(log_a)
    mult = jnp.sqrt(-jnp.expm1(2.0 * log_a))
    mult = jnp.where((pos == 0)[None, :, None], 1.0, mult)
    bt = mult * (i * x).astype(jnp.float32)
    bt = bt.at[:, 0].add(a[:, 0] * h0.astype(jnp.float32))
    _, h = lax.associative_scan(_lin_combine, (a, bt), axis=1)
    return h.astype(x.dtype), h[:, -1].astype(x.dtype)


def _pool_mix(x, pos, prev, w_g, scale):
    B, T, C = x.shape
    P = POOL_MAX - 1
    xe_raw = jnp.concatenate([prev, x], axis=1)
    xe = xe_raw.astype(jnp.float32)
    cs = jnp.cumsum(xe, axis=1)
    cs = jnp.concatenate([jnp.zeros_like(cs[:, :1]), cs], axis=1)
    outs = []
    for g, w in enumerate(POOL_WINDOWS):
        lo, hi = g * POOL_GROUP, (g + 1) * POOL_GROUP
        s = cs[:, P + 1:P + 1 + T, lo:hi] - cs[:, P + 1 - w:P + 1 - w + T, lo:hi]
        cnt = jnp.minimum(w, pos + 1).astype(jnp.float32)[None, :, None]
        outs.append(s / cnt)
    pooled = jnp.concatenate(outs, axis=-1)
    d = (pooled - xe[:, P:]).astype(x.dtype).reshape(B, T, POOL_GROUPS, POOL_GROUP)
    y = jnp.einsum('btgi,gij->btgj', d, w_g).reshape(B, T, C) * scale
    return y, xe_raw[:, -P:]


def _mixer(xn, pos, conv_prev, h0, pool_prev, w_in, conv_w, conv_b, w_r, b_r, w_i, b_i,
           lam, pool_w, pool_scale, w_br_a, w_br_b, w_out):
    proj = xn @ w_in
    x_a, x_b, g_a, g_b = jnp.split(proj, [D_LRU, D_LRU + D_POOL, D_LRU + D_POOL + D_MODEL], axis=-1)
    c_a, conv_state = _causal_conv(x_a, conv_prev, conv_w, conv_b)
    y_a, h_last = _rg_lru(c_a, pos, h0, w_r, b_r, w_i, b_i, lam)
    y_b, pool_state = _pool_mix(x_b, pos, pool_prev, pool_w, pool_scale)
    merged = jax.nn.sigmoid(g_a) * (y_a @ w_br_a) + jax.nn.sigmoid(g_b) * (y_b @ w_br_b)
    return merged @ w_out, conv_state, h_last, pool_state


def _swiglu(xn, w_gu, w_down):
    gu = xn @ w_gu
    g, u = jnp.split(gu, [D_FF], axis=-1)
    return (jax.nn.silu(g) * u) @ w_down


def setup_inputs(seed: int = 0) -> dict:
    key = jax.random.key(seed)
    ks = jax.random.split(key, 26)
    f32 = jnp.float32
    nrm = lambda k, s, sc: jax.random.normal(k, s, f32) * sc
    u = jax.random.uniform(ks[10], (DEPTH, D_LRU), f32, 0.9, 0.999)
    a_base = u ** (1.0 / LRU_C)
    lam = jnp.log(a_base) - jnp.log1p(-a_base)
    return {
        'x_prompt': nrm(ks[0], (BATCH, SEQ, D_MODEL), 1.0),
        'x_sample': nrm(ks[1], (DEC_BATCH, DEC_SEQ, D_MODEL), 1.0),
        'state_conv': nrm(ks[2], (DEPTH, DEC_BATCH, CONV_W - 1, D_LRU), 1.0),
        'state_lru': nrm(ks[3], (DEPTH, DEC_BATCH, D_LRU), 0.5),
        'state_pool': nrm(ks[4], (DEPTH, DEC_BATCH, POOL_MAX - 1, D_POOL), 1.0),
        'meta_tokens': nrm(ks[5], (N_META, D_MODEL), 1.0),
        'norm1_g': 1.0 + nrm(ks[6], (DEPTH, D_MODEL), 0.05),
        'w_in': nrm(ks[7], (DEPTH, D_MODEL, D_IN), D_MODEL ** -0.5),
        'conv_w': nrm(ks[8], (DEPTH, CONV_W, D_LRU), CONV_W ** -0.5),
        'conv_b': nrm(ks[9], (DEPTH, D_LRU), 0.01),
        'lam': lam,
        'w_r': nrm(ks[11], (DEPTH, LRU_HEADS, LRU_BLOCK, LRU_BLOCK), LRU_BLOCK ** -0.5),
        'b_r': nrm(ks[12], (DEPTH, D_LRU), 0.01),
        'w_i': nrm(ks[13], (DEPTH, LRU_HEADS, LRU_BLOCK, LRU_BLOCK), LRU_BLOCK ** -0.5),
        'b_i': nrm(ks[14], (DEPTH, D_LRU), 0.01),
        'pool_w': nrm(ks[15], (DEPTH, POOL_GROUPS, POOL_GROUP, POOL_GROUP), POOL_GROUP ** -0.5),
        'pool_scale': 1.0 + nrm(ks[16], (DEPTH, D_POOL), 0.05),
        'w_br_a': nrm(ks[17], (DEPTH, D_LRU, D_MODEL), D_LRU ** -0.5),
        'w_br_b': nrm(ks[18], (DEPTH, D_POOL, D_MODEL), D_POOL ** -0.5),
        'w_out': nrm(ks[19], (DEPTH, D_MODEL, D_MODEL), D_MODEL ** -0.5),
        'norm2_g': 1.0 + nrm(ks[20], (DEPTH, D_MODEL), 0.05),
        'w_gu': nrm(ks[21], (DEPTH, D_MODEL, 2 * D_FF), D_MODEL ** -0.5),
        'w_down': nrm(ks[22], (DEPTH, D_FF, D_MODEL), D_FF ** -0.5),
        'final_g': 1.0 + nrm(ks[23], (D_MODEL,), 0.05),
    }


def reference(x_prompt, x_sample, state_conv, state_lru, state_pool, meta_tokens,
              norm1_g, w_in, conv_w, conv_b, lam, w_r, b_r, w_i, b_i, pool_w, pool_scale,
              w_br_a, w_br_b, w_out, norm2_g, w_gu, w_down, final_g):
    B = x_prompt.shape[0]
    meta = jnp.broadcast_to(meta_tokens[None].astype(x_prompt.dtype), (B, N_META, D_MODEL))
    xp = jnp.concatenate([meta, x_prompt], axis=1)
    xs = x_sample
    Tp, Ts = xp.shape[1], xs.shape[1]
    pos_p = jnp.arange(Tp, dtype=jnp.int32)
    pos_s = PAST_LEN + jnp.arange(Ts, dtype=jnp.int32)
    zconv = jnp.zeros((B, CONV_W - 1, D_LRU), xp.dtype)
    zh = jnp.zeros((B, D_LRU), xp.dtype)
    zpool = jnp.zeros((B, POOL_MAX - 1, D_POOL), xp.dtype)
    cp_l, hp_l, pp_l, cs_l, hs_l, ps_l = [], [], [], [], [], []
    for l in range(DEPTH):
        lw = (w_in[l], conv_w[l], conv_b[l], w_r[l], b_r[l], w_i[l], b_i[l], lam[l],
              pool_w[l], pool_scale[l], w_br_a[l], w_br_b[l], w_out[l])
        mp, c_p, h_p, p_p = _mixer(_rmsnorm(xp, norm1_g[l]), pos_p, zconv, zh, zpool, *lw)
        ms, c_s, h_s, p_s = _mixer(_rmsnorm(xs, norm1_g[l]), pos_s, state_conv[l], state_lru[l],
                                   state_pool[l], *lw)
        xp = xp + mp
        xs = xs + ms
        xp = xp + _swiglu(_rmsnorm(xp, norm2_g[l]), w_gu[l], w_down[l])
        xs = xs + _swiglu(_rmsnorm(xs, norm2_g[l]), w_gu[l], w_down[l])
        cp_l.append(c_p); hp_l.append(h_p); pp_l.append(p_p)
        cs_l.append(c_s); hs_l.append(h_s); ps_l.append(p_s)
    y_prompt = _rmsnorm(xp, final_g)[:, N_META:]
    y_sample = _rmsnorm(xs, final_g)
    return (y_prompt, y_sample, jnp.stack(cp_l), jnp.stack(hp_l), jnp.stack(pp_l),
            jnp.stack(cs_l), jnp.stack(hs_l), jnp.stack(ps_l))
```

```python
import functools

import jax
import jax.numpy as jnp
from jax import lax
from jax.experimental import pallas as pl
from jax.experimental.pallas import tpu as pltpu

D_MODEL = 1024
D_LRU = 1024
D_POOL = 1024
LRU_HEADS = 8
LRU_BLOCK = D_LRU // LRU_HEADS
LRU_C = 8.0
CONV_W = 4
POOL_WINDOWS = (2, 4, 8, 16)
POOL_GROUP = D_POOL // len(POOL_WINDOWS)
POOL_MAX = 16
D_FF = 2816
N_META = 16
PAST_LEN = 2048
EPS = 1e-6

SUBLANES = 8
BF16_ROWS = 16
CONV_HDR = 8
POOL_HDR = 16
NCH = 256
TM_PROMPT = 512
VMEM_LIMIT = 56 * 1024 * 1024

f32 = jnp.float32
bf16 = jnp.bfloat16


def _dot(a, b):
    return jnp.dot(a, b, preferred_element_type=f32)


def _rmsnorm(x, g):
    y = x * lax.rsqrt(jnp.mean(x * x, axis=-1, keepdims=True) + EPS)
    return y * g


def _row_chunk(tm):
    return 32 if tm % 32 == 0 else BF16_ROWS


def _for_rows(total, chunk, fn):
    n = total // chunk
    assert n * chunk == total
    if n == 1:
        fn(0)
    else:
        def body(k, c):
            fn(pl.multiple_of(k * chunk, chunk))
            return c
        lax.fori_loop(0, n, body, 0)


def _mixer_kernel(segs, carry, tm,
                  x_ref, convp_ref, hp_ref, poolp_ref, g1_ref, win_ref, cw_ref, cb_ref, wri_ref,
                  br_ref, bi_ref, lam_ref, pw_ref, ps_ref, wa_ref, wb_ref, wo_ref,
                  y_ref, convn_ref, hn_ref, pooln_ref,
                  xn_s, abuf, pbuf, gbuf, cbuf, cbf, a_s, b_s, ya_s, d_s, yb_s, m_s, hcar):
    rc = _row_chunk(tm)

    def load_prev():
        for s in range(len(segs)):
            abuf[s, 0:CONV_HDR, :] = jnp.zeros((CONV_HDR, D_LRU), f32)
            abuf[s, CONV_HDR - (CONV_W - 1):CONV_HDR, :] = convp_ref[s]
            pbuf[s, 0:POOL_HDR, :] = jnp.zeros((POOL_HDR, D_POOL), f32)
            pbuf[s, POOL_HDR - (POOL_MAX - 1):POOL_HDR, :] = poolp_ref[s]
        if carry:
            hcar[0:1, :] = hp_ref[0]

    if carry:
        pl.when(pl.program_id(1) == 0)(load_prev)
    else:
        load_prev()

    def norm_rows(r0):
        rows = pl.ds(r0, rc)
        xn_s[rows, :] = _rmsnorm(x_ref[rows, :], g1_ref[...]).astype(bf16)
    _for_rows(tm, rc, norm_rows)

    for c in range(4 * D_MODEL // NCH):
        col = c * NCH
        p = _dot(xn_s[...], win_ref[:, col:col + NCH])
        if col < D_LRU + D_POOL:
            dst, hdr, lo = (abuf, CONV_HDR, col) if col < D_LRU else (pbuf, POOL_HDR, col - D_LRU)
            for s, (row0, length, _) in enumerate(segs):
                dst[s, hdr:hdr + length, lo:lo + NCH] = p[row0:row0 + length]
        else:
            lo = col - D_LRU - D_POOL
            gbuf[:, lo:lo + NCH] = p

    def conv_rows(s, row0, n, t0):
        blk = abuf[s, pl.ds(t0, n + CONV_HDR), :]
        base = CONV_HDR - (CONV_W - 1)
        acc = cb_ref[...] + cw_ref[0:1, :] * blk[base:base + n]
        for k in range(1, CONV_W):
            acc = acc + cw_ref[k:k + 1, :] * blk[base + k:base + k + n]
        rows = pl.ds(row0 + t0, n)
        cbuf[rows, :] = acc
        cbf[rows, :] = acc.astype(bf16)

    for s, (row0, length, _) in enumerate(segs):
        n = min(rc, length)
        _for_rows(length, n, functools.partial(conv_rows, s, row0, n))
        convn_ref[s] = abuf[s, CONV_HDR + length - (CONV_W - 1):CONV_HDR + length, :]

    for h in range(LRU_HEADS):
        hs = slice(h * LRU_BLOCK, (h + 1) * LRU_BLOCK)
        ri = _dot(cbf[:, hs], wri_ref[h])
        a_s[:, hs] = ri[:, :LRU_BLOCK] + br_ref[:, hs]
        b_s[:, hs] = ri[:, LRU_BLOCK:] + bi_ref[:, hs]

    lam = lam_ref[...]
    log_sig_lam = jnp.minimum(lam, 0.0) - jnp.log1p(jnp.exp(-jnp.abs(lam)))
    lam_c = LRU_C * log_sig_lam

    def gate_rows(pos0, row0, n, t0):
        rows = pl.ds(row0 + t0, n)
        r = jax.nn.sigmoid(a_s[rows, :])
        i = jax.nn.sigmoid(b_s[rows, :])
        log_a = r * lam_c
        a = jnp.exp(log_a)
        th = jnp.tanh(log_a)
        mult = jnp.sqrt(-2.0 * th / (1.0 - th))
        if pos0 == 0:
            assert not carry
            pos = t0 + lax.broadcasted_iota(jnp.int32, (n, 1), 0)
            mult = jnp.where(pos == 0, 1.0, mult)
        a_s[rows, :] = a
        b_s[rows, :] = mult * (i * cbuf[rows, :])

    for s, (row0, length, pos0) in enumerate(segs):
        n = min(rc, length)
        _for_rows(length, n, functools.partial(gate_rows, pos0, row0, n))

    sub = lax.broadcasted_iota(jnp.int32, (SUBLANES, D_LRU), 0)

    def scan_block(a, b, h):
        for d in (1, 2, 4):
            keep = sub >= d
            a_prev = pltpu.roll(a, d, axis=0)
            b_prev = pltpu.roll(b, d, axis=0)
            b = jnp.where(keep, a * b_prev, 0.0) + b
            a = jnp.where(keep, a * a_prev, a)
        return a * h + b

    def scan_rows(row0, k, h):
        r0 = pl.multiple_of(row0 + k * BF16_ROWS, BF16_ROWS)
        lo = pl.ds(r0, SUBLANES)
        hi = pl.ds(r0 + SUBLANES, SUBLANES)
        h_lo = scan_block(a_s[lo, :], b_s[lo, :], h)
        h_hi = scan_block(a_s[hi, :], b_s[hi, :], h_lo[SUBLANES - 1:SUBLANES, :])
        ya_s[pl.ds(r0, BF16_ROWS), :] = jnp.concatenate([h_lo, h_hi], axis=0).astype(bf16)
        return h_hi[SUBLANES - 1:SUBLANES, :]

    for s, (row0, length, _) in enumerate(segs):
        h0 = hcar[0:1, :] if carry else hp_ref[s]
        h_last = lax.fori_loop(0, length // BF16_ROWS, functools.partial(scan_rows, row0), h0)
        hn_ref[s] = h_last
        if carry:
            hcar[0:1, :] = h_last

    def pool_rows(s, pos0, row0, n, t0):
        rows = pl.ds(row0 + t0, n)
        for g, w in enumerate(POOL_WINDOWS):
            gs = slice(g * POOL_GROUP, (g + 1) * POOL_GROUP)
            blk = pbuf[s, pl.ds(t0, n + POOL_HDR), gs]
            xb = blk[POOL_HDR:POOL_HDR + n]
            tot = xb
            for j in range(1, w):
                tot = tot + blk[POOL_HDR - j:POOL_HDR - j + n]
            if pos0 >= POOL_MAX - 1:
                pooled = tot * (1.0 / w)
            else:
                assert not carry
                pos = pos0 + t0 + lax.broadcasted_iota(jnp.int32, (n, 1), 0)
                pooled = tot / jnp.minimum(w, pos + 1).astype(f32)
            d_s[rows, gs] = (pooled - xb).astype(bf16)

    for s, (row0, length, pos0) in enumerate(segs):
        n = min(rc, length)
        _for_rows(length, n, functools.partial(pool_rows, s, pos0, row0, n))
        pooln_ref[s] = pbuf[s, POOL_HDR + length - (POOL_MAX - 1):POOL_HDR + length, :]

    if carry:
        abuf[0, 0:CONV_HDR, :] = abuf[0, tm:tm + CONV_HDR, :]
        pbuf[0, 0:POOL_HDR, :] = pbuf[0, tm:tm + POOL_HDR, :]

    for g in range(len(POOL_WINDOWS)):
        gs = slice(g * POOL_GROUP, (g + 1) * POOL_GROUP)
        yb_s[:, gs] = (_dot(d_s[:, gs], pw_ref[g]) * ps_ref[:, gs]).astype(bf16)

    for c in range(D_MODEL // NCH):
        cs = slice(c * NCH, (c + 1) * NCH)
        cs_b = slice(D_MODEL + c * NCH, D_MODEL + (c + 1) * NCH)
        ta = _dot(ya_s[...], wa_ref[:, cs])
        tb = _dot(yb_s[...], wb_ref[:, cs])
        m_s[:, cs] = (jax.nn.sigmoid(gbuf[:, cs]) * ta + jax.nn.sigmoid(gbuf[:, cs_b]) * tb).astype(bf16)

    for c in range(D_MODEL // NCH):
        cs = slice(c * NCH, (c + 1) * NCH)
        y_ref[:, cs] = x_ref[:, cs] + _dot(m_s[...], wo_ref[:, cs])


def _ffn_kernel(final, tm, x_ref, g2_ref, wgu_ref, wd_ref, fg_ref, y_ref, xn_s, h_s):
    rc = _row_chunk(tm)

    def norm_rows(r0):
        rows = pl.ds(r0, rc)
        xn_s[rows, :] = _rmsnorm(x_ref[rows, :], g2_ref[...]).astype(bf16)
    _for_rows(tm, rc, norm_rows)

    for c in range(D_FF // NCH):
        gate = _dot(xn_s[...], wgu_ref[:, c * NCH:(c + 1) * NCH])
        up = _dot(xn_s[...], wgu_ref[:, D_FF + c * NCH:D_FF + (c + 1) * NCH])
        h_s[:, c * NCH:(c + 1) * NCH] = (jax.nn.silu(gate) * up).astype(bf16)

    for c in range(D_MODEL // NCH):
        cs = slice(c * NCH, (c + 1) * NCH)
        y_ref[:, cs] = x_ref[:, cs] + _dot(h_s[...], wd_ref[:, cs])

    if final:
        def final_rows(r0):
            rows = pl.ds(r0, rc)
            y_ref[rows, :] = _rmsnorm(y_ref[rows, :], fg_ref[...])
        _for_rows(tm, rc, final_rows)


def _const_spec(shape):
    nd = len(shape)
    return pl.BlockSpec(shape, lambda *_: (0,) * nd, pipeline_mode=pl.Buffered(1))


def _mixer_call(x, conv_prev, h_prev, pool_prev, state_idx, lw, segs, carry, tm, name):
    nb, t, _ = x.shape
    nt = t // tm
    assert nt * tm == t and (carry or nt == 1)
    ns = len(segs)
    max_len = max(length for _, length, _ in segs)

    def state_in_spec(rows, width):
        return pl.BlockSpec((ns, rows, width), lambda b, i: (state_idx, 0, 0))

    def state_out_spec(rows, width):
        return pl.BlockSpec((ns, rows, width), lambda b, i: (b, 0, 0))

    x_spec = pl.BlockSpec((None, tm, D_MODEL), lambda b, i: (b, i, 0))
    weights = (lw['norm1_g'], lw['w_in'], lw['conv_w'], lw['conv_b'], lw['wri'], lw['b_r'], lw['b_i'],
               lw['lam'], lw['pool_w'], lw['pool_scale'], lw['w_br_a'], lw['w_br_b'], lw['w_out'])
    in_specs = [x_spec,
                state_in_spec(CONV_W - 1, D_LRU), state_in_spec(1, D_LRU), state_in_spec(POOL_MAX - 1, D_POOL)]
    in_specs += [_const_spec(w.shape) for w in weights]
    out_shape = (jax.ShapeDtypeStruct(x.shape, f32),
                 jax.ShapeDtypeStruct((nb * ns, CONV_W - 1, D_LRU), f32),
                 jax.ShapeDtypeStruct((nb * ns, 1, D_LRU), f32),
                 jax.ShapeDtypeStruct((nb * ns, POOL_MAX - 1, D_POOL), f32))
    out_specs = (x_spec, state_out_spec(CONV_W - 1, D_LRU), state_out_spec(1, D_LRU),
                 state_out_spec(POOL_MAX - 1, D_POOL))
    scratch = [
        pltpu.VMEM((tm, D_MODEL), bf16),
        pltpu.VMEM((ns, CONV_HDR + max_len, D_LRU), f32),
        pltpu.VMEM((ns, POOL_HDR + max_len, D_POOL), f32),
        pltpu.VMEM((tm, 2 * D_MODEL), f32),
        pltpu.VMEM((tm, D_LRU), f32),
        pltpu.VMEM((tm, D_LRU), bf16),
        pltpu.VMEM((tm, D_LRU), f32),
        pltpu.VMEM((tm, D_LRU), f32),
        pltpu.VMEM((tm, D_LRU), bf16),
        pltpu.VMEM((tm, D_POOL), bf16),
        pltpu.VMEM((tm, D_POOL), bf16),
        pltpu.VMEM((tm, D_MODEL), bf16),
        pltpu.VMEM((SUBLANES, D_LRU), f32),
    ]
    return pl.pallas_call(
        functools.partial(_mixer_kernel, tuple(segs), carry, tm),
        grid=(nb, nt),
        in_specs=in_specs,
        out_specs=out_specs,
        out_shape=out_shape,
        scratch_shapes=scratch,
        compiler_params=pltpu.CompilerParams(
            dimension_semantics=("arbitrary", "arbitrary"), vmem_limit_bytes=VMEM_LIMIT),
        name=name,
    )(x, conv_prev, h_prev, pool_prev, *weights)


def _ffn_call(x, lw, final_g, final, tm, name):
    nb, t, _ = x.shape
    nt = t // tm
    assert nt * tm == t
    x_spec = pl.BlockSpec((None, tm, D_MODEL), lambda b, i: (b, i, 0))
    weights = (lw['norm2_g'], lw['w_gu'], lw['w_down'], final_g)
    return pl.pallas_call(
        functools.partial(_ffn_kernel, final, tm),
        grid=(nb, nt),
        in_specs=[x_spec] + [_const_spec(w.shape) for w in weights],
        out_specs=x_spec,
        out_shape=jax.ShapeDtypeStruct(x.shape, f32),
        scratch_shapes=[pltpu.VMEM((tm, D_MODEL), bf16), pltpu.VMEM((tm, D_FF), bf16)],
        compiler_params=pltpu.CompilerParams(
            dimension_semantics=("arbitrary", "arbitrary"), vmem_limit_bytes=VMEM_LIMIT),
        name=name,
    )(x, *weights)


def kernel(x_prompt, x_sample, state_conv, state_lru, state_pool, meta_tokens, norm1_g, w_in, conv_w, conv_b,
           lam, w_r, b_r, w_i, b_i, pool_w, pool_scale, w_br_a, w_br_b, w_out, norm2_g, w_gu, w_down, final_g):
    depth = w_in.shape[0]
    nb, seq, _ = x_prompt.shape
    ns, ts, _ = x_sample.shape
    row = lambda v: v.reshape(1, -1).astype(f32)

    segs_s = [(s * ts, ts, PAST_LEN) for s in range(ns)] + [(ns * ts, N_META, 0)]
    tm_s = ns * ts + N_META
    xs = jnp.concatenate([x_sample.reshape(ns * ts, D_MODEL), meta_tokens.astype(f32)], axis=0)[None]
    xp = x_prompt
    segs_p = [(0, TM_PROMPT, N_META)]
    fg = row(final_g)

    outs = {k: [] for k in ('cp', 'hp', 'pp', 'cs', 'hs', 'ps')}
    for l in range(depth):
        lw = {
            'norm1_g': row(norm1_g[l]), 'w_in': w_in[l].astype(bf16), 'conv_w': conv_w[l], 'conv_b': row(conv_b[l]),
            'wri': jnp.concatenate([w_r[l], w_i[l]], axis=-1).astype(bf16),
            'b_r': row(b_r[l]), 'b_i': row(b_i[l]), 'lam': row(lam[l]),
            'pool_w': pool_w[l].astype(bf16), 'pool_scale': row(pool_scale[l]),
            'w_br_a': w_br_a[l].astype(bf16), 'w_br_b': w_br_b[l].astype(bf16), 'w_out': w_out[l].astype(bf16),
            'norm2_g': row(norm2_g[l]), 'w_gu': w_gu[l].astype(bf16), 'w_down': w_down[l].astype(bf16),
        }
        last = l == depth - 1
        zeros = lambda rows, width: jnp.zeros((1, rows, width), f32)
        conv_prev = jnp.concatenate([state_conv[l], zeros(CONV_W - 1, D_LRU)], axis=0)
        h_prev = jnp.concatenate([state_lru[l][:, None, :], zeros(1, D_LRU)], axis=0)
        pool_prev = jnp.concatenate([state_pool[l], zeros(POOL_MAX - 1, D_POOL)], axis=0)

        xs, c_s, h_s, p_s = _mixer_call(xs, conv_prev, h_prev, pool_prev, 0, lw, segs_s, False, tm_s,
                                        f"mixer_step_l{l}")
        xs = _ffn_call(xs, lw, fg, last, tm_s, f"ffn_step_l{l}")
        xp, c_p, h_p, p_p = _mixer_call(xp, c_s, h_s, p_s, ns, lw, segs_p, True, TM_PROMPT, f"mixer_prompt_l{l}")
        xp = _ffn_call(xp, lw, fg, last, TM_PROMPT, f"ffn_prompt_l{l}")

        outs['cp'].append(c_p); outs['hp'].append(h_p[:, 0]); outs['pp'].append(p_p)
        outs['cs'].append(c_s[:ns]); outs['hs'].append(h_s[:ns, 0]); outs['ps'].append(p_s[:ns])

    y_sample = xs[0, :ns * ts].reshape(ns, ts, D_MODEL)
    return (xp, y_sample, jnp.stack(outs['cp']), jnp.stack(outs['hp']), jnp.stack(outs['pp']),
            jnp.stack(outs['cs']), jnp.stack(outs['hs']), jnp.stack(outs['ps']))
```

```python
import functools

import jax
import jax.numpy as jnp
from jax import lax
from jax.experimental import pallas as pl
from jax.experimental.pallas import tpu as pltpu

D_MODEL = 1024
D_LRU = 1024
D_POOL = 1024
LRU_HEADS = 8
LRU_BLOCK = D_LRU // LRU_HEADS
LRU_C = 8.0
CONV_W = 4
POOL_WINDOWS = (2, 4, 8, 16)
POOL_GROUP = D_POOL // len(POOL_WINDOWS)
POOL_MAX = 16
D_FF = 2816
N_META = 16
PAST_LEN = 2048
EPS = 1e-6

SUBLANES = 8
BF16_ROWS = 16
CONV_HDR = 8
POOL_HDR = 16
NCH = 256
TM_PROMPT = 512
VMEM_LIMIT = 56 * 1024 * 1024

f32 = jnp.float32
bf16 = jnp.bfloat16


def _dot(a, b):
    return jnp.dot(a, b, preferred_element_type=f32)


def _rmsnorm(x, g):
    y = x * lax.rsqrt(jnp.mean(x * x, axis=-1, keepdims=True) + EPS)
    return y * g


def _row_chunk(tm):
    return 32 if tm % 32 == 0 else BF16_ROWS


def _norm_chunk(tm):
    return max(c for c in range(BF16_ROWS, 129, BF16_ROWS) if tm % c == 0)


def _sigmoid(x):
    return 0.5 * jnp.tanh(0.5 * x) + 0.5


def _shift_rows(x, k):
    return pltpu.roll(x, k, axis=0)


def _for_rows(total, chunk, fn):
    n = total // chunk
    assert n * chunk == total
    if n == 1:
        fn(0)
    else:
        def body(k, c):
            fn(pl.multiple_of(k * chunk, chunk))
            return c
        lax.fori_loop(0, n, body, 0)


def _mixer_kernel(segs, carry, tm,
                  x_ref, convp_ref, hp_ref, poolp_ref, g1_ref, win_ref, cw_ref, cb_ref, wri_ref,
                  br_ref, bi_ref, lam_ref, pw_ref, ps_ref, wa_ref, wb_ref, wo_ref,
                  y_ref, convn_ref, hn_ref, pooln_ref,
                  xn_s, abuf, pbuf, gbuf, cbuf, cbf, a_s, b_s, ya_s, d_s, yb_s, m_s, hcar):
    rc = _row_chunk(tm)

    def load_prev():
        for s in range(len(segs)):
            abuf[s, 0:CONV_HDR, :] = jnp.zeros((CONV_HDR, D_LRU), f32)
            abuf[s, CONV_HDR - (CONV_W - 1):CONV_HDR, :] = convp_ref[s]
            pbuf[s, 0:POOL_HDR, :] = jnp.zeros((POOL_HDR, D_POOL), f32)
            pbuf[s, POOL_HDR - (POOL_MAX - 1):POOL_HDR, :] = poolp_ref[s]
        if carry:
            hcar[0:1, :] = hp_ref[0]

    if carry:
        pl.when(pl.program_id(1) == 0)(load_prev)
    else:
        load_prev()

    nc = _norm_chunk(tm)

    def norm_rows(r0):
        rows = pl.ds(r0, nc)
        xn_s[rows, :] = _rmsnorm(x_ref[rows, :], g1_ref[...]).astype(bf16)
    _for_rows(tm, nc, norm_rows)

    for c in range(4 * D_MODEL // NCH):
        col = c * NCH
        p = _dot(xn_s[...], win_ref[:, col:col + NCH])
        if col < D_LRU + D_POOL:
            dst, hdr, lo = (abuf, CONV_HDR, col) if col < D_LRU else (pbuf, POOL_HDR, col - D_LRU)
            for s, (row0, length, _) in enumerate(segs):
                dst[s, hdr:hdr + length, lo:lo + NCH] = p[row0:row0 + length]
        else:
            lo = col - D_LRU - D_POOL
            gbuf[:, lo:lo + NCH] = p

    def conv_rows(s, row0, n, t0):
        blk = abuf[s, pl.ds(t0, n + CONV_HDR), :]
        tap = lambda k: _shift_rows(blk, CONV_W - 1 - k)[CONV_HDR:] if k < CONV_W - 1 else blk[CONV_HDR:]
        acc = cb_ref[...] + cw_ref[0:1, :] * tap(0)
        for k in range(1, CONV_W):
            acc = acc + cw_ref[k:k + 1, :] * tap(k)
        rows = pl.ds(row0 + t0, n)
        cbuf[rows, :] = acc
        cbf[rows, :] = acc.astype(bf16)

    for s, (row0, length, _) in enumerate(segs):
        n = min(rc, length)
        _for_rows(length, n, functools.partial(conv_rows, s, row0, n))
        convn_ref[s] = abuf[s, CONV_HDR + length - (CONV_W - 1):CONV_HDR + length, :]

    for h in range(LRU_HEADS):
        hs = slice(h * LRU_BLOCK, (h + 1) * LRU_BLOCK)
        ri = _dot(cbf[:, hs], wri_ref[h])
        a_s[:, hs] = ri[:, :LRU_BLOCK] + br_ref[:, hs]
        b_s[:, hs] = ri[:, LRU_BLOCK:] + bi_ref[:, hs]

    lam = lam_ref[...]
    log_sig_lam = jnp.minimum(lam, 0.0) - jnp.log1p(jnp.exp(-jnp.abs(lam)))
    lam_c = LRU_C * log_sig_lam

    def gate_rows(pos0, row0, n, t0):
        rows = pl.ds(row0 + t0, n)
        r = _sigmoid(a_s[rows, :])
        i = _sigmoid(b_s[rows, :])
        log_a = r * lam_c
        a = jnp.exp(log_a)
        th = jnp.tanh(log_a)
        mult = jnp.sqrt(-2.0 * th / (1.0 - th))
        if pos0 == 0:
            assert not carry
            pos = t0 + lax.broadcasted_iota(jnp.int32, (n, 1), 0)
            mult = jnp.where(pos == 0, 1.0, mult)
        a_s[rows, :] = a
        b_s[rows, :] = mult * (i * cbuf[rows, :])

    for s, (row0, length, pos0) in enumerate(segs):
        n = min(rc, length)
        _for_rows(length, n, functools.partial(gate_rows, pos0, row0, n))

    sub = lax.broadcasted_iota(jnp.int32, (SUBLANES, D_LRU), 0)

    def scan_block(a, b, h):
        for d in (1, 2, 4):
            keep = sub >= d
            a_prev = pltpu.roll(a, d, axis=0)
            b_prev = pltpu.roll(b, d, axis=0)
            b = jnp.where(keep, a * b_prev, 0.0) + b
            a = jnp.where(keep, a * a_prev, a)
        return a * h + b

    def scan_rows(row0, k, h):
        r0 = pl.multiple_of(row0 + k * BF16_ROWS, BF16_ROWS)
        lo = pl.ds(r0, SUBLANES)
        hi = pl.ds(r0 + SUBLANES, SUBLANES)
        h_lo = scan_block(a_s[lo, :], b_s[lo, :], h)
        h_hi = scan_block(a_s[hi, :], b_s[hi, :], h_lo[SUBLANES - 1:SUBLANES, :])
        ya_s[pl.ds(r0, BF16_ROWS), :] = jnp.concatenate([h_lo, h_hi], axis=0).astype(bf16)
        return h_hi[SUBLANES - 1:SUBLANES, :]

    for s, (row0, length, _) in enumerate(segs):
        h0 = hcar[0:1, :] if carry else hp_ref[s]
        h_last = lax.fori_loop(0, length // BF16_ROWS, functools.partial(scan_rows, row0), h0)
        hn_ref[s] = h_last
        if carry:
            hcar[0:1, :] = h_last

    def pool_rows(s, pos0, row0, n, t0):
        rows = pl.ds(row0 + t0, n)
        for g, w in enumerate(POOL_WINDOWS):
            gs = slice(g * POOL_GROUP, (g + 1) * POOL_GROUP)
            blk = pbuf[s, pl.ds(t0, n + POOL_HDR), gs]
            xb = blk[POOL_HDR:]
            tot, span = blk, 1
            while span < w:
                tot = tot + _shift_rows(tot, span)
                span *= 2
            tot = tot[POOL_HDR:]
            if pos0 >= POOL_MAX - 1:
                pooled = tot * (1.0 / w)
            else:
                assert not carry
                pos = pos0 + t0 + lax.broadcasted_iota(jnp.int32, (n, 1), 0)
                pooled = tot / jnp.minimum(w, pos + 1).astype(f32)
            d_s[rows, gs] = (pooled - xb).astype(bf16)

    for s, (row0, length, pos0) in enumerate(segs):
        n = min(rc, length)
        _for_rows(length, n, functools.partial(pool_rows, s, pos0, row0, n))
        pooln_ref[s] = pbuf[s, POOL_HDR + length - (POOL_MAX - 1):POOL_HDR + length, :]

    if carry:
        abuf[0, 0:CONV_HDR, :] = abuf[0, tm:tm + CONV_HDR, :]
        pbuf[0, 0:POOL_HDR, :] = pbuf[0, tm:tm + POOL_HDR, :]

    for g in range(len(POOL_WINDOWS)):
        gs = slice(g * POOL_GROUP, (g + 1) * POOL_GROUP)
        yb_s[:, gs] = (_dot(d_s[:, gs], pw_ref[g]) * ps_ref[:, gs]).astype(bf16)

    for c in range(D_MODEL // NCH):
        cs = slice(c * NCH, (c + 1) * NCH)
        cs_b = slice(D_MODEL + c * NCH, D_MODEL + (c + 1) * NCH)
        ta = _dot(ya_s[...], wa_ref[:, cs])
        tb = _dot(yb_s[...], wb_ref[:, cs])
        m_s[:, cs] = (_sigmoid(gbuf[:, cs]) * ta + _sigmoid(gbuf[:, cs_b]) * tb).astype(bf16)

    for c in range(D_MODEL // NCH):
        cs = slice(c * NCH, (c + 1) * NCH)
        y_ref[:, cs] = x_ref[:, cs] + _dot(m_s[...], wo_ref[:, cs])


def _ffn_kernel(final, tm, x_ref, g2_ref, wgu_ref, wd_ref, fg_ref, y_ref, xn_s, h_s):
    nc = _norm_chunk(tm)

    def norm_rows(r0):
        rows = pl.ds(r0, nc)
        xn_s[rows, :] = _rmsnorm(x_ref[rows, :], g2_ref[...]).astype(bf16)
    _for_rows(tm, nc, norm_rows)

    for c in range(D_FF // NCH):
        gate = _dot(xn_s[...], wgu_ref[:, c * NCH:(c + 1) * NCH])
        up = _dot(xn_s[...], wgu_ref[:, D_FF + c * NCH:D_FF + (c + 1) * NCH])
        h_s[:, c * NCH:(c + 1) * NCH] = (jax.nn.silu(gate) * up).astype(bf16)

    for c in range(D_MODEL // NCH):
        cs = slice(c * NCH, (c + 1) * NCH)
        y_ref[:, cs] = x_ref[:, cs] + _dot(h_s[...], wd_ref[:, cs])

    if final:
        def final_rows(r0):
            rows = pl.ds(r0, nc)
            y_ref[rows, :] = _rmsnorm(y_ref[rows, :], fg_ref[...])
        _for_rows(tm, nc, final_rows)


def _const_spec(shape):
    nd = len(shape)
    return pl.BlockSpec(shape, lambda *_: (0,) * nd, pipeline_mode=pl.Buffered(1))


def _mixer_call(x, conv_prev, h_prev, pool_prev, state_idx, lw, segs, carry, tm, name):
    nb, t, _ = x.shape
    nt = t // tm
    assert nt * tm == t and (carry or nt == 1)
    ns = len(segs)
    max_len = max(length for _, length, _ in segs)

    def state_in_spec(rows, width):
        return pl.BlockSpec((ns, rows, width), lambda b, i: (state_idx, 0, 0))

    def state_out_spec(rows, width):
        return pl.BlockSpec((ns, rows, width), lambda b, i: (b, 0, 0))

    x_spec = pl.BlockSpec((None, tm, D_MODEL), lambda b, i: (b, i, 0))
    weights = (lw['norm1_g'], lw['w_in'], lw['conv_w'], lw['conv_b'], lw['wri'], lw['b_r'], lw['b_i'],
               lw['lam'], lw['pool_w'], lw['pool_scale'], lw['w_br_a'], lw['w_br_b'], lw['w_out'])
    in_specs = [x_spec,
                state_in_spec(CONV_W - 1, D_LRU), state_in_spec(1, D_LRU), state_in_spec(POOL_MAX - 1, D_POOL)]
    in_specs += [_const_spec(w.shape) for w in weights]
    out_shape = (jax.ShapeDtypeStruct(x.shape, f32),
                 jax.ShapeDtypeStruct((nb * ns, CONV_W - 1, D_LRU), f32),
                 jax.ShapeDtypeStruct((nb * ns, 1, D_LRU), f32),
                 jax.ShapeDtypeStruct((nb * ns, POOL_MAX - 1, D_POOL), f32))
    out_specs = (x_spec, state_out_spec(CONV_W - 1, D_LRU), state_out_spec(1, D_LRU),
                 state_out_spec(POOL_MAX - 1, D_POOL))
    scratch = [
        pltpu.VMEM((tm, D_MODEL), bf16),
        pltpu.VMEM((ns, CONV_HDR + max_len, D_LRU), f32),
        pltpu.VMEM((ns, POOL_HDR + max_len, D_POOL), f32),
        pltpu.VMEM((tm, 2 * D_MODEL), f32),
        pltpu.VMEM((tm, D_LRU), f32),
        pltpu.VMEM((tm, D_LRU), bf16),
        pltpu.VMEM((tm, D_LRU), f32),
        pltpu.VMEM((tm, D_LRU), f32),
        pltpu.VMEM((tm, D_LRU), bf16),
        pltpu.VMEM((tm, D_POOL), bf16),
        pltpu.VMEM((tm, D_POOL), bf16),
        pltpu.VMEM((tm, D_MODEL), bf16),
        pltpu.VMEM((SUBLANES, D_LRU), f32),
    ]
    return pl.pallas_call(
        functools.partial(_mixer_kernel, tuple(segs), carry, tm),
        grid=(nb, nt),
        in_specs=in_specs,
        out_specs=out_specs,
        out_shape=out_shape,
        scratch_shapes=scratch,
        compiler_params=pltpu.CompilerParams(
            dimension_semantics=("arbitrary", "arbitrary"), vmem_limit_bytes=VMEM_LIMIT),
        name=name,
    )(x, conv_prev, h_prev, pool_prev, *weights)


def _ffn_call(x, lw, final_g, final, tm, name):
    nb, t, _ = x.shape
    nt = t // tm
    assert nt * tm == t
    x_spec = pl.BlockSpec((None, tm, D_MODEL), lambda b, i: (b, i, 0))
    weights = (lw['norm2_g'], lw['w_gu'], lw['w_down'], final_g)
    return pl.pallas_call(
        functools.partial(_ffn_kernel, final, tm),
        grid=(nb, nt),
        in_specs=[x_spec] + [_const_spec(w.shape) for w in weights],
        out_specs=x_spec,
        out_shape=jax.ShapeDtypeStruct(x.shape, f32),
        scratch_shapes=[pltpu.VMEM((tm, D_MODEL), bf16), pltpu.VMEM((tm, D_FF), bf16)],
        compiler_params=pltpu.CompilerParams(
            dimension_semantics=("arbitrary", "arbitrary"), vmem_limit_bytes=VMEM_LIMIT),
        name=name,
    )(x, *weights)


def kernel(x_prompt, x_sample, state_conv, state_lru, state_pool, meta_tokens, norm1_g, w_in, conv_w, conv_b,
           lam, w_r, b_r, w_i, b_i, pool_w, pool_scale, w_br_a, w_br_b, w_out, norm2_g, w_gu, w_down, final_g):
    depth = w_in.shape[0]
    nb, seq, _ = x_prompt.shape
    ns, ts, _ = x_sample.shape
    row = lambda v: v.reshape(1, -1).astype(f32)

    segs_s = [(s * ts, ts, PAST_LEN) for s in range(ns)] + [(ns * ts, N_META, 0)]
    tm_s = ns * ts + N_META
    xs = jnp.concatenate([x_sample.reshape(ns * ts, D_MODEL), meta_tokens.astype(f32)], axis=0)[None]
    xp = x_prompt
    segs_p = [(0, TM_PROMPT, N_META)]
    fg = row(final_g)

    outs = {k: [] for k in ('cp', 'hp', 'pp', 'cs', 'hs', 'ps')}
    for l in range(depth):
        lw = {
            'norm1_g': row(norm1_g[l]), 'w_in': w_in[l].astype(bf16), 'conv_w': conv_w[l], 'conv_b': row(conv_b[l]),
            'wri': jnp.concatenate([w_r[l], w_i[l]], axis=-1).astype(bf16),
            'b_r': row(b_r[l]), 'b_i': row(b_i[l]), 'lam': row(lam[l]),
            'pool_w': pool_w[l].astype(bf16), 'pool_scale': row(pool_scale[l]),
            'w_br_a': w_br_a[l].astype(bf16), 'w_br_b': w_br_b[l].astype(bf16), 'w_out': w_out[l].astype(bf16),
            'norm2_g': row(norm2_g[l]), 'w_gu': w_gu[l].astype(bf16), 'w_down': w_down[l].astype(bf16),
        }
        last = l == depth - 1
        zeros = lambda rows, width: jnp.zeros((1, rows, width), f32)
        conv_prev = jnp.concatenate([state_conv[l], zeros(CONV_W - 1, D_LRU)], axis=0)
        h_prev = jnp.concatenate([state_lru[l][:, None, :], zeros(1, D_LRU)], axis=0)
        pool_prev = jnp.concatenate([state_pool[l], zeros(POOL_MAX - 1, D_POOL)], axis=0)

        xs, c_s, h_s, p_s = _mixer_call(xs, conv_prev, h_prev, pool_prev, 0, lw, segs_s, False, tm_s,
                                        f"mixer_step_l{l}")
        xs = _ffn_call(xs, lw, fg, last, tm_s, f"ffn_step_l{l}")
        xp, c_p, h_p, p_p = _mixer_call(xp, c_s, h_s, p_s, ns, lw, segs_p, True, TM_PROMPT, f"mixer_prompt_l{l}")
        xp = _ffn_call(xp, lw, fg, last, TM_PROMPT, f"ffn_prompt_l{l}")

        outs['cp'].append(c_p); outs['hp'].append(h_p[:, 0]); outs['pp'].append(p_p)
        outs['cs'].append(c_s[:ns]); outs['hs'].append(h_s[:ns, 0]); outs['ps'].append(p_s[:ns])

    y_sample = xs[0, :ns * ts].reshape(ns, ts, D_MODEL)
    return (xp, y_sample, jnp.stack(outs['cp']), jnp.stack(outs['hp']), jnp.stack(outs['pp']),
            jnp.stack(outs['cs']), jnp.stack(outs['hs']), jnp.stack(outs['ps']))
```

```python
import functools

import jax
import jax.numpy as jnp
from jax import lax
from jax.experimental import pallas as pl
from jax.experimental.pallas import tpu as pltpu

D_MODEL = 1024
D_LRU = 1024
D_POOL = 1024
LRU_HEADS = 8
LRU_BLOCK = D_LRU // LRU_HEADS
LRU_C = 8.0
CONV_W = 4
POOL_WINDOWS = (2, 4, 8, 16)
POOL_GROUP = D_POOL // len(POOL_WINDOWS)
POOL_MAX = 16
D_FF = 2816
N_META = 16
PAST_LEN = 2048
EPS = 1e-6

SUBLANES = 8
BF16_ROWS = 16
HDR = 16
NCH = 256
N_IN_CH = 4 * D_MODEL // NCH
CONV_CH = 0
POOL_CH = D_LRU // NCH
GATE_A_CH = (D_LRU + D_POOL) // NCH
GATE_B_CH = (D_LRU + D_POOL + D_MODEL) // NCH
TM_PROMPT = 512
ROW_CH = 32
VMEM_LIMIT = 60 * 1024 * 1024

f32 = jnp.float32
bf16 = jnp.bfloat16

assert POOL_GROUP == NCH and HDR >= POOL_MAX - 1 and HDR % BF16_ROWS == 0


def _dot(a, b):
    return jnp.dot(a, b, preferred_element_type=f32)


def _rmsnorm(x, g):
    y = x * lax.rsqrt(jnp.mean(x * x, axis=-1, keepdims=True) + EPS)
    return y * g


def _norm_chunk(tm):
    return max(c for c in range(BF16_ROWS, 129, BF16_ROWS) if tm % c == 0)


def _sigmoid(x):
    return 0.5 * jnp.tanh(0.5 * x) + 0.5


def _shift_rows(x, k):
    return pltpu.roll(x, k, axis=0)


def _for_rows(total, chunk, fn):
    n = total // chunk
    assert n * chunk == total
    if n == 1:
        fn(0)
    else:
        def body(k, c):
            fn(pl.multiple_of(k * chunk, chunk))
            return c
        lax.fori_loop(0, n, body, 0)


def _norm_to_bf16(x_ref, g_ref, xn_s, tm):
    nc = _norm_chunk(tm)

    def norm_rows(r0):
        rows = pl.ds(r0, nc)
        xn_s[rows, :] = _rmsnorm(x_ref[rows, :], g_ref[...]).astype(bf16)
    _for_rows(tm, nc, norm_rows)


def _conv_taps(blk, cw_ref, cb_ref, cs):
    tap = lambda k: _shift_rows(blk, CONV_W - 1 - k)[HDR:] if k < CONV_W - 1 else blk[HDR:]
    acc = cb_ref[:, cs] + cw_ref[0:1, cs] * tap(0)
    for k in range(1, CONV_W):
        acc = acc + cw_ref[k:k + 1, cs] * tap(k)
    return acc


def _lam_scaled(lam_ref):
    lam = lam_ref[...]
    log_sig_lam = jnp.minimum(lam, 0.0) - jnp.log1p(jnp.exp(-jnp.abs(lam)))
    return LRU_C * log_sig_lam


def _lru_coeffs(r_pre, i_pre, c, lam_c, start_mask=None):
    r = _sigmoid(r_pre)
    i = _sigmoid(i_pre)
    log_a = r * lam_c
    a = jnp.exp(log_a)
    th = jnp.tanh(log_a)
    mult = jnp.sqrt(-2.0 * th / (1.0 - th))
    if start_mask is not None:
        mult = jnp.where(start_mask, 1.0, mult)
    return a, mult * (i * c)


def _scan_block(a, b, h):
    sub = lax.broadcasted_iota(jnp.int32, a.shape, 0)
    for d in (1, 2, 4):
        keep = sub >= d
        a_prev = pltpu.roll(a, d, axis=0)
        b_prev = pltpu.roll(b, d, axis=0)
        b = jnp.where(keep, a * b_prev, 0.0) + b
        a = jnp.where(keep, a * a_prev, a)
    return a * h + b


def _pool_diff(blk, w, cnt=None):
    xb = blk[HDR:]
    tot, span = blk, 1
    while span < w:
        tot = tot + _shift_rows(tot, span)
        span *= 2
    tot = tot[HDR:]
    pooled = tot * (1.0 / w) if cnt is None else tot / cnt
    return pooled - xb


def _branch_out(tm, proj_rows, ya_s, d_s, yb_s, m_s, pw_ref, ps_ref, wa_ref, wb_ref, wo_ref, xres_ref, y_ref):
    for g in range(len(POOL_WINDOWS)):
        gs = slice(g * POOL_GROUP, (g + 1) * POOL_GROUP)
        yb_s[:, gs] = (_dot(d_s[:, gs], pw_ref[g]) * ps_ref[:, gs]).astype(bf16)
    for c in range(D_MODEL // NCH):
        cs = slice(c * NCH, (c + 1) * NCH)
        ta = _dot(ya_s[...], wa_ref[:, cs])
        tb = _dot(yb_s[...], wb_ref[:, cs])
        m_s[:, cs] = (_sigmoid(proj_rows(GATE_A_CH + c)) * ta + _sigmoid(proj_rows(GATE_B_CH + c)) * tb).astype(bf16)
    for c in range(D_MODEL // NCH):
        cs = slice(c * NCH, (c + 1) * NCH)
        y_ref[:, cs] = xres_ref[:, cs] + _dot(m_s[...], wo_ref[:, cs])


def _gate_preact(cbf, wri_ref, br_ref, bi_ref, a_s, b_s):
    for h in range(LRU_HEADS):
        hs = slice(h * LRU_BLOCK, (h + 1) * LRU_BLOCK)
        ri = _dot(cbf[:, hs], wri_ref[h])
        a_s[:, hs] = ri[:, :LRU_BLOCK] + br_ref[:, hs]
        b_s[:, hs] = ri[:, LRU_BLOCK:] + bi_ref[:, hs]


def _mixer_step_kernel(segs, tm,
                       x_ref, convp_ref, hp_ref, poolp_ref, g1_ref, win_ref, cw_ref, cb_ref, wri_ref,
                       br_ref, bi_ref, lam_ref, pw_ref, ps_ref, wa_ref, wb_ref, wo_ref,
                       y_ref, convn_ref, hn_ref, pooln_ref,
                       xn_s, proj, gbuf, cbuf, cbf, a_s, b_s, ya_s, d_s, yb_s, m_s):
    n_hist = GATE_A_CH
    for s in range(len(segs)):
        for c in range(n_hist):
            proj[s, c, 0:HDR, :] = jnp.zeros((HDR, NCH), f32)
        for c in range(POOL_CH):
            cs = slice(c * NCH, (c + 1) * NCH)
            proj[s, CONV_CH + c, HDR - (CONV_W - 1):HDR, :] = convp_ref[s, :, cs]
            proj[s, POOL_CH + c, HDR - (POOL_MAX - 1):HDR, :] = poolp_ref[s, :, cs]

    _norm_to_bf16(x_ref, g1_ref, xn_s, tm)

    for c in range(N_IN_CH):
        p = _dot(xn_s[...], win_ref[c])
        if c < n_hist:
            for s, (row0, length, _) in enumerate(segs):
                proj[s, c, HDR:HDR + length, :] = p[row0:row0 + length]
        else:
            gbuf[c - n_hist] = p

    for s, (row0, length, _) in enumerate(segs):
        n = min(ROW_CH, length)

        def conv_rows(t0, s=s, row0=row0, n=n):
            rows = pl.ds(row0 + t0, n)
            for c in range(POOL_CH):
                cs = slice(c * NCH, (c + 1) * NCH)
                acc = _conv_taps(proj[s, CONV_CH + c, pl.ds(t0, n + HDR), :], cw_ref, cb_ref, cs)
                cbuf[rows, cs] = acc
                cbf[rows, cs] = acc.astype(bf16)
        _for_rows(length, n, conv_rows)
        for c in range(POOL_CH):
            cs = slice(c * NCH, (c + 1) * NCH)
            convn_ref[s, :, cs] = proj[s, CONV_CH + c, HDR + length - (CONV_W - 1):HDR + length, :]
            pooln_ref[s, :, cs] = proj[s, POOL_CH + c, HDR + length - (POOL_MAX - 1):HDR + length, :]

    _gate_preact(cbf, wri_ref, br_ref, bi_ref, a_s, b_s)
    lam_c = _lam_scaled(lam_ref)

    for s, (row0, length, pos0) in enumerate(segs):
        n = min(ROW_CH, length)

        def mix_rows(t0, h, s=s, row0=row0, n=n, pos0=pos0):
            rows = pl.ds(row0 + t0, n)
            pos = pos0 + t0 + lax.broadcasted_iota(jnp.int32, (n, 1), 0)
            start = (pos == 0) if pos0 == 0 else None
            a, b = _lru_coeffs(a_s[rows, :], b_s[rows, :], cbuf[rows, :], lam_c, start)
            hs = []
            for k in range(n // SUBLANES):
                blk = slice(k * SUBLANES, (k + 1) * SUBLANES)
                hs.append(_scan_block(a[blk], b[blk], h))
                h = hs[-1][SUBLANES - 1:SUBLANES, :]
            ya_s[rows, :] = jnp.concatenate(hs, axis=0).astype(bf16)
            cnt = None if pos0 >= POOL_MAX - 1 else pos + 1
            for g, w in enumerate(POOL_WINDOWS):
                gs = slice(g * POOL_GROUP, (g + 1) * POOL_GROUP)
                cnt_g = None if cnt is None else jnp.minimum(w, cnt).astype(f32)
                d_s[rows, gs] = _pool_diff(proj[s, POOL_CH + g, pl.ds(t0, n + HDR), :], w, cnt_g).astype(bf16)
            return h

        steps = length // n
        h = hp_ref[s]
        if steps == 1:
            h = mix_rows(0, h)
        else:
            h = lax.fori_loop(0, steps, lambda k, h: mix_rows(pl.multiple_of(k * n, n), h), h)
        hn_ref[s] = h

    _branch_out(tm, lambda c: gbuf[c - n_hist], ya_s, d_s, yb_s, m_s, pw_ref, ps_ref, wa_ref, wb_ref, wo_ref,
                x_ref, y_ref)


def _mixer_stream_kernel(tm, x_ref, convp_ref, hp_ref, poolp_ref, g1_ref, win_ref, cw_ref, cb_ref, wri_ref,
                         br_ref, bi_ref, lam_ref, pw_ref, ps_ref, wa_ref, wb_ref, wo_ref,
                         y_ref, convn_ref, hn_ref, pooln_ref,
                         xn_s, hist, gbuf, cbuf, cbf, ya_s, d_s, yb_s, m_s, hcar):
    n_hist = GATE_A_CH

    @pl.when(pl.program_id(1) == 0)
    def _():
        for c in range(POOL_CH):
            cs = slice(c * NCH, (c + 1) * NCH)
            hist[CONV_CH + c] = jnp.zeros((HDR, NCH), f32)
            hist[POOL_CH + c] = jnp.zeros((HDR, NCH), f32)
            hist[CONV_CH + c, HDR - (CONV_W - 1):HDR, :] = convp_ref[0, :, cs]
            hist[POOL_CH + c, HDR - (POOL_MAX - 1):HDR, :] = poolp_ref[0, :, cs]
        hcar[0:1, :] = hp_ref[0]

    _norm_to_bf16(x_ref, g1_ref, xn_s, tm)
    lam_c = _lam_scaled(lam_ref)

    def project_with_history(c):
        p = _dot(xn_s[...], win_ref[c])
        ext = jnp.concatenate([hist[c], p], axis=0)
        hist[c] = p[tm - HDR:]
        return p, ext

    def conv_chunk(c):
        cs = slice(c * NCH, (c + 1) * NCH)
        p, ext = project_with_history(CONV_CH + c)
        acc = _conv_taps(ext, cw_ref, cb_ref, cs)
        cbuf[:, cs] = acc
        cbf[:, cs] = acc.astype(bf16)
        convn_ref[0, :, cs] = p[tm - (CONV_W - 1):]

    def pool_chunk(g):
        gs = slice(g * POOL_GROUP, (g + 1) * POOL_GROUP)
        p, ext = project_with_history(POOL_CH + g)
        d_s[:, gs] = _pool_diff(ext, POOL_WINDOWS[g]).astype(bf16)
        pooln_ref[0, :, gs] = p[tm - (POOL_MAX - 1):]

    def gate_chunk(c):
        gbuf[c] = _dot(xn_s[...], win_ref[n_hist + c])

    def lru_head(h):
        hs = slice(h * LRU_BLOCK, (h + 1) * LRU_BLOCK)
        ri = _dot(cbf[:, hs], wri_ref[h])
        a, b = _lru_coeffs(ri[:, :LRU_BLOCK] + br_ref[:, hs], ri[:, LRU_BLOCK:] + bi_ref[:, hs],
                           cbuf[:, hs], lam_c[:, hs])
        state = hcar[0:1, hs]
        outs = []
        for k in range(tm // SUBLANES):
            blk = slice(k * SUBLANES, (k + 1) * SUBLANES)
            outs.append(_scan_block(a[blk], b[blk], state))
            state = outs[-1][SUBLANES - 1:SUBLANES, :]
        ya_s[:, hs] = jnp.concatenate(outs, axis=0).astype(bf16)
        hcar[0:1, hs] = state
        hn_ref[0, :, hs] = state

    for c in range(POOL_CH):
        conv_chunk(c)
    others = [functools.partial(pool_chunk, g) for g in range(len(POOL_WINDOWS))]
    others += [functools.partial(gate_chunk, c) for c in range(N_IN_CH - n_hist)]
    per_head = -(-len(others) // LRU_HEADS)
    for h in range(LRU_HEADS):
        lru_head(h)
        for fn in others[h * per_head:(h + 1) * per_head]:
            fn()

    _branch_out(tm, lambda c: gbuf[c - n_hist], ya_s, d_s, yb_s, m_s, pw_ref, ps_ref, wa_ref, wb_ref, wo_ref,
                x_ref, y_ref)


def _ffn_kernel(final, tm, x_ref, g2_ref, wgu_ref, wd_ref, fg_ref, y_ref, xn_s, h_s):
    nc = _norm_chunk(tm)
    _norm_to_bf16(x_ref, g2_ref, xn_s, tm)

    for c in range(D_FF // NCH):
        gate = _dot(xn_s[...], wgu_ref[:, c * NCH:(c + 1) * NCH])
        up = _dot(xn_s[...], wgu_ref[:, D_FF + c * NCH:D_FF + (c + 1) * NCH])
        h_s[:, c * NCH:(c + 1) * NCH] = (jax.nn.silu(gate) * up).astype(bf16)

    for c in range(D_MODEL // NCH):
        cs = slice(c * NCH, (c + 1) * NCH)
        y_ref[:, cs] = x_ref[:, cs] + _dot(h_s[...], wd_ref[:, cs])

    if final:
        def final_rows(r0):
            rows = pl.ds(r0, nc)
            y_ref[rows, :] = _rmsnorm(y_ref[rows, :], fg_ref[...])
        _for_rows(tm, nc, final_rows)


def _const_spec(shape):
    nd = len(shape)
    return pl.BlockSpec(shape, lambda *_: (0,) * nd, pipeline_mode=pl.Buffered(1))


def _mixer_weights(lw):
    return (lw['norm1_g'], lw['w_in'], lw['conv_w'], lw['conv_b'], lw['wri'], lw['b_r'], lw['b_i'],
            lw['lam'], lw['pool_w'], lw['pool_scale'], lw['w_br_a'], lw['w_br_b'], lw['w_out'])


def _state_shapes(n):
    return (jax.ShapeDtypeStruct((n, CONV_W - 1, D_LRU), f32),
            jax.ShapeDtypeStruct((n, 1, D_LRU), f32),
            jax.ShapeDtypeStruct((n, POOL_MAX - 1, D_POOL), f32))


def _state_specs(n, index_map):
    return [pl.BlockSpec((n, CONV_W - 1, D_LRU), index_map), pl.BlockSpec((n, 1, D_LRU), index_map),
            pl.BlockSpec((n, POOL_MAX - 1, D_POOL), index_map)]


def _mixer_step_call(x, conv_prev, h_prev, pool_prev, lw, segs, name):
    tm = x.shape[0]
    ns = len(segs)
    max_len = max(length for _, length, _ in segs)
    weights = _mixer_weights(lw)
    x_spec = pl.BlockSpec((tm, D_MODEL), lambda i: (0, 0))
    whole = lambda i: (0, 0, 0)
    scratch = [
        pltpu.VMEM((tm, D_MODEL), bf16),
        pltpu.VMEM((ns, GATE_A_CH, HDR + max_len, NCH), f32),
        pltpu.VMEM((N_IN_CH - GATE_A_CH, tm, NCH), f32),
        pltpu.VMEM((tm, D_LRU), f32),
        pltpu.VMEM((tm, D_LRU), bf16),
        pltpu.VMEM((tm, D_LRU), f32),
        pltpu.VMEM((tm, D_LRU), f32),
        pltpu.VMEM((tm, D_LRU), bf16),
        pltpu.VMEM((tm, D_POOL), bf16),
        pltpu.VMEM((tm, D_POOL), bf16),
        pltpu.VMEM((tm, D_MODEL), bf16),
    ]
    return pl.pallas_call(
        functools.partial(_mixer_step_kernel, tuple(segs), tm),
        grid=(1,),
        in_specs=[x_spec] + _state_specs(ns, whole) + [_const_spec(w.shape) for w in weights],
        out_specs=[x_spec] + _state_specs(ns, whole),
        out_shape=(jax.ShapeDtypeStruct(x.shape, f32),) + _state_shapes(ns),
        scratch_shapes=scratch,
        compiler_params=pltpu.CompilerParams(dimension_semantics=("arbitrary",), vmem_limit_bytes=VMEM_LIMIT),
        name=name,
    )(x, conv_prev, h_prev, pool_prev, *weights)


def _mixer_stream_call(x, conv_init, h_init, pool_init, init_idx, lw, tm, name):
    nb, t, _ = x.shape
    nt = t // tm
    assert nt * tm == t
    weights = _mixer_weights(lw)
    x_spec = pl.BlockSpec((None, tm, D_MODEL), lambda b, i: (b, i, 0))
    scratch = [
        pltpu.VMEM((tm, D_MODEL), bf16),
        pltpu.VMEM((GATE_A_CH, HDR, NCH), f32),
        pltpu.VMEM((N_IN_CH - GATE_A_CH, tm, NCH), f32),
        pltpu.VMEM((tm, D_LRU), f32),
        pltpu.VMEM((tm, D_LRU), bf16),
        pltpu.VMEM((tm, D_LRU), bf16),
        pltpu.VMEM((tm, D_POOL), bf16),
        pltpu.VMEM((tm, D_POOL), bf16),
        pltpu.VMEM((tm, D_MODEL), bf16),
        pltpu.VMEM((SUBLANES, D_LRU), f32),
    ]
    return pl.pallas_call(
        functools.partial(_mixer_stream_kernel, tm),
        grid=(nb, nt),
        in_specs=[x_spec] + _state_specs(1, lambda b, i: (init_idx, 0, 0)) + [_const_spec(w.shape) for w in weights],
        out_specs=[x_spec] + _state_specs(1, lambda b, i: (b, 0, 0)),
        out_shape=(jax.ShapeDtypeStruct(x.shape, f32),) + _state_shapes(nb),
        scratch_shapes=scratch,
        compiler_params=pltpu.CompilerParams(
            dimension_semantics=("arbitrary", "arbitrary"), vmem_limit_bytes=VMEM_LIMIT),
        name=name,
    )(x, conv_init, h_init, pool_init, *weights)


def _ffn_call(x, lw, final_g, final, tm, name):
    nb, t, _ = x.shape
    nt = t // tm
    assert nt * tm == t
    x_spec = pl.BlockSpec((None, tm, D_MODEL), lambda b, i: (b, i, 0))
    weights = (lw['norm2_g'], lw['w_gu'], lw['w_down'], final_g)
    return pl.pallas_call(
        functools.partial(_ffn_kernel, final, tm),
        grid=(nb, nt),
        in_specs=[x_spec] + [_const_spec(w.shape) for w in weights],
        out_specs=x_spec,
        out_shape=jax.ShapeDtypeStruct(x.shape, f32),
        scratch_shapes=[pltpu.VMEM((tm, D_MODEL), bf16), pltpu.VMEM((tm, D_FF), bf16)],
        compiler_params=pltpu.CompilerParams(
            dimension_semantics=("arbitrary", "arbitrary"), vmem_limit_bytes=VMEM_LIMIT),
        name=name,
    )(x, *weights)


def kernel(x_prompt, x_sample, state_conv, state_lru, state_pool, meta_tokens, norm1_g, w_in, conv_w, conv_b,
           lam, w_r, b_r, w_i, b_i, pool_w, pool_scale, w_br_a, w_br_b, w_out, norm2_g, w_gu, w_down, final_g):
    depth = w_in.shape[0]
    ns, ts, _ = x_sample.shape
    row = lambda v: v.reshape(1, -1).astype(f32)

    segs_s = [(s * ts, ts, PAST_LEN) for s in range(ns)] + [(ns * ts, N_META, 0)]
    tm_s = ns * ts + N_META
    xs = jnp.concatenate([x_sample.reshape(ns * ts, D_MODEL), meta_tokens.astype(f32)], axis=0)
    xp = x_prompt
    fg = row(final_g)

    outs = {k: [] for k in ('cp', 'hp', 'pp', 'cs', 'hs', 'ps')}
    for l in range(depth):
        lw = {
            'norm1_g': row(norm1_g[l]),
            'w_in': w_in[l].astype(bf16).reshape(D_MODEL, N_IN_CH, NCH).transpose(1, 0, 2),
            'conv_w': conv_w[l], 'conv_b': row(conv_b[l]),
            'wri': jnp.concatenate([w_r[l], w_i[l]], axis=-1).astype(bf16),
            'b_r': row(b_r[l]), 'b_i': row(b_i[l]), 'lam': row(lam[l]),
            'pool_w': pool_w[l].astype(bf16), 'pool_scale': row(pool_scale[l]),
            'w_br_a': w_br_a[l].astype(bf16), 'w_br_b': w_br_b[l].astype(bf16), 'w_out': w_out[l].astype(bf16),
            'norm2_g': row(norm2_g[l]), 'w_gu': w_gu[l].astype(bf16), 'w_down': w_down[l].astype(bf16),
        }
        last = l == depth - 1
        zeros = lambda rows, width: jnp.zeros((1, rows, width), f32)
        conv_prev = jnp.concatenate([state_conv[l], zeros(CONV_W - 1, D_LRU)], axis=0)
        h_prev = jnp.concatenate([state_lru[l][:, None, :], zeros(1, D_LRU)], axis=0)
        pool_prev = jnp.concatenate([state_pool[l], zeros(POOL_MAX - 1, D_POOL)], axis=0)

        xs, c_s, h_s, p_s = _mixer_step_call(xs, conv_prev, h_prev, pool_prev, lw, segs_s, f"mixer_step_l{l}")
        xs = _ffn_call(xs[None], lw, fg, last, tm_s, f"ffn_step_l{l}")[0]
        xp, c_p, h_p, p_p = _mixer_stream_call(xp, c_s, h_s, p_s, ns, lw, TM_PROMPT, f"mixer_prompt_l{l}")
        xp = _ffn_call(xp, lw, fg, last, TM_PROMPT, f"ffn_prompt_l{l}")

        outs['cp'].append(c_p); outs['hp'].append(h_p[:, 0]); outs['pp'].append(p_p)
        outs['cs'].append(c_s[:ns]); outs['hs'].append(h_s[:ns, 0]); outs['ps'].append(p_s[:ns])

    y_sample = xs[:ns * ts].reshape(ns, ts, D_MODEL)
    return (xp, y_sample, jnp.stack(outs['cp']), jnp.stack(outs['hp']), jnp.stack(outs['pp']),
            jnp.stack(outs['cs']), jnp.stack(outs['hs']), jnp.stack(outs['ps']))
```

```python
import functools

import jax
import jax.numpy as jnp
from jax import lax
from jax.experimental import pallas as pl
from jax.experimental.pallas import tpu as pltpu

D_MODEL = 1024
D_LRU = 1024
D_POOL = 1024
LRU_HEADS = 8
LRU_BLOCK = D_LRU // LRU_HEADS
LRU_C = 8.0
CONV_W = 4
POOL_WINDOWS = (2, 4, 8, 16)
POOL_GROUP = D_POOL // len(POOL_WINDOWS)
POOL_MAX = 16
D_FF = 2816
N_META = 16
PAST_LEN = 2048
EPS = 1e-6

SUBLANES = 8
BF16_ROWS = 16
HDR = 16
NCH = 256
N_IN_CH = 4 * D_MODEL // NCH
CONV_CH = 0
POOL_CH = D_LRU // NCH
GATE_A_CH = (D_LRU + D_POOL) // NCH
GATE_B_CH = (D_LRU + D_POOL + D_MODEL) // NCH
N_OUT_CH = D_MODEL // NCH
TM_PROMPT = 512
ROW_CH = 32
VMEM_LIMIT = 60 * 1024 * 1024

f32 = jnp.float32
bf16 = jnp.bfloat16

assert POOL_GROUP == NCH and HDR >= POOL_MAX - 1 and HDR % BF16_ROWS == 0


def _dot(a, b):
    return jnp.dot(a, b, preferred_element_type=f32)


def _chunk(c):
    return slice(c * NCH, (c + 1) * NCH)


def _rmsnorm(x, g):
    y = x * lax.rsqrt(jnp.mean(x * x, axis=-1, keepdims=True) + EPS)
    return y * g


def _norm_chunk(tm):
    return max(c for c in range(BF16_ROWS, 129, BF16_ROWS) if tm % c == 0)


def _sigmoid(x):
    return 0.5 * jnp.tanh(0.5 * x) + 0.5


def _shift_rows(x, k):
    return pltpu.roll(x, k, axis=0)


def _for_rows(total, chunk, fn):
    n = total // chunk
    assert n * chunk == total
    if n == 1:
        fn(0)
    else:
        def body(k, c):
            fn(pl.multiple_of(k * chunk, chunk))
            return c
        lax.fori_loop(0, n, body, 0)


def _norm_to_bf16(x_ref, g_ref, xn_s, tm):
    nc = _norm_chunk(tm)

    def norm_rows(r0):
        rows = pl.ds(r0, nc)
        xn_s[rows, :] = _rmsnorm(x_ref[rows, :], g_ref[...]).astype(bf16)
    _for_rows(tm, nc, norm_rows)


def _conv_taps(blk, cw_ref, cb_ref, cs):
    tap = lambda k: _shift_rows(blk, CONV_W - 1 - k)[HDR:] if k < CONV_W - 1 else blk[HDR:]
    acc = cb_ref[:, cs] + cw_ref[0:1, cs] * tap(0)
    for k in range(1, CONV_W):
        acc = acc + cw_ref[k:k + 1, cs] * tap(k)
    return acc


def _lam_scaled(lam_ref):
    lam = lam_ref[...]
    log_sig_lam = jnp.minimum(lam, 0.0) - jnp.log1p(jnp.exp(-jnp.abs(lam)))
    return LRU_C * log_sig_lam


def _lru_coeffs(r_pre, i_pre, c, lam_c, start_mask=None):
    r = _sigmoid(r_pre)
    i = _sigmoid(i_pre)
    log_a = r * lam_c
    a = jnp.exp(log_a)
    th = jnp.tanh(log_a)
    mult = jnp.sqrt(-2.0 * th / (1.0 - th))
    if start_mask is not None:
        mult = jnp.where(start_mask, 1.0, mult)
    return a, mult * (i * c)


def _scan_block(a, b, h):
    sub = lax.broadcasted_iota(jnp.int32, a.shape, 0)
    for d in (1, 2, 4):
        keep = sub >= d
        a_prev = pltpu.roll(a, d, axis=0)
        b_prev = pltpu.roll(b, d, axis=0)
        b = jnp.where(keep, a * b_prev, 0.0) + b
        a = jnp.where(keep, a * a_prev, a)
    return a * h + b


def _scan_rows(a, b, h):
    outs = []
    for k in range(a.shape[0] // SUBLANES):
        blk = slice(k * SUBLANES, (k + 1) * SUBLANES)
        outs.append(_scan_block(a[blk], b[blk], h))
        h = outs[-1][SUBLANES - 1:SUBLANES, :]
    return jnp.concatenate(outs, axis=0), h


def _pool_diff(blk, w, cnt=None):
    xb = blk[HDR:]
    tot, span = blk, 1
    while span < w:
        tot = tot + _shift_rows(tot, span)
        span *= 2
    tot = tot[HDR:]
    pooled = tot * (1.0 / w) if cnt is None else tot / cnt
    return pooled - xb


def _pool_project(g, d_s, yb_s, pw_ref, ps_ref):
    gs = _chunk(g)
    yb_s[:, gs] = (_dot(d_s[:, gs], pw_ref[g]) * ps_ref[:, gs]).astype(bf16)


def _branch_b(c, gate_b, yb_s, wb_ref, mb_s):
    mb_s[:, _chunk(c)] = _sigmoid(gate_b) * _dot(yb_s[...], wb_ref[:, _chunk(c)])


def _merge(c, gate_a, ya_s, wa_ref, mb_s, m_s):
    cs = _chunk(c)
    m_s[:, cs] = (_sigmoid(gate_a) * _dot(ya_s[...], wa_ref[:, cs]) + mb_s[:, cs]).astype(bf16)


def _project_out(c, m_s, wo_ref, x_ref, y_ref):
    cs = _chunk(c)
    y_ref[:, cs] = x_ref[:, cs] + _dot(m_s[...], wo_ref[:, cs])


def _mixer_step_kernel(segs, tm,
                       x_ref, convp_ref, hp_ref, poolp_ref, g1_ref, win_ref, cw_ref, cb_ref, wri_ref,
                       br_ref, bi_ref, lam_ref, pw_ref, ps_ref, wa_ref, wb_ref, wo_ref,
                       y_ref, convn_ref, hn_ref, pooln_ref, convm_ref, hm_ref, poolm_ref,
                       xn_s, proj, gbuf, cbuf, cbf, a_s, b_s, ya_s, d_s, yb_s, mb_s, m_s):
    n_hist = GATE_A_CH
    assert sum(1 for seg in segs if not seg[3]) == 1
    for s, (_, _, _, has_state) in enumerate(segs):
        for c in range(n_hist):
            proj[s, c, 0:HDR, :] = jnp.zeros((HDR, NCH), f32)
        if has_state:
            for c in range(POOL_CH):
                proj[s, CONV_CH + c, HDR - (CONV_W - 1):HDR, :] = convp_ref[s, :, _chunk(c)]
                proj[s, POOL_CH + c, HDR - (POOL_MAX - 1):HDR, :] = poolp_ref[s, :, _chunk(c)]

    _norm_to_bf16(x_ref, g1_ref, xn_s, tm)

    for c in range(N_IN_CH):
        p = _dot(xn_s[...], win_ref[:, _chunk(c)])
        if c < n_hist:
            for s, (row0, length, _, _) in enumerate(segs):
                proj[s, c, HDR:HDR + length, :] = p[row0:row0 + length]
        else:
            gbuf[c - n_hist] = p

    for s, (row0, length, _, has_state) in enumerate(segs):
        n = min(ROW_CH, length)

        def conv_rows(t0, s=s, row0=row0, n=n):
            rows = pl.ds(row0 + t0, n)
            for c in range(POOL_CH):
                acc = _conv_taps(proj[s, CONV_CH + c, pl.ds(t0, n + HDR), :], cw_ref, cb_ref, _chunk(c))
                cbuf[rows, _chunk(c)] = acc
                cbf[rows, _chunk(c)] = acc.astype(bf16)
        _for_rows(length, n, conv_rows)
        conv_out, pool_out, k = (convn_ref, pooln_ref, s) if has_state else (convm_ref, poolm_ref, 0)
        for c in range(POOL_CH):
            conv_out[k, :, _chunk(c)] = proj[s, CONV_CH + c, HDR + length - (CONV_W - 1):HDR + length, :]
            pool_out[k, :, _chunk(c)] = proj[s, POOL_CH + c, HDR + length - (POOL_MAX - 1):HDR + length, :]

    for h in range(LRU_HEADS):
        hs = slice(h * LRU_BLOCK, (h + 1) * LRU_BLOCK)
        ri = _dot(cbf[:, hs], wri_ref[h])
        a_s[:, hs] = ri[:, :LRU_BLOCK] + br_ref[:, hs]
        b_s[:, hs] = ri[:, LRU_BLOCK:] + bi_ref[:, hs]
    lam_c = _lam_scaled(lam_ref)

    for s, (row0, length, pos0, has_state) in enumerate(segs):
        n = min(ROW_CH, length)

        def mix_rows(t0, h, s=s, row0=row0, n=n, pos0=pos0):
            rows = pl.ds(row0 + t0, n)
            pos = pos0 + t0 + lax.broadcasted_iota(jnp.int32, (n, 1), 0)
            start = (pos == 0) if pos0 == 0 else None
            a, b = _lru_coeffs(a_s[rows, :], b_s[rows, :], cbuf[rows, :], lam_c, start)
            states, h = _scan_rows(a, b, h)
            ya_s[rows, :] = states.astype(bf16)
            cnt = None if pos0 >= POOL_MAX - 1 else pos + 1
            for g, w in enumerate(POOL_WINDOWS):
                cnt_g = None if cnt is None else jnp.minimum(w, cnt).astype(f32)
                d_s[rows, _chunk(g)] = _pool_diff(proj[s, POOL_CH + g, pl.ds(t0, n + HDR), :], w, cnt_g).astype(bf16)
            return h

        steps = length // n
        h = hp_ref[s] if has_state else jnp.zeros((1, D_LRU), f32)
        if steps == 1:
            h = mix_rows(0, h)
        else:
            h = lax.fori_loop(0, steps, lambda k, h: mix_rows(pl.multiple_of(k * n, n), h), h)
        if has_state:
            hn_ref[s] = h
        else:
            hm_ref[0] = h

    for g in range(len(POOL_WINDOWS)):
        _pool_project(g, d_s, yb_s, pw_ref, ps_ref)
    for c in range(N_OUT_CH):
        _branch_b(c, gbuf[GATE_B_CH - n_hist + c], yb_s, wb_ref, mb_s)
        _merge(c, gbuf[GATE_A_CH - n_hist + c], ya_s, wa_ref, mb_s, m_s)
    for c in range(N_OUT_CH):
        _project_out(c, m_s, wo_ref, x_ref, y_ref)


def _mixer_stream_kernel(tm, x_ref, convp_ref, hp_ref, poolp_ref, g1_ref, win_ref, cw_ref, cb_ref, wri_ref,
                         br_ref, bi_ref, lam_ref, pw_ref, ps_ref, wa_ref, wb_ref, wo_ref,
                         y_ref, convn_ref, hn_ref, pooln_ref,
                         xn_s, hist, gbuf, cbuf, cbf, ya_s, d_s, yb_s, mb_s, m_s, hcar):
    n_hist = GATE_A_CH

    @pl.when(pl.program_id(1) == 0)
    def _():
        for c in range(POOL_CH):
            hist[CONV_CH + c] = jnp.zeros((HDR, NCH), f32)
            hist[POOL_CH + c] = jnp.zeros((HDR, NCH), f32)
            hist[CONV_CH + c, HDR - (CONV_W - 1):HDR, :] = convp_ref[0, :, _chunk(c)]
            hist[POOL_CH + c, HDR - (POOL_MAX - 1):HDR, :] = poolp_ref[0, :, _chunk(c)]
        hcar[0:1, :] = hp_ref[0]

    _norm_to_bf16(x_ref, g1_ref, xn_s, tm)
    lam_c = _lam_scaled(lam_ref)

    def project_with_history(c):
        p = _dot(xn_s[...], win_ref[:, _chunk(c)])
        ext = jnp.concatenate([hist[c], p], axis=0)
        hist[c] = p[tm - HDR:]
        return p, ext

    def conv_chunk(c):
        p, ext = project_with_history(CONV_CH + c)
        acc = _conv_taps(ext, cw_ref, cb_ref, _chunk(c))
        cbuf[:, _chunk(c)] = acc
        cbf[:, _chunk(c)] = acc.astype(bf16)
        convn_ref[0, :, _chunk(c)] = p[tm - (CONV_W - 1):]

    def pool_chunk(g):
        p, ext = project_with_history(POOL_CH + g)
        d_s[:, _chunk(g)] = _pool_diff(ext, POOL_WINDOWS[g]).astype(bf16)
        pooln_ref[0, :, _chunk(g)] = p[tm - (POOL_MAX - 1):]

    def gate_chunk(c):
        gbuf[c - n_hist] = _dot(xn_s[...], win_ref[:, _chunk(c)])

    def lru_head(h):
        hs = slice(h * LRU_BLOCK, (h + 1) * LRU_BLOCK)
        ri = _dot(cbf[:, hs], wri_ref[h])
        a, b = _lru_coeffs(ri[:, :LRU_BLOCK] + br_ref[:, hs], ri[:, LRU_BLOCK:] + bi_ref[:, hs],
                           cbuf[:, hs], lam_c[:, hs])
        states, last = _scan_rows(a, b, hcar[0:1, hs])
        ya_s[:, hs] = states.astype(bf16)
        hcar[0:1, hs] = last
        hn_ref[0, :, hs] = last

    for c in range(POOL_CH):
        conv_chunk(c)
    others = [functools.partial(pool_chunk, g) for g in range(len(POOL_WINDOWS))]
    others += [functools.partial(gate_chunk, GATE_B_CH + c) for c in range(N_OUT_CH)]
    others += [functools.partial(_pool_project, g, d_s, yb_s, pw_ref, ps_ref) for g in range(len(POOL_WINDOWS))]
    others += [lambda c=c: _branch_b(c, gbuf[GATE_B_CH - n_hist + c], yb_s, wb_ref, mb_s) for c in range(N_OUT_CH)]
    others += [functools.partial(gate_chunk, GATE_A_CH + c) for c in range(N_OUT_CH)]
    for h in range(LRU_HEADS):
        lru_head(h)
        lo, hi = (h * len(others)) // LRU_HEADS, ((h + 1) * len(others)) // LRU_HEADS
        for fn in others[lo:hi]:
            fn()

    for c in range(N_OUT_CH):
        _merge(c, gbuf[GATE_A_CH - n_hist + c], ya_s, wa_ref, mb_s, m_s)
    for c in range(N_OUT_CH):
        _project_out(c, m_s, wo_ref, x_ref, y_ref)


def _ffn_kernel(final, tm, x_ref, g2_ref, wgu_ref, wd_ref, fg_ref, y_ref, xn_s, h_s):
    nc = _norm_chunk(tm)
    _norm_to_bf16(x_ref, g2_ref, xn_s, tm)

    for c in range(D_FF // NCH):
        gate = _dot(xn_s[...], wgu_ref[:, c * NCH:(c + 1) * NCH])
        up = _dot(xn_s[...], wgu_ref[:, D_FF + c * NCH:D_FF + (c + 1) * NCH])
        h_s[:, c * NCH:(c + 1) * NCH] = (jax.nn.silu(gate) * up).astype(bf16)

    for c in range(N_OUT_CH):
        y_ref[:, _chunk(c)] = x_ref[:, _chunk(c)] + _dot(h_s[...], wd_ref[:, _chunk(c)])

    if final:
        def final_rows(r0):
            rows = pl.ds(r0, nc)
            y_ref[rows, :] = _rmsnorm(y_ref[rows, :], fg_ref[...])
        _for_rows(tm, nc, final_rows)


def _layer_spec(w, layer):
    nd = w.ndim - 1
    return pl.BlockSpec((None,) + w.shape[1:], lambda *_: (layer,) + (0,) * nd, pipeline_mode=pl.Buffered(1))


def _mixer_weights(w):
    return (w['norm1_g'], w['w_in'], w['conv_w'], w['conv_b'], w['wri'], w['b_r'], w['b_i'],
            w['lam'], w['pool_w'], w['pool_scale'], w['w_br_a'], w['w_br_b'], w['w_out'])


def _state_shapes(n):
    return (jax.ShapeDtypeStruct((n, CONV_W - 1, D_LRU), f32),
            jax.ShapeDtypeStruct((n, 1, D_LRU), f32),
            jax.ShapeDtypeStruct((n, POOL_MAX - 1, D_POOL), f32))


def _state_specs(n, index_map):
    return [pl.BlockSpec((n, CONV_W - 1, D_LRU), index_map), pl.BlockSpec((n, 1, D_LRU), index_map),
            pl.BlockSpec((n, POOL_MAX - 1, D_POOL), index_map)]


def _mixer_step_call(x, state_conv, state_lru, state_pool, w, layer, segs, name):
    tm = x.shape[0]
    ns = state_conv.shape[1]
    max_len = max(seg[1] for seg in segs)
    weights = _mixer_weights(w)
    x_spec = pl.BlockSpec((tm, D_MODEL), lambda i: (0, 0))
    whole = lambda i: (0, 0, 0)
    state_in = [pl.BlockSpec((None,) + s.shape[1:], lambda i: (layer, 0, 0, 0))
                for s in (state_conv, state_lru, state_pool)]
    scratch = [
        pltpu.VMEM((tm, D_MODEL), bf16),
        pltpu.VMEM((len(segs), GATE_A_CH, HDR + max_len, NCH), f32),
        pltpu.VMEM((N_IN_CH - GATE_A_CH, tm, NCH), f32),
        pltpu.VMEM((tm, D_LRU), f32),
        pltpu.VMEM((tm, D_LRU), bf16),
        pltpu.VMEM((tm, D_LRU), f32),
        pltpu.VMEM((tm, D_LRU), f32),
        pltpu.VMEM((tm, D_LRU), bf16),
        pltpu.VMEM((tm, D_POOL), bf16),
        pltpu.VMEM((tm, D_POOL), bf16),
        pltpu.VMEM((tm, D_MODEL), f32),
        pltpu.VMEM((tm, D_MODEL), bf16),
    ]
    return pl.pallas_call(
        functools.partial(_mixer_step_kernel, tuple(segs), tm),
        grid=(1,),
        in_specs=[x_spec] + state_in + [_layer_spec(v, layer) for v in weights],
        out_specs=[x_spec] + _state_specs(ns, whole) + _state_specs(1, whole),
        out_shape=(jax.ShapeDtypeStruct(x.shape, f32),) + _state_shapes(ns) + _state_shapes(1),
        scratch_shapes=scratch,
        compiler_params=pltpu.CompilerParams(dimension_semantics=("arbitrary",), vmem_limit_bytes=VMEM_LIMIT),
        name=name,
    )(x, state_conv, state_lru, state_pool, *weights)


def _mixer_stream_call(x, conv_init, h_init, pool_init, w, layer, tm, name):
    nb, t, _ = x.shape
    nt = t // tm
    assert nt * tm == t
    weights = _mixer_weights(w)
    x_spec = pl.BlockSpec((None, tm, D_MODEL), lambda b, i: (b, i, 0))
    scratch = [
        pltpu.VMEM((tm, D_MODEL), bf16),
        pltpu.VMEM((GATE_A_CH, HDR, NCH), f32),
        pltpu.VMEM((N_IN_CH - GATE_A_CH, tm, NCH), f32),
        pltpu.VMEM((tm, D_LRU), f32),
        pltpu.VMEM((tm, D_LRU), bf16),
        pltpu.VMEM((tm, D_LRU), bf16),
        pltpu.VMEM((tm, D_POOL), bf16),
        pltpu.VMEM((tm, D_POOL), bf16),
        pltpu.VMEM((tm, D_MODEL), f32),
        pltpu.VMEM((tm, D_MODEL), bf16),
        pltpu.VMEM((SUBLANES, D_LRU), f32),
    ]
    return pl.pallas_call(
        functools.partial(_mixer_stream_kernel, tm),
        grid=(nb, nt),
        in_specs=[x_spec] + _state_specs(1, lambda b, i: (0, 0, 0)) + [_layer_spec(v, layer) for v in weights],
        out_specs=[x_spec] + _state_specs(1, lambda b, i: (b, 0, 0)),
        out_shape=(jax.ShapeDtypeStruct(x.shape, f32),) + _state_shapes(nb),
        scratch_shapes=scratch,
        compiler_params=pltpu.CompilerParams(
            dimension_semantics=("arbitrary", "arbitrary"), vmem_limit_bytes=VMEM_LIMIT),
        name=name,
    )(x, conv_init, h_init, pool_init, *weights)


def _ffn_call(x, w, layer, final, tm, name):
    nb, t, _ = x.shape
    nt = t // tm
    assert nt * tm == t
    x_spec = pl.BlockSpec((None, tm, D_MODEL), lambda b, i: (b, i, 0))
    weights = (w['norm2_g'], w['w_gu'], w['w_down'])
    fg_spec = pl.BlockSpec(w['final_g'].shape, lambda b, i: (0, 0))
    return pl.pallas_call(
        functools.partial(_ffn_kernel, final, tm),
        grid=(nb, nt),
        in_specs=[x_spec] + [_layer_spec(v, layer) for v in weights] + [fg_spec],
        out_specs=x_spec,
        out_shape=jax.ShapeDtypeStruct(x.shape, f32),
        scratch_shapes=[pltpu.VMEM((tm, D_MODEL), bf16), pltpu.VMEM((tm, D_FF), bf16)],
        compiler_params=pltpu.CompilerParams(
            dimension_semantics=("arbitrary", "arbitrary"), vmem_limit_bytes=VMEM_LIMIT),
        name=name,
    )(x, *weights, w['final_g'])


def kernel(x_prompt, x_sample, state_conv, state_lru, state_pool, meta_tokens, norm1_g, w_in, conv_w, conv_b,
           lam, w_r, b_r, w_i, b_i, pool_w, pool_scale, w_br_a, w_br_b, w_out, norm2_g, w_gu, w_down, final_g):
    depth = w_in.shape[0]
    ns, ts, _ = x_sample.shape
    rows = lambda v: v.reshape(depth, 1, -1).astype(f32)
    w = {
        'norm1_g': rows(norm1_g), 'w_in': w_in.astype(bf16), 'conv_w': conv_w, 'conv_b': rows(conv_b),
        'wri': jnp.concatenate([w_r, w_i], axis=-1).astype(bf16),
        'b_r': rows(b_r), 'b_i': rows(b_i), 'lam': rows(lam),
        'pool_w': pool_w.astype(bf16), 'pool_scale': rows(pool_scale),
        'w_br_a': w_br_a.astype(bf16), 'w_br_b': w_br_b.astype(bf16), 'w_out': w_out.astype(bf16),
        'norm2_g': rows(norm2_g), 'w_gu': w_gu.astype(bf16), 'w_down': w_down.astype(bf16),
        'final_g': final_g.reshape(1, -1).astype(f32),
    }

    segs_s = [(s * ts, ts, PAST_LEN, True) for s in range(ns)] + [(ns * ts, N_META, 0, False)]
    tm_s = ns * ts + N_META
    xs = jnp.concatenate([x_sample.reshape(ns * ts, D_MODEL), meta_tokens.astype(f32)], axis=0)
    xp = x_prompt
    state_lru = state_lru[:, :, None, :]

    outs = {k: [] for k in ('cp', 'hp', 'pp', 'cs', 'hs', 'ps')}
    for l in range(depth):
        last = l == depth - 1
        xs, c_s, h_s, p_s, c_m, h_m, p_m = _mixer_step_call(xs, state_conv, state_lru, state_pool, w, l, segs_s,
                                                            f"mixer_step_l{l}")
        xs = _ffn_call(xs[None], w, l, last, tm_s, f"ffn_step_l{l}")[0]
        xp, c_p, h_p, p_p = _mixer_stream_call(xp, c_m, h_m, p_m, w, l, TM_PROMPT, f"mixer_prompt_l{l}")
        xp = _ffn_call(xp, w, l, last, TM_PROMPT, f"ffn_prompt_l{l}")

        outs['cp'].append(c_p); outs['hp'].append(h_p[:, 0]); outs['pp'].append(p_p)
        outs['cs'].append(c_s); outs['hs'].append(h_s[:, 0]); outs['ps'].append(p_s)

    y_sample = xs[:ns * ts].reshape(ns, ts, D_MODEL)
    return (xp, y_sample, jnp.stack(outs['cp']), jnp.stack(outs['hp']), jnp.stack(outs['pp']),
            jnp.stack(outs['cs']), jnp.stack(outs['hs']), jnp.stack(outs['ps']))
```

```python
import functools

import jax
import jax.numpy as jnp
from jax import lax
from jax.experimental import pallas as pl
from jax.experimental.pallas import tpu as pltpu

D_MODEL = 1024
D_LRU = 1024
D_POOL = 1024
LRU_HEADS = 8
LRU_BLOCK = D_LRU // LRU_HEADS
LRU_C = 8.0
CONV_W = 4
POOL_WINDOWS = (2, 4, 8, 16)
POOL_GROUP = D_POOL // len(POOL_WINDOWS)
POOL_MAX = 16
D_FF = 2816
N_META = 16
PAST_LEN = 2048
EPS = 1e-6

SUBLANES = 8
BF16_ROWS = 16
HDR = 16
NCH = 256
N_IN_CH = 4 * D_MODEL // NCH
CONV_CH = 0
POOL_CH = D_LRU // NCH
GATE_A_CH = (D_LRU + D_POOL) // NCH
GATE_B_CH = (D_LRU + D_POOL + D_MODEL) // NCH
N_OUT_CH = D_MODEL // NCH
TM_PROMPT = 512
TM_LAYER = 256
ROW_CH = 32
VMEM_LIMIT = 60 * 1024 * 1024

f32 = jnp.float32
bf16 = jnp.bfloat16

assert POOL_GROUP == NCH and HDR >= POOL_MAX - 1 and HDR % BF16_ROWS == 0


def _dot(a, b):
    return jnp.dot(a, b, preferred_element_type=f32)


def _chunk(c):
    return slice(c * NCH, (c + 1) * NCH)


def _rmsnorm(x, g):
    y = x * lax.rsqrt(jnp.mean(x * x, axis=-1, keepdims=True) + EPS)
    return y * g


def _norm_chunk(tm):
    return max(c for c in range(BF16_ROWS, 129, BF16_ROWS) if tm % c == 0)


def _sigmoid(x):
    return 0.5 * jnp.tanh(0.5 * x) + 0.5


def _shift_rows(x, k):
    return pltpu.roll(x, k, axis=0)


def _for_rows(total, chunk, fn):
    n = total // chunk
    assert n * chunk == total
    if n == 1:
        fn(0)
    else:
        def body(k, c):
            fn(pl.multiple_of(k * chunk, chunk))
            return c
        lax.fori_loop(0, n, body, 0)


def _norm_to_bf16(x_ref, g_ref, xn_s, tm):
    nc = _norm_chunk(tm)

    def norm_rows(r0):
        rows = pl.ds(r0, nc)
        xn_s[rows, :] = _rmsnorm(x_ref[rows, :], g_ref[...]).astype(bf16)
    _for_rows(tm, nc, norm_rows)


def _conv_taps(blk, cw_ref, cb_ref, cs):
    tap = lambda k: _shift_rows(blk, CONV_W - 1 - k)[HDR:] if k < CONV_W - 1 else blk[HDR:]
    acc = cb_ref[:, cs] + cw_ref[0:1, cs] * tap(0)
    for k in range(1, CONV_W):
        acc = acc + cw_ref[k:k + 1, cs] * tap(k)
    return acc


def _lam_scaled(lam_ref):
    lam = lam_ref[...]
    log_sig_lam = jnp.minimum(lam, 0.0) - jnp.log1p(jnp.exp(-jnp.abs(lam)))
    return LRU_C * log_sig_lam


def _lru_coeffs(r_pre, i_pre, c, lam_c, start_mask=None):
    r = _sigmoid(r_pre)
    i = _sigmoid(i_pre)
    log_a = r * lam_c
    a = jnp.exp(log_a)
    th = jnp.tanh(log_a)
    mult = jnp.sqrt(-2.0 * th / (1.0 - th))
    if start_mask is not None:
        mult = jnp.where(start_mask, 1.0, mult)
    return a, mult * (i * c)


def _scan_block(a, b, h):
    sub = lax.broadcasted_iota(jnp.int32, a.shape, 0)
    for d in (1, 2, 4):
        keep = sub >= d
        a_prev = pltpu.roll(a, d, axis=0)
        b_prev = pltpu.roll(b, d, axis=0)
        b = jnp.where(keep, a * b_prev, 0.0) + b
        a = jnp.where(keep, a * a_prev, a)
    return a * h + b


def _scan_rows(a, b, h):
    outs = []
    for k in range(a.shape[0] // SUBLANES):
        blk = slice(k * SUBLANES, (k + 1) * SUBLANES)
        outs.append(_scan_block(a[blk], b[blk], h))
        h = outs[-1][SUBLANES - 1:SUBLANES, :]
    return jnp.concatenate(outs, axis=0), h


def _pool_diff(blk, w, cnt=None):
    xb = blk[HDR:]
    tot, span = blk, 1
    while span < w:
        tot = tot + _shift_rows(tot, span)
        span *= 2
    tot = tot[HDR:]
    pooled = tot * (1.0 / w) if cnt is None else tot / cnt
    return pooled - xb


def _pool_project(g, d_s, yb_s, pw_ref, ps_ref):
    gs = _chunk(g)
    yb_s[:, gs] = (_dot(d_s[:, gs], pw_ref[g]) * ps_ref[:, gs]).astype(bf16)


def _branch_b(c, gate_b, yb_s, wb_ref, mb_s):
    mb_s[:, _chunk(c)] = _sigmoid(gate_b) * _dot(yb_s[...], wb_ref[:, _chunk(c)])


def _merge(c, gate_a, ya_s, wa_ref, mb_s, m_s):
    cs = _chunk(c)
    m_s[:, cs] = (_sigmoid(gate_a) * _dot(ya_s[...], wa_ref[:, cs]) + mb_s[:, cs]).astype(bf16)


def _project_out(c, m_s, wo_ref, x_ref, y_ref):
    cs = _chunk(c)
    y_ref[:, cs] = x_ref[:, cs] + _dot(m_s[...], wo_ref[:, cs])


def _mixer_step_kernel(segs, tm,
                       x_ref, convp_ref, hp_ref, poolp_ref, g1_ref, win_ref, cw_ref, cb_ref, wri_ref,
                       br_ref, bi_ref, lam_ref, pw_ref, ps_ref, wa_ref, wb_ref, wo_ref,
                       y_ref, convn_ref, hn_ref, pooln_ref, convm_ref, hm_ref, poolm_ref,
                       xn_s, proj, gbuf, cbuf, cbf, a_s, b_s, ya_s, d_s, yb_s, mb_s, m_s):
    n_hist = GATE_A_CH
    assert sum(1 for seg in segs if not seg[3]) == 1
    for s, (_, _, _, has_state) in enumerate(segs):
        for c in range(n_hist):
            proj[s, c, 0:HDR, :] = jnp.zeros((HDR, NCH), f32)
        if has_state:
            for c in range(POOL_CH):
                proj[s, CONV_CH + c, HDR - (CONV_W - 1):HDR, :] = convp_ref[s, :, _chunk(c)]
                proj[s, POOL_CH + c, HDR - (POOL_MAX - 1):HDR, :] = poolp_ref[s, :, _chunk(c)]

    _norm_to_bf16(x_ref, g1_ref, xn_s, tm)

    for c in range(N_IN_CH):
        p = _dot(xn_s[...], win_ref[:, _chunk(c)])
        if c < n_hist:
            for s, (row0, length, _, _) in enumerate(segs):
                proj[s, c, HDR:HDR + length, :] = p[row0:row0 + length]
        else:
            gbuf[c - n_hist] = p

    for s, (row0, length, _, has_state) in enumerate(segs):
        n = min(ROW_CH, length)

        def conv_rows(t0, s=s, row0=row0, n=n):
            rows = pl.ds(row0 + t0, n)
            for c in range(POOL_CH):
                acc = _conv_taps(proj[s, CONV_CH + c, pl.ds(t0, n + HDR), :], cw_ref, cb_ref, _chunk(c))
                cbuf[rows, _chunk(c)] = acc
                cbf[rows, _chunk(c)] = acc.astype(bf16)
        _for_rows(length, n, conv_rows)
        conv_out, pool_out, k = (convn_ref, pooln_ref, s) if has_state else (convm_ref, poolm_ref, 0)
        for c in range(POOL_CH):
            conv_out[k, :, _chunk(c)] = proj[s, CONV_CH + c, HDR + length - (CONV_W - 1):HDR + length, :]
            pool_out[k, :, _chunk(c)] = proj[s, POOL_CH + c, HDR + length - (POOL_MAX - 1):HDR + length, :]

    for h in range(LRU_HEADS):
        hs = slice(h * LRU_BLOCK, (h + 1) * LRU_BLOCK)
        ri = _dot(cbf[:, hs], wri_ref[h])
        a_s[:, hs] = ri[:, :LRU_BLOCK] + br_ref[:, hs]
        b_s[:, hs] = ri[:, LRU_BLOCK:] + bi_ref[:, hs]
    lam_c = _lam_scaled(lam_ref)

    for s, (row0, length, pos0, has_state) in enumerate(segs):
        n = min(ROW_CH, length)

        def mix_rows(t0, h, s=s, row0=row0, n=n, pos0=pos0):
            rows = pl.ds(row0 + t0, n)
            pos = pos0 + t0 + lax.broadcasted_iota(jnp.int32, (n, 1), 0)
            start = (pos == 0) if pos0 == 0 else None
            a, b = _lru_coeffs(a_s[rows, :], b_s[rows, :], cbuf[rows, :], lam_c, start)
            states, h = _scan_rows(a, b, h)
            ya_s[rows, :] = states.astype(bf16)
            cnt = None if pos0 >= POOL_MAX - 1 else pos + 1
            for g, w in enumerate(POOL_WINDOWS):
                cnt_g = None if cnt is None else jnp.minimum(w, cnt).astype(f32)
                d_s[rows, _chunk(g)] = _pool_diff(proj[s, POOL_CH + g, pl.ds(t0, n + HDR), :], w, cnt_g).astype(bf16)
            return h

        steps = length // n
        h = hp_ref[s] if has_state else jnp.zeros((1, D_LRU), f32)
        if steps == 1:
            h = mix_rows(0, h)
        else:
            h = lax.fori_loop(0, steps, lambda k, h: mix_rows(pl.multiple_of(k * n, n), h), h)
        if has_state:
            hn_ref[s] = h
        else:
            hm_ref[0] = h

    for g in range(len(POOL_WINDOWS)):
        _pool_project(g, d_s, yb_s, pw_ref, ps_ref)
    for c in range(N_OUT_CH):
        _branch_b(c, gbuf[GATE_B_CH - n_hist + c], yb_s, wb_ref, mb_s)
        _merge(c, gbuf[GATE_A_CH - n_hist + c], ya_s, wa_ref, mb_s, m_s)
    for c in range(N_OUT_CH):
        _project_out(c, m_s, wo_ref, x_ref, y_ref)


def _mixer_stream_kernel(tm, x_ref, convp_ref, hp_ref, poolp_ref, g1_ref, win_ref, cw_ref, cb_ref, wri_ref,
                         br_ref, bi_ref, lam_ref, pw_ref, ps_ref, wa_ref, wb_ref, wo_ref,
                         y_ref, convn_ref, hn_ref, pooln_ref,
                         xn_s, hist, gbuf, cbuf, cbf, ya_s, d_s, yb_s, mb_s, m_s, hcar):
    n_hist = GATE_A_CH

    @pl.when(pl.program_id(1) == 0)
    def _():
        for c in range(POOL_CH):
            hist[CONV_CH + c] = jnp.zeros((HDR, NCH), f32)
            hist[POOL_CH + c] = jnp.zeros((HDR, NCH), f32)
            hist[CONV_CH + c, HDR - (CONV_W - 1):HDR, :] = convp_ref[0, :, _chunk(c)]
            hist[POOL_CH + c, HDR - (POOL_MAX - 1):HDR, :] = poolp_ref[0, :, _chunk(c)]
        hcar[0:1, :] = hp_ref[0]

    _norm_to_bf16(x_ref, g1_ref, xn_s, tm)
    lam_c = _lam_scaled(lam_ref)

    def project_with_history(c):
        p = _dot(xn_s[...], win_ref[:, _chunk(c)])
        ext = jnp.concatenate([hist[c], p], axis=0)
        hist[c] = p[tm - HDR:]
        return p, ext

    def conv_chunk(c):
        p, ext = project_with_history(CONV_CH + c)
        acc = _conv_taps(ext, cw_ref, cb_ref, _chunk(c))
        cbuf[:, _chunk(c)] = acc
        cbf[:, _chunk(c)] = acc.astype(bf16)
        convn_ref[0, :, _chunk(c)] = p[tm - (CONV_W - 1):]

    def pool_chunk(g):
        p, ext = project_with_history(POOL_CH + g)
        d_s[:, _chunk(g)] = _pool_diff(ext, POOL_WINDOWS[g]).astype(bf16)
        pooln_ref[0, :, _chunk(g)] = p[tm - (POOL_MAX - 1):]

    def gate_chunk(c):
        gbuf[c - n_hist] = _dot(xn_s[...], win_ref[:, _chunk(c)])

    def lru_head(h):
        hs = slice(h * LRU_BLOCK, (h + 1) * LRU_BLOCK)
        ri = _dot(cbf[:, hs], wri_ref[h])
        a, b = _lru_coeffs(ri[:, :LRU_BLOCK] + br_ref[:, hs], ri[:, LRU_BLOCK:] + bi_ref[:, hs],
                           cbuf[:, hs], lam_c[:, hs])
        states, last = _scan_rows(a, b, hcar[0:1, hs])
        ya_s[:, hs] = states.astype(bf16)
        hcar[0:1, hs] = last
        hn_ref[0, :, hs] = last

    for c in range(POOL_CH):
        conv_chunk(c)
    others = [functools.partial(pool_chunk, g) for g in range(len(POOL_WINDOWS))]
    others += [functools.partial(gate_chunk, GATE_B_CH + c) for c in range(N_OUT_CH)]
    others += [functools.partial(_pool_project, g, d_s, yb_s, pw_ref, ps_ref) for g in range(len(POOL_WINDOWS))]
    others += [lambda c=c: _branch_b(c, gbuf[GATE_B_CH - n_hist + c], yb_s, wb_ref, mb_s) for c in range(N_OUT_CH)]
    others += [functools.partial(gate_chunk, GATE_A_CH + c) for c in range(N_OUT_CH)]
    for h in range(LRU_HEADS):
        lru_head(h)
        lo, hi = (h * len(others)) // LRU_HEADS, ((h + 1) * len(others)) // LRU_HEADS
        for fn in others[lo:hi]:
            fn()

    for c in range(N_OUT_CH):
        _merge(c, gbuf[GATE_A_CH - n_hist + c], ya_s, wa_ref, mb_s, m_s)
    for c in range(N_OUT_CH):
        _project_out(c, m_s, wo_ref, x_ref, y_ref)


def _layer_stream_kernel(tm, tiles_per_stream, n_tiles, final,
                         x_ref, convp_ref, hp_ref, poolp_ref, g1_ref, win_ref, cw_ref, cb_ref, wri_ref,
                         br_ref, bi_ref, lam_ref, pw_ref, ps_ref, wa_ref, wb_ref, wo_ref,
                         g2_ref, wgu_ref, wd_ref, fg_ref,
                         y_ref, convn_ref, hn_ref, pooln_ref,
                         xn_s, hist, gbuf, cbuf, cbf, ya_s, d_s, yb_s, mb_s, m_s, hcar, ymid, xn2_s, h_s):
    j = pl.program_id(0)
    new = j % 2
    cur = 1 - new
    n_hist = GATE_A_CH
    nc = _norm_chunk(tm)

    @pl.when(j == 0)
    def _():
        hist[...] = jnp.zeros(hist.shape, f32)
        hcar[...] = jnp.zeros(hcar.shape, f32)
        ymid[1] = jnp.zeros((tm, D_MODEL), f32)

    for c in range(POOL_CH):
        convn_ref[0, :, _chunk(c)] = hist[CONV_CH + c, HDR - (CONV_W - 1):HDR, :]
        pooln_ref[0, :, _chunk(c)] = hist[POOL_CH + c, HDR - (POOL_MAX - 1):HDR, :]
    hn_ref[0] = hcar[0:1, :]

    @pl.when(jnp.logical_and(j % tiles_per_stream == 0, j < n_tiles))
    def _():
        for c in range(POOL_CH):
            hist[CONV_CH + c] = jnp.zeros((HDR, NCH), f32)
            hist[POOL_CH + c] = jnp.zeros((HDR, NCH), f32)
            hist[CONV_CH + c, HDR - (CONV_W - 1):HDR, :] = convp_ref[0, :, _chunk(c)]
            hist[POOL_CH + c, HDR - (POOL_MAX - 1):HDR, :] = poolp_ref[0, :, _chunk(c)]
        hcar[0:1, :] = hp_ref[0]

    lam_c = _lam_scaled(lam_ref)

    def ffn_norm():
        for r0 in range(0, tm, nc):
            xn2_s[r0:r0 + nc, :] = _rmsnorm(ymid[cur, r0:r0 + nc, :], g2_ref[...]).astype(bf16)

    def ffn_up(c):
        gate = _dot(xn2_s[...], wgu_ref[:, _chunk(c)])
        up = _dot(xn2_s[...], wgu_ref[:, D_FF + c * NCH:D_FF + (c + 1) * NCH])
        h_s[:, _chunk(c)] = (jax.nn.silu(gate) * up).astype(bf16)

    def ffn_down(c):
        y_ref[:, _chunk(c)] = ymid[cur, :, _chunk(c)] + _dot(h_s[...], wd_ref[:, _chunk(c)])

    def ffn_final_norm():
        for r0 in range(0, tm, nc):
            y_ref[r0:r0 + nc, :] = _rmsnorm(y_ref[r0:r0 + nc, :], fg_ref[...])

    def mixer_norm():
        for r0 in range(0, tm, nc):
            xn_s[r0:r0 + nc, :] = _rmsnorm(x_ref[r0:r0 + nc, :], g1_ref[...]).astype(bf16)

    def project_with_history(c):
        p = _dot(xn_s[...], win_ref[:, _chunk(c)])
        ext = jnp.concatenate([hist[c], p], axis=0)
        hist[c] = p[tm - HDR:]
        return ext

    def conv_chunk(c):
        acc = _conv_taps(project_with_history(CONV_CH + c), cw_ref, cb_ref, _chunk(c))
        cbuf[:, _chunk(c)] = acc
        cbf[:, _chunk(c)] = acc.astype(bf16)

    def pool_chunk(g):
        d_s[:, _chunk(g)] = _pool_diff(project_with_history(POOL_CH + g), POOL_WINDOWS[g]).astype(bf16)

    def gate_chunk(c):
        gbuf[c - n_hist] = _dot(xn_s[...], win_ref[:, _chunk(c)])

    def lru_head(h):
        hs = slice(h * LRU_BLOCK, (h + 1) * LRU_BLOCK)
        ri = _dot(cbf[:, hs], wri_ref[h])
        a, b = _lru_coeffs(ri[:, :LRU_BLOCK] + br_ref[:, hs], ri[:, LRU_BLOCK:] + bi_ref[:, hs],
                           cbuf[:, hs], lam_c[:, hs])
        states, last = _scan_rows(a, b, hcar[0:1, hs])
        ya_s[:, hs] = states.astype(bf16)
        hcar[0:1, hs] = last

    def project_out(c):
        ymid[new, :, _chunk(c)] = x_ref[:, _chunk(c)] + _dot(m_s[...], wo_ref[:, _chunk(c)])

    n_ff = D_FF // NCH
    filler = [functools.partial(ffn_up, c) for c in range(n_ff)]
    filler += [functools.partial(ffn_down, c) for c in range(N_OUT_CH)]
    if final:
        filler.append(ffn_final_norm)
    others = [functools.partial(pool_chunk, g) for g in range(len(POOL_WINDOWS))]
    others += [functools.partial(gate_chunk, GATE_B_CH + c) for c in range(N_OUT_CH)]
    others += [functools.partial(_pool_project, g, d_s, yb_s, pw_ref, ps_ref) for g in range(len(POOL_WINDOWS))]
    others += [lambda c=c: _branch_b(c, gbuf[GATE_B_CH - n_hist + c], yb_s, wb_ref, mb_s) for c in range(N_OUT_CH)]
    others += [functools.partial(gate_chunk, GATE_A_CH + c) for c in range(N_OUT_CH)]

    stages = [functools.partial(conv_chunk, c) for c in range(POOL_CH)]
    stages += [functools.partial(lru_head, h) for h in range(LRU_HEADS)]
    first_head = POOL_CH

    ffn_norm()
    mixer_norm()
    for k, stage in enumerate(stages):
        stage()
        lo, hi = (k * len(filler)) // len(stages), ((k + 1) * len(filler)) // len(stages)
        for fn in filler[lo:hi]:
            fn()
        if k >= first_head:
            h = k - first_head
            lo, hi = (h * len(others)) // LRU_HEADS, ((h + 1) * len(others)) // LRU_HEADS
            for fn in others[lo:hi]:
                fn()
    for c in range(N_OUT_CH):
        _merge(c, gbuf[GATE_A_CH - n_hist + c], ya_s, wa_ref, mb_s, m_s)
    for c in range(N_OUT_CH):
        project_out(c)


def _ffn_kernel(final, tm, x_ref, g2_ref, wgu_ref, wd_ref, fg_ref, y_ref, xn_s, h_s):
    nc = _norm_chunk(tm)
    _norm_to_bf16(x_ref, g2_ref, xn_s, tm)

    for c in range(D_FF // NCH):
        gate = _dot(xn_s[...], wgu_ref[:, c * NCH:(c + 1) * NCH])
        up = _dot(xn_s[...], wgu_ref[:, D_FF + c * NCH:D_FF + (c + 1) * NCH])
        h_s[:, c * NCH:(c + 1) * NCH] = (jax.nn.silu(gate) * up).astype(bf16)

    for c in range(N_OUT_CH):
        y_ref[:, _chunk(c)] = x_ref[:, _chunk(c)] + _dot(h_s[...], wd_ref[:, _chunk(c)])

    if final:
        def final_rows(r0):
            rows = pl.ds(r0, nc)
            y_ref[rows, :] = _rmsnorm(y_ref[rows, :], fg_ref[...])
        _for_rows(tm, nc, final_rows)


def _layer_spec(w, layer):
    nd = w.ndim - 1
    return pl.BlockSpec((None,) + w.shape[1:], lambda *_: (layer,) + (0,) * nd, pipeline_mode=pl.Buffered(1))


def _mixer_weights(w):
    return (w['norm1_g'], w['w_in'], w['conv_w'], w['conv_b'], w['wri'], w['b_r'], w['b_i'],
            w['lam'], w['pool_w'], w['pool_scale'], w['w_br_a'], w['w_br_b'], w['w_out'])


def _state_shapes(n):
    return (jax.ShapeDtypeStruct((n, CONV_W - 1, D_LRU), f32),
            jax.ShapeDtypeStruct((n, 1, D_LRU), f32),
            jax.ShapeDtypeStruct((n, POOL_MAX - 1, D_POOL), f32))


def _state_specs(n, index_map):
    return [pl.BlockSpec((n, CONV_W - 1, D_LRU), index_map), pl.BlockSpec((n, 1, D_LRU), index_map),
            pl.BlockSpec((n, POOL_MAX - 1, D_POOL), index_map)]


def _mixer_step_call(x, state_conv, state_lru, state_pool, w, layer, segs, name):
    tm = x.shape[0]
    ns = state_conv.shape[1]
    max_len = max(seg[1] for seg in segs)
    weights = _mixer_weights(w)
    x_spec = pl.BlockSpec((tm, D_MODEL), lambda i: (0, 0))
    whole = lambda i: (0, 0, 0)
    state_in = [pl.BlockSpec((None,) + s.shape[1:], lambda i: (layer, 0, 0, 0))
                for s in (state_conv, state_lru, state_pool)]
    scratch = [
        pltpu.VMEM((tm, D_MODEL), bf16),
        pltpu.VMEM((len(segs), GATE_A_CH, HDR + max_len, NCH), f32),
        pltpu.VMEM((N_IN_CH - GATE_A_CH, tm, NCH), f32),
        pltpu.VMEM((tm, D_LRU), f32),
        pltpu.VMEM((tm, D_LRU), bf16),
        pltpu.VMEM((tm, D_LRU), f32),
        pltpu.VMEM((tm, D_LRU), f32),
        pltpu.VMEM((tm, D_LRU), bf16),
        pltpu.VMEM((tm, D_POOL), bf16),
        pltpu.VMEM((tm, D_POOL), bf16),
        pltpu.VMEM((tm, D_MODEL), f32),
        pltpu.VMEM((tm, D_MODEL), bf16),
    ]
    return pl.pallas_call(
        functools.partial(_mixer_step_kernel, tuple(segs), tm),
        grid=(1,),
        in_specs=[x_spec] + state_in + [_layer_spec(v, layer) for v in weights],
        out_specs=[x_spec] + _state_specs(ns, whole) + _state_specs(1, whole),
        out_shape=(jax.ShapeDtypeStruct(x.shape, f32),) + _state_shapes(ns) + _state_shapes(1),
        scratch_shapes=scratch,
        compiler_params=pltpu.CompilerParams(dimension_semantics=("arbitrary",), vmem_limit_bytes=VMEM_LIMIT),
        name=name,
    )(x, state_conv, state_lru, state_pool, *weights)


def _mixer_stream_call(x, conv_init, h_init, pool_init, w, layer, tm, name):
    nb, t, _ = x.shape
    nt = t // tm
    assert nt * tm == t
    weights = _mixer_weights(w)
    x_spec = pl.BlockSpec((None, tm, D_MODEL), lambda b, i: (b, i, 0))
    scratch = [
        pltpu.VMEM((tm, D_MODEL), bf16),
        pltpu.VMEM((GATE_A_CH, HDR, NCH), f32),
        pltpu.VMEM((N_IN_CH - GATE_A_CH, tm, NCH), f32),
        pltpu.VMEM((tm, D_LRU), f32),
        pltpu.VMEM((tm, D_LRU), bf16),
        pltpu.VMEM((tm, D_LRU), bf16),
        pltpu.VMEM((tm, D_POOL), bf16),
        pltpu.VMEM((tm, D_POOL), bf16),
        pltpu.VMEM((tm, D_MODEL), f32),
        pltpu.VMEM((tm, D_MODEL), bf16),
        pltpu.VMEM((SUBLANES, D_LRU), f32),
    ]
    return pl.pallas_call(
        functools.partial(_mixer_stream_kernel, tm),
        grid=(nb, nt),
        in_specs=[x_spec] + _state_specs(1, lambda b, i: (0, 0, 0)) + [_layer_spec(v, layer) for v in weights],
        out_specs=[x_spec] + _state_specs(1, lambda b, i: (b, 0, 0)),
        out_shape=(jax.ShapeDtypeStruct(x.shape, f32),) + _state_shapes(nb),
        scratch_shapes=scratch,
        compiler_params=pltpu.CompilerParams(
            dimension_semantics=("arbitrary", "arbitrary"), vmem_limit_bytes=VMEM_LIMIT),
        name=name,
    )(x, conv_init, h_init, pool_init, *weights)


def _layer_stream_call(x, conv_init, h_init, pool_init, w, layer, final, tm, name):
    nb, t, _ = x.shape
    tps = t // tm
    n_tiles = nb * tps
    assert tps * tm == t
    weights = _mixer_weights(w) + (w['norm2_g'], w['w_gu'], w['w_down'])
    x_tiles = x.reshape(n_tiles, tm, D_MODEL)
    x_spec = pl.BlockSpec((None, tm, D_MODEL), lambda j: (jnp.minimum(j, n_tiles - 1), 0, 0))
    y_spec = pl.BlockSpec((None, tm, D_MODEL), lambda j: (jnp.maximum(j - 1, 0), 0, 0))
    fg_spec = pl.BlockSpec(w['final_g'].shape, lambda j: (0, 0))
    scratch = [
        pltpu.VMEM((tm, D_MODEL), bf16),
        pltpu.VMEM((GATE_A_CH, HDR, NCH), f32),
        pltpu.VMEM((N_IN_CH - GATE_A_CH, tm, NCH), f32),
        pltpu.VMEM((tm, D_LRU), f32),
        pltpu.VMEM((tm, D_LRU), bf16),
        pltpu.VMEM((tm, D_LRU), bf16),
        pltpu.VMEM((tm, D_POOL), bf16),
        pltpu.VMEM((tm, D_POOL), bf16),
        pltpu.VMEM((tm, D_MODEL), f32),
        pltpu.VMEM((tm, D_MODEL), bf16),
        pltpu.VMEM((SUBLANES, D_LRU), f32),
        pltpu.VMEM((2, tm, D_MODEL), f32),
        pltpu.VMEM((tm, D_MODEL), bf16),
        pltpu.VMEM((tm, D_FF), bf16),
    ]
    y, c_n, h_n, p_n = pl.pallas_call(
        functools.partial(_layer_stream_kernel, tm, tps, n_tiles, final),
        grid=(n_tiles + 1,),
        in_specs=([x_spec] + _state_specs(1, lambda j: (0, 0, 0)) + [_layer_spec(v, layer) for v in weights]
                  + [fg_spec]),
        out_specs=[y_spec] + _state_specs(1, lambda j: (jnp.maximum(j - 1, 0) // tps, 0, 0)),
        out_shape=(jax.ShapeDtypeStruct(x_tiles.shape, f32),) + _state_shapes(nb),
        scratch_shapes=scratch,
        compiler_params=pltpu.CompilerParams(dimension_semantics=("arbitrary",), vmem_limit_bytes=VMEM_LIMIT),
        name=name,
    )(x_tiles, conv_init, h_init, pool_init, *weights, w['final_g'])
    return y.reshape(x.shape), c_n, h_n, p_n


def _ffn_call(x, w, layer, final, tm, name):
    nb, t, _ = x.shape
    nt = t // tm
    assert nt * tm == t
    x_spec = pl.BlockSpec((None, tm, D_MODEL), lambda b, i: (b, i, 0))
    weights = (w['norm2_g'], w['w_gu'], w['w_down'])
    fg_spec = pl.BlockSpec(w['final_g'].shape, lambda b, i: (0, 0))
    return pl.pallas_call(
        functools.partial(_ffn_kernel, final, tm),
        grid=(nb, nt),
        in_specs=[x_spec] + [_layer_spec(v, layer) for v in weights] + [fg_spec],
        out_specs=x_spec,
        out_shape=jax.ShapeDtypeStruct(x.shape, f32),
        scratch_shapes=[pltpu.VMEM((tm, D_MODEL), bf16), pltpu.VMEM((tm, D_FF), bf16)],
        compiler_params=pltpu.CompilerParams(
            dimension_semantics=("arbitrary", "arbitrary"), vmem_limit_bytes=VMEM_LIMIT),
        name=name,
    )(x, *weights, w['final_g'])


def kernel(x_prompt, x_sample, state_conv, state_lru, state_pool, meta_tokens, norm1_g, w_in, conv_w, conv_b,
           lam, w_r, b_r, w_i, b_i, pool_w, pool_scale, w_br_a, w_br_b, w_out, norm2_g, w_gu, w_down, final_g):
    depth = w_in.shape[0]
    ns, ts, _ = x_sample.shape
    rows = lambda v: v.reshape(depth, 1, -1).astype(f32)
    w = {
        'norm1_g': rows(norm1_g), 'w_in': w_in.astype(bf16), 'conv_w': conv_w, 'conv_b': rows(conv_b),
        'wri': jnp.concatenate([w_r, w_i], axis=-1).astype(bf16),
        'b_r': rows(b_r), 'b_i': rows(b_i), 'lam': rows(lam),
        'pool_w': pool_w.astype(bf16), 'pool_scale': rows(pool_scale),
        'w_br_a': w_br_a.astype(bf16), 'w_br_b': w_br_b.astype(bf16), 'w_out': w_out.astype(bf16),
        'norm2_g': rows(norm2_g), 'w_gu': w_gu.astype(bf16), 'w_down': w_down.astype(bf16),
        'final_g': final_g.reshape(1, -1).astype(f32),
    }

    segs_s = [(s * ts, ts, PAST_LEN, True) for s in range(ns)] + [(ns * ts, N_META, 0, False)]
    tm_s = ns * ts + N_META
    xs = jnp.concatenate([x_sample.reshape(ns * ts, D_MODEL), meta_tokens.astype(f32)], axis=0)
    xp = x_prompt
    state_lru = state_lru[:, :, None, :]

    outs = {k: [] for k in ('cp', 'hp', 'pp', 'cs', 'hs', 'ps')}
    for l in range(depth):
        last = l == depth - 1
        xs, c_s, h_s, p_s, c_m, h_m, p_m = _mixer_step_call(xs, state_conv, state_lru, state_pool, w, l, segs_s,
                                                            f"mixer_step_l{l}")
        xs = _ffn_call(xs[None], w, l, last, tm_s, f"ffn_step_l{l}")[0]
        xp, c_p, h_p, p_p = _layer_stream_call(xp, c_m, h_m, p_m, w, l, last, TM_LAYER, f"layer_prompt_l{l}")

        outs['cp'].append(c_p); outs['hp'].append(h_p[:, 0]); outs['pp'].append(p_p)
        outs['cs'].append(c_s); outs['hs'].append(h_s[:, 0]); outs['ps'].append(p_s)

    y_sample = xs[:ns * ts].reshape(ns, ts, D_MODEL)
    return (xp, y_sample, jnp.stack(outs['cp']), jnp.stack(outs['hp']), jnp.stack(outs['pp']),
            jnp.stack(outs['cs']), jnp.stack(outs['hs']), jnp.stack(outs['ps']))
```

```python
import functools

import jax
import jax.numpy as jnp
from jax import lax
from jax.experimental import pallas as pl
from jax.experimental.pallas import tpu as pltpu

D_MODEL = 1024
D_LRU = 1024
D_POOL = 1024
LRU_HEADS = 8
LRU_BLOCK = D_LRU // LRU_HEADS
LRU_C = 8.0
CONV_W = 4
POOL_WINDOWS = (2, 4, 8, 16)
POOL_GROUP = D_POOL // len(POOL_WINDOWS)
POOL_MAX = 16
D_FF = 2816
N_META = 16
PAST_LEN = 2048
EPS = 1e-6

SUBLANES = 8
BF16_ROWS = 16
HDR = 16
NCH = 256
N_IN_CH = 4 * D_MODEL // NCH
CONV_CH = 0
POOL_CH = D_LRU // NCH
GATE_A_CH = (D_LRU + D_POOL) // NCH
GATE_B_CH = (D_LRU + D_POOL + D_MODEL) // NCH
N_OUT_CH = D_MODEL // NCH
TM_PROMPT = 512
TM_LAYER = 512
ROW_CH = 32
VMEM_LIMIT = 60 * 1024 * 1024

f32 = jnp.float32
bf16 = jnp.bfloat16

assert POOL_GROUP == NCH and HDR >= POOL_MAX - 1 and HDR % BF16_ROWS == 0


def _dot(a, b):
    return jnp.dot(a, b, preferred_element_type=f32)


def _chunk(c):
    return slice(c * NCH, (c + 1) * NCH)


def _rmsnorm(x, g):
    y = x * lax.rsqrt(jnp.mean(x * x, axis=-1, keepdims=True) + EPS)
    return y * g


def _norm_chunk(tm):
    return max(c for c in range(BF16_ROWS, 129, BF16_ROWS) if tm % c == 0)


def _sigmoid(x):
    return 0.5 * jnp.tanh(0.5 * x) + 0.5


def _shift_rows(x, k):
    return pltpu.roll(x, k, axis=0)


def _for_rows(total, chunk, fn):
    n = total // chunk
    assert n * chunk == total
    if n == 1:
        fn(0)
    else:
        def body(k, c):
            fn(pl.multiple_of(k * chunk, chunk))
            return c
        lax.fori_loop(0, n, body, 0)


def _norm_to_bf16(x_ref, g_ref, xn_s, tm):
    nc = _norm_chunk(tm)

    def norm_rows(r0):
        rows = pl.ds(r0, nc)
        xn_s[rows, :] = _rmsnorm(x_ref[rows, :], g_ref[...]).astype(bf16)
    _for_rows(tm, nc, norm_rows)


def _conv_taps(blk, cw_ref, cb_ref, cs):
    tap = lambda k: _shift_rows(blk, CONV_W - 1 - k)[HDR:] if k < CONV_W - 1 else blk[HDR:]
    acc = cb_ref[:, cs] + cw_ref[0:1, cs] * tap(0)
    for k in range(1, CONV_W):
        acc = acc + cw_ref[k:k + 1, cs] * tap(k)
    return acc


def _lam_scaled(lam_ref):
    lam = lam_ref[...]
    log_sig_lam = jnp.minimum(lam, 0.0) - jnp.log1p(jnp.exp(-jnp.abs(lam)))
    return LRU_C * log_sig_lam


def _lru_coeffs(r_pre, i_pre, c, lam_c, start_mask=None):
    r = _sigmoid(r_pre)
    i = _sigmoid(i_pre)
    log_a = r * lam_c
    a = jnp.exp(log_a)
    th = jnp.tanh(log_a)
    mult = jnp.sqrt(-2.0 * th / (1.0 - th))
    if start_mask is not None:
        mult = jnp.where(start_mask, 1.0, mult)
    return a, mult * (i * c)


def _scan_block(a, b, h):
    sub = lax.broadcasted_iota(jnp.int32, a.shape, 0)
    for d in (1, 2, 4):
        keep = sub >= d
        a_prev = pltpu.roll(a, d, axis=0)
        b_prev = pltpu.roll(b, d, axis=0)
        b = jnp.where(keep, a * b_prev, 0.0) + b
        a = jnp.where(keep, a * a_prev, a)
    return a * h + b


def _scan_rows(a, b, h):
    outs = []
    for k in range(a.shape[0] // SUBLANES):
        blk = slice(k * SUBLANES, (k + 1) * SUBLANES)
        outs.append(_scan_block(a[blk], b[blk], h))
        h = outs[-1][SUBLANES - 1:SUBLANES, :]
    return jnp.concatenate(outs, axis=0), h


def _pool_diff(blk, w, cnt=None):
    xb = blk[HDR:]
    tot, span = blk, 1
    while span < w:
        tot = tot + _shift_rows(tot, span)
        span *= 2
    tot = tot[HDR:]
    pooled = tot * (1.0 / w) if cnt is None else tot / cnt
    return pooled - xb


def _pool_project(g, d_s, yb_s, pw_ref, ps_ref):
    gs = _chunk(g)
    yb_s[:, gs] = (_dot(d_s[:, gs], pw_ref[g]) * ps_ref[:, gs]).astype(bf16)


def _branch_b(c, gate_b, yb_s, wb_ref, mb_s):
    mb_s[:, _chunk(c)] = _sigmoid(gate_b) * _dot(yb_s[...], wb_ref[:, _chunk(c)])


def _merge(c, gate_a, ya_s, wa_ref, mb_s, m_s):
    cs = _chunk(c)
    m_s[:, cs] = (_sigmoid(gate_a) * _dot(ya_s[...], wa_ref[:, cs]) + mb_s[:, cs]).astype(bf16)


def _project_out(c, m_s, wo_ref, x_ref, y_ref):
    cs = _chunk(c)
    y_ref[:, cs] = x_ref[:, cs] + _dot(m_s[...], wo_ref[:, cs])


def _mixer_step_kernel(segs, tm,
                       x_ref, convp_ref, hp_ref, poolp_ref, g1_ref, win_ref, cw_ref, cb_ref, wri_ref,
                       br_ref, bi_ref, lam_ref, pw_ref, ps_ref, wa_ref, wb_ref, wo_ref,
                       y_ref, convn_ref, hn_ref, pooln_ref, convm_ref, hm_ref, poolm_ref,
                       xn_s, proj, gbuf, cbuf, cbf, a_s, b_s, ya_s, d_s, yb_s, mb_s, m_s):
    n_hist = GATE_A_CH
    assert sum(1 for seg in segs if not seg[3]) == 1
    for s, (_, _, _, has_state) in enumerate(segs):
        for c in range(n_hist):
            proj[s, c, 0:HDR, :] = jnp.zeros((HDR, NCH), f32)
        if has_state:
            for c in range(POOL_CH):
                proj[s, CONV_CH + c, HDR - (CONV_W - 1):HDR, :] = convp_ref[s, :, _chunk(c)]
                proj[s, POOL_CH + c, HDR - (POOL_MAX - 1):HDR, :] = poolp_ref[s, :, _chunk(c)]

    _norm_to_bf16(x_ref, g1_ref, xn_s, tm)

    for c in range(N_IN_CH):
        p = _dot(xn_s[...], win_ref[:, _chunk(c)])
        if c < n_hist:
            for s, (row0, length, _, _) in enumerate(segs):
                proj[s, c, HDR:HDR + length, :] = p[row0:row0 + length]
        else:
            gbuf[c - n_hist] = p

    for s, (row0, length, _, has_state) in enumerate(segs):
        n = min(ROW_CH, length)

        def conv_rows(t0, s=s, row0=row0, n=n):
            rows = pl.ds(row0 + t0, n)
            for c in range(POOL_CH):
                acc = _conv_taps(proj[s, CONV_CH + c, pl.ds(t0, n + HDR), :], cw_ref, cb_ref, _chunk(c))
                cbuf[rows, _chunk(c)] = acc
                cbf[rows, _chunk(c)] = acc.astype(bf16)
        _for_rows(length, n, conv_rows)
        conv_out, pool_out, k = (convn_ref, pooln_ref, s) if has_state else (convm_ref, poolm_ref, 0)
        for c in range(POOL_CH):
            conv_out[k, :, _chunk(c)] = proj[s, CONV_CH + c, HDR + length - (CONV_W - 1):HDR + length, :]
            pool_out[k, :, _chunk(c)] = proj[s, POOL_CH + c, HDR + length - (POOL_MAX - 1):HDR + length, :]

    for h in range(LRU_HEADS):
        hs = slice(h * LRU_BLOCK, (h + 1) * LRU_BLOCK)
        ri = _dot(cbf[:, hs], wri_ref[h])
        a_s[:, hs] = ri[:, :LRU_BLOCK] + br_ref[:, hs]
        b_s[:, hs] = ri[:, LRU_BLOCK:] + bi_ref[:, hs]
    lam_c = _lam_scaled(lam_ref)

    for s, (row0, length, pos0, has_state) in enumerate(segs):
        n = min(ROW_CH, length)

        def mix_rows(t0, h, s=s, row0=row0, n=n, pos0=pos0):
            rows = pl.ds(row0 + t0, n)
            pos = pos0 + t0 + lax.broadcasted_iota(jnp.int32, (n, 1), 0)
            start = (pos == 0) if pos0 == 0 else None
            a, b = _lru_coeffs(a_s[rows, :], b_s[rows, :], cbuf[rows, :], lam_c, start)
            states, h = _scan_rows(a, b, h)
            ya_s[rows, :] = states.astype(bf16)
            cnt = None if pos0 >= POOL_MAX - 1 else pos + 1
            for g, w in enumerate(POOL_WINDOWS):
                cnt_g = None if cnt is None else jnp.minimum(w, cnt).astype(f32)
                d_s[rows, _chunk(g)] = _pool_diff(proj[s, POOL_CH + g, pl.ds(t0, n + HDR), :], w, cnt_g).astype(bf16)
            return h

        steps = length // n
        h = hp_ref[s] if has_state else jnp.zeros((1, D_LRU), f32)
        if steps == 1:
            h = mix_rows(0, h)
        else:
            h = lax.fori_loop(0, steps, lambda k, h: mix_rows(pl.multiple_of(k * n, n), h), h)
        if has_state:
            hn_ref[s] = h
        else:
            hm_ref[0] = h

    for g in range(len(POOL_WINDOWS)):
        _pool_project(g, d_s, yb_s, pw_ref, ps_ref)
    for c in range(N_OUT_CH):
        _branch_b(c, gbuf[GATE_B_CH - n_hist + c], yb_s, wb_ref, mb_s)
        _merge(c, gbuf[GATE_A_CH - n_hist + c], ya_s, wa_ref, mb_s, m_s)
    for c in range(N_OUT_CH):
        _project_out(c, m_s, wo_ref, x_ref, y_ref)


def _mixer_stream_kernel(tm, x_ref, convp_ref, hp_ref, poolp_ref, g1_ref, win_ref, cw_ref, cb_ref, wri_ref,
                         br_ref, bi_ref, lam_ref, pw_ref, ps_ref, wa_ref, wb_ref, wo_ref,
                         y_ref, convn_ref, hn_ref, pooln_ref,
                         xn_s, hist, gbuf, cbuf, cbf, ya_s, d_s, yb_s, mb_s, m_s, hcar):
    n_hist = GATE_A_CH

    @pl.when(pl.program_id(1) == 0)
    def _():
        for c in range(POOL_CH):
            hist[CONV_CH + c] = jnp.zeros((HDR, NCH), f32)
            hist[POOL_CH + c] = jnp.zeros((HDR, NCH), f32)
            hist[CONV_CH + c, HDR - (CONV_W - 1):HDR, :] = convp_ref[0, :, _chunk(c)]
            hist[POOL_CH + c, HDR - (POOL_MAX - 1):HDR, :] = poolp_ref[0, :, _chunk(c)]
        hcar[0:1, :] = hp_ref[0]

    _norm_to_bf16(x_ref, g1_ref, xn_s, tm)
    lam_c = _lam_scaled(lam_ref)

    def project_with_history(c):
        p = _dot(xn_s[...], win_ref[:, _chunk(c)])
        ext = jnp.concatenate([hist[c], p], axis=0)
        hist[c] = p[tm - HDR:]
        return p, ext

    def conv_chunk(c):
        p, ext = project_with_history(CONV_CH + c)
        acc = _conv_taps(ext, cw_ref, cb_ref, _chunk(c))
        cbuf[:, _chunk(c)] = acc
        cbf[:, _chunk(c)] = acc.astype(bf16)
        convn_ref[0, :, _chunk(c)] = p[tm - (CONV_W - 1):]

    def pool_chunk(g):
        p, ext = project_with_history(POOL_CH + g)
        d_s[:, _chunk(g)] = _pool_diff(ext, POOL_WINDOWS[g]).astype(bf16)
        pooln_ref[0, :, _chunk(g)] = p[tm - (POOL_MAX - 1):]

    def gate_chunk(c):
        gbuf[c - n_hist] = _dot(xn_s[...], win_ref[:, _chunk(c)])

    def lru_head(h):
        hs = slice(h * LRU_BLOCK, (h + 1) * LRU_BLOCK)
        ri = _dot(cbf[:, hs], wri_ref[h])
        a, b = _lru_coeffs(ri[:, :LRU_BLOCK] + br_ref[:, hs], ri[:, LRU_BLOCK:] + bi_ref[:, hs],
                           cbuf[:, hs], lam_c[:, hs])
        states, last = _scan_rows(a, b, hcar[0:1, hs])
        ya_s[:, hs] = states.astype(bf16)
        hcar[0:1, hs] = last
        hn_ref[0, :, hs] = last

    for c in range(POOL_CH):
        conv_chunk(c)
    others = [functools.partial(pool_chunk, g) for g in range(len(POOL_WINDOWS))]
    others += [functools.partial(gate_chunk, GATE_B_CH + c) for c in range(N_OUT_CH)]
    others += [functools.partial(_pool_project, g, d_s, yb_s, pw_ref, ps_ref) for g in range(len(POOL_WINDOWS))]
    others += [lambda c=c: _branch_b(c, gbuf[GATE_B_CH - n_hist + c], yb_s, wb_ref, mb_s) for c in range(N_OUT_CH)]
    others += [functools.partial(gate_chunk, GATE_A_CH + c) for c in range(N_OUT_CH)]
    for h in range(LRU_HEADS):
        lru_head(h)
        lo, hi = (h * len(others)) // LRU_HEADS, ((h + 1) * len(others)) // LRU_HEADS
        for fn in others[lo:hi]:
            fn()

    for c in range(N_OUT_CH):
        _merge(c, gbuf[GATE_A_CH - n_hist + c], ya_s, wa_ref, mb_s, m_s)
    for c in range(N_OUT_CH):
        _project_out(c, m_s, wo_ref, x_ref, y_ref)


def _layer_stream_kernel(tm, tiles_per_stream, n_tiles, final,
                         x_ref, convp_ref, hp_ref, poolp_ref, g1_ref, win_ref, cw_ref, cb_ref, wri_ref,
                         br_ref, bi_ref, lam_ref, pw_ref, ps_ref, wa_ref, wb_ref, wo_ref,
                         g2_ref, wgu_ref, wd_ref, fg_ref,
                         y_ref, convn_ref, hn_ref, pooln_ref,
                         xn_s, hist, gbuf, cbuf, cbf, ya_s, d_s, yb_s, mb_s, m_s, hcar, ymid, xn2_s, h_s):
    j = pl.program_id(0)
    new = j % 2
    cur = 1 - new
    n_hist = GATE_A_CH
    nc = _norm_chunk(tm)

    @pl.when(j == 0)
    def _():
        hist[...] = jnp.zeros(hist.shape, f32)
        hcar[...] = jnp.zeros(hcar.shape, f32)
        ymid[1] = jnp.zeros((tm, D_MODEL), f32)

    for c in range(POOL_CH):
        convn_ref[0, :, _chunk(c)] = hist[CONV_CH + c, HDR - (CONV_W - 1):HDR, :]
        pooln_ref[0, :, _chunk(c)] = hist[POOL_CH + c, HDR - (POOL_MAX - 1):HDR, :]
    hn_ref[0] = hcar[0:1, :]

    @pl.when(jnp.logical_and(j % tiles_per_stream == 0, j < n_tiles))
    def _():
        for c in range(POOL_CH):
            hist[CONV_CH + c] = jnp.zeros((HDR, NCH), f32)
            hist[POOL_CH + c] = jnp.zeros((HDR, NCH), f32)
            hist[CONV_CH + c, HDR - (CONV_W - 1):HDR, :] = convp_ref[0, :, _chunk(c)]
            hist[POOL_CH + c, HDR - (POOL_MAX - 1):HDR, :] = poolp_ref[0, :, _chunk(c)]
        hcar[0:1, :] = hp_ref[0]

    lam_c = _lam_scaled(lam_ref)

    def ffn_norm():
        for r0 in range(0, tm, nc):
            xn2_s[r0:r0 + nc, :] = _rmsnorm(ymid[cur, r0:r0 + nc, :], g2_ref[...]).astype(bf16)

    def ffn_up(c):
        gate = _dot(xn2_s[...], wgu_ref[:, _chunk(c)])
        up = _dot(xn2_s[...], wgu_ref[:, D_FF + c * NCH:D_FF + (c + 1) * NCH])
        h_s[:, _chunk(c)] = (jax.nn.silu(gate) * up).astype(bf16)

    def ffn_down(c):
        y_ref[:, _chunk(c)] = ymid[cur, :, _chunk(c)] + _dot(h_s[...], wd_ref[:, _chunk(c)])

    def ffn_final_norm():
        for r0 in range(0, tm, nc):
            y_ref[r0:r0 + nc, :] = _rmsnorm(y_ref[r0:r0 + nc, :], fg_ref[...])

    def mixer_norm():
        for r0 in range(0, tm, nc):
            xn_s[r0:r0 + nc, :] = _rmsnorm(x_ref[r0:r0 + nc, :], g1_ref[...]).astype(bf16)

    def project_with_history(c):
        p = _dot(xn_s[...], win_ref[:, _chunk(c)])
        ext = jnp.concatenate([hist[c], p], axis=0)
        hist[c] = p[tm - HDR:]
        return ext

    def conv_chunk(c):
        acc = _conv_taps(project_with_history(CONV_CH + c), cw_ref, cb_ref, _chunk(c))
        cbuf[:, _chunk(c)] = acc
        cbf[:, _chunk(c)] = acc.astype(bf16)

    def pool_chunk(g):
        d_s[:, _chunk(g)] = _pool_diff(project_with_history(POOL_CH + g), POOL_WINDOWS[g]).astype(bf16)

    def gate_chunk(c):
        gbuf[c - n_hist] = _dot(xn_s[...], win_ref[:, _chunk(c)])

    def lru_head(h):
        hs = slice(h * LRU_BLOCK, (h + 1) * LRU_BLOCK)
        ri = _dot(cbf[:, hs], wri_ref[h])
        a, b = _lru_coeffs(ri[:, :LRU_BLOCK] + br_ref[:, hs], ri[:, LRU_BLOCK:] + bi_ref[:, hs],
                           cbuf[:, hs], lam_c[:, hs])
        states, last = _scan_rows(a, b, hcar[0:1, hs])
        ya_s[:, hs] = states.astype(bf16)
        hcar[0:1, hs] = last

    def project_out(c):
        ymid[new, :, _chunk(c)] = x_ref[:, _chunk(c)] + _dot(m_s[...], wo_ref[:, _chunk(c)])

    n_ff = D_FF // NCH
    filler = [functools.partial(ffn_up, c) for c in range(n_ff)]
    filler += [functools.partial(ffn_down, c) for c in range(N_OUT_CH)]
    if final:
        filler.append(ffn_final_norm)
    others = [functools.partial(pool_chunk, g) for g in range(len(POOL_WINDOWS))]
    others += [functools.partial(gate_chunk, GATE_B_CH + c) for c in range(N_OUT_CH)]
    others += [functools.partial(_pool_project, g, d_s, yb_s, pw_ref, ps_ref) for g in range(len(POOL_WINDOWS))]
    others += [lambda c=c: _branch_b(c, gbuf[GATE_B_CH - n_hist + c], yb_s, wb_ref, mb_s) for c in range(N_OUT_CH)]
    others += [functools.partial(gate_chunk, GATE_A_CH + c) for c in range(N_OUT_CH)]

    stages = [functools.partial(conv_chunk, c) for c in range(POOL_CH)]
    stages += [functools.partial(lru_head, h) for h in range(LRU_HEADS)]
    first_head = POOL_CH

    ffn_norm()
    mixer_norm()
    for k, stage in enumerate(stages):
        stage()
        lo, hi = (k * len(filler)) // len(stages), ((k + 1) * len(filler)) // len(stages)
        for fn in filler[lo:hi]:
            fn()
        if k >= first_head:
            h = k - first_head
            lo, hi = (h * len(others)) // LRU_HEADS, ((h + 1) * len(others)) // LRU_HEADS
            for fn in others[lo:hi]:
                fn()
    for c in range(N_OUT_CH):
        _merge(c, gbuf[GATE_A_CH - n_hist + c], ya_s, wa_ref, mb_s, m_s)
    for c in range(N_OUT_CH):
        project_out(c)


def _ffn_kernel(final, tm, x_ref, g2_ref, wgu_ref, wd_ref, fg_ref, y_ref, xn_s, h_s):
    nc = _norm_chunk(tm)
    _norm_to_bf16(x_ref, g2_ref, xn_s, tm)

    for c in range(D_FF // NCH):
        gate = _dot(xn_s[...], wgu_ref[:, c * NCH:(c + 1) * NCH])
        up = _dot(xn_s[...], wgu_ref[:, D_FF + c * NCH:D_FF + (c + 1) * NCH])
        h_s[:, c * NCH:(c + 1) * NCH] = (jax.nn.silu(gate) * up).astype(bf16)

    for c in range(N_OUT_CH):
        y_ref[:, _chunk(c)] = x_ref[:, _chunk(c)] + _dot(h_s[...], wd_ref[:, _chunk(c)])

    if final:
        def final_rows(r0):
            rows = pl.ds(r0, nc)
            y_ref[rows, :] = _rmsnorm(y_ref[rows, :], fg_ref[...])
        _for_rows(tm, nc, final_rows)


def _layer_spec(w, layer):
    nd = w.ndim - 1
    return pl.BlockSpec((None,) + w.shape[1:], lambda *_: (layer,) + (0,) * nd, pipeline_mode=pl.Buffered(1))


def _mixer_weights(w):
    return (w['norm1_g'], w['w_in'], w['conv_w'], w['conv_b'], w['wri'], w['b_r'], w['b_i'],
            w['lam'], w['pool_w'], w['pool_scale'], w['w_br_a'], w['w_br_b'], w['w_out'])


def _state_shapes(n):
    return (jax.ShapeDtypeStruct((n, CONV_W - 1, D_LRU), f32),
            jax.ShapeDtypeStruct((n, 1, D_LRU), f32),
            jax.ShapeDtypeStruct((n, POOL_MAX - 1, D_POOL), f32))


def _state_specs(n, index_map):
    return [pl.BlockSpec((n, CONV_W - 1, D_LRU), index_map), pl.BlockSpec((n, 1, D_LRU), index_map),
            pl.BlockSpec((n, POOL_MAX - 1, D_POOL), index_map)]


def _mixer_step_call(x, state_conv, state_lru, state_pool, w, layer, segs, name):
    tm = x.shape[0]
    ns = state_conv.shape[1]
    max_len = max(seg[1] for seg in segs)
    weights = _mixer_weights(w)
    x_spec = pl.BlockSpec((tm, D_MODEL), lambda i: (0, 0))
    whole = lambda i: (0, 0, 0)
    state_in = [pl.BlockSpec((None,) + s.shape[1:], lambda i: (layer, 0, 0, 0))
                for s in (state_conv, state_lru, state_pool)]
    scratch = [
        pltpu.VMEM((tm, D_MODEL), bf16),
        pltpu.VMEM((len(segs), GATE_A_CH, HDR + max_len, NCH), f32),
        pltpu.VMEM((N_IN_CH - GATE_A_CH, tm, NCH), f32),
        pltpu.VMEM((tm, D_LRU), f32),
        pltpu.VMEM((tm, D_LRU), bf16),
        pltpu.VMEM((tm, D_LRU), f32),
        pltpu.VMEM((tm, D_LRU), f32),
        pltpu.VMEM((tm, D_LRU), bf16),
        pltpu.VMEM((tm, D_POOL), bf16),
        pltpu.VMEM((tm, D_POOL), bf16),
        pltpu.VMEM((tm, D_MODEL), f32),
        pltpu.VMEM((tm, D_MODEL), bf16),
    ]
    return pl.pallas_call(
        functools.partial(_mixer_step_kernel, tuple(segs), tm),
        grid=(1,),
        in_specs=[x_spec] + state_in + [_layer_spec(v, layer) for v in weights],
        out_specs=[x_spec] + _state_specs(ns, whole) + _state_specs(1, whole),
        out_shape=(jax.ShapeDtypeStruct(x.shape, f32),) + _state_shapes(ns) + _state_shapes(1),
        scratch_shapes=scratch,
        compiler_params=pltpu.CompilerParams(dimension_semantics=("arbitrary",), vmem_limit_bytes=VMEM_LIMIT),
        name=name,
    )(x, state_conv, state_lru, state_pool, *weights)


def _mixer_stream_call(x, conv_init, h_init, pool_init, w, layer, tm, name):
    nb, t, _ = x.shape
    nt = t // tm
    assert nt * tm == t
    weights = _mixer_weights(w)
    x_spec = pl.BlockSpec((None, tm, D_MODEL), lambda b, i: (b, i, 0))
    scratch = [
        pltpu.VMEM((tm, D_MODEL), bf16),
        pltpu.VMEM((GATE_A_CH, HDR, NCH), f32),
        pltpu.VMEM((N_IN_CH - GATE_A_CH, tm, NCH), f32),
        pltpu.VMEM((tm, D_LRU), f32),
        pltpu.VMEM((tm, D_LRU), bf16),
        pltpu.VMEM((tm, D_LRU), bf16),
        pltpu.VMEM((tm, D_POOL), bf16),
        pltpu.VMEM((tm, D_POOL), bf16),
        pltpu.VMEM((tm, D_MODEL), f32),
        pltpu.VMEM((tm, D_MODEL), bf16),
        pltpu.VMEM((SUBLANES, D_LRU), f32),
    ]
    return pl.pallas_call(
        functools.partial(_mixer_stream_kernel, tm),
        grid=(nb, nt),
        in_specs=[x_spec] + _state_specs(1, lambda b, i: (0, 0, 0)) + [_layer_spec(v, layer) for v in weights],
        out_specs=[x_spec] + _state_specs(1, lambda b, i: (b, 0, 0)),
        out_shape=(jax.ShapeDtypeStruct(x.shape, f32),) + _state_shapes(nb),
        scratch_shapes=scratch,
        compiler_params=pltpu.CompilerParams(
            dimension_semantics=("arbitrary", "arbitrary"), vmem_limit_bytes=VMEM_LIMIT),
        name=name,
    )(x, conv_init, h_init, pool_init, *weights)


def _layer_stream_call(x, conv_init, h_init, pool_init, w, layer, final, tm, name):
    nb, t, _ = x.shape
    tps = t // tm
    n_tiles = nb * tps
    assert tps * tm == t
    weights = _mixer_weights(w) + (w['norm2_g'], w['w_gu'], w['w_down'])
    x_tiles = x.reshape(n_tiles, tm, D_MODEL)
    x_spec = pl.BlockSpec((None, tm, D_MODEL), lambda j: (jnp.minimum(j, n_tiles - 1), 0, 0))
    y_spec = pl.BlockSpec((None, tm, D_MODEL), lambda j: (jnp.maximum(j - 1, 0), 0, 0))
    fg_spec = pl.BlockSpec(w['final_g'].shape, lambda j: (0, 0))
    scratch = [
        pltpu.VMEM((tm, D_MODEL), bf16),
        pltpu.VMEM((GATE_A_CH, HDR, NCH), f32),
        pltpu.VMEM((N_IN_CH - GATE_A_CH, tm, NCH), f32),
        pltpu.VMEM((tm, D_LRU), f32),
        pltpu.VMEM((tm, D_LRU), bf16),
        pltpu.VMEM((tm, D_LRU), bf16),
        pltpu.VMEM((tm, D_POOL), bf16),
        pltpu.VMEM((tm, D_POOL), bf16),
        pltpu.VMEM((tm, D_MODEL), f32),
        pltpu.VMEM((tm, D_MODEL), bf16),
        pltpu.VMEM((SUBLANES, D_LRU), f32),
        pltpu.VMEM((2, tm, D_MODEL), f32),
        pltpu.VMEM((tm, D_MODEL), bf16),
        pltpu.VMEM((tm, D_FF), bf16),
    ]
    y, c_n, h_n, p_n = pl.pallas_call(
        functools.partial(_layer_stream_kernel, tm, tps, n_tiles, final),
        grid=(n_tiles + 1,),
        in_specs=([x_spec] + _state_specs(1, lambda j: (0, 0, 0)) + [_layer_spec(v, layer) for v in weights]
                  + [fg_spec]),
        out_specs=[y_spec] + _state_specs(1, lambda j: (jnp.maximum(j - 1, 0) // tps, 0, 0)),
        out_shape=(jax.ShapeDtypeStruct(x_tiles.shape, f32),) + _state_shapes(nb),
        scratch_shapes=scratch,
        compiler_params=pltpu.CompilerParams(dimension_semantics=("arbitrary",), vmem_limit_bytes=VMEM_LIMIT),
        name=name,
    )(x_tiles, conv_init, h_init, pool_init, *weights, w['final_g'])
    return y.reshape(x.shape), c_n, h_n, p_n


def _ffn_call(x, w, layer, final, tm, name):
    nb, t, _ = x.shape
    nt = t // tm
    assert nt * tm == t
    x_spec = pl.BlockSpec((None, tm, D_MODEL), lambda b, i: (b, i, 0))
    weights = (w['norm2_g'], w['w_gu'], w['w_down'])
    fg_spec = pl.BlockSpec(w['final_g'].shape, lambda b, i: (0, 0))
    return pl.pallas_call(
        functools.partial(_ffn_kernel, final, tm),
        grid=(nb, nt),
        in_specs=[x_spec] + [_layer_spec(v, layer) for v in weights] + [fg_spec],
        out_specs=x_spec,
        out_shape=jax.ShapeDtypeStruct(x.shape, f32),
        scratch_shapes=[pltpu.VMEM((tm, D_MODEL), bf16), pltpu.VMEM((tm, D_FF), bf16)],
        compiler_params=pltpu.CompilerParams(
            dimension_semantics=("arbitrary", "arbitrary"), vmem_limit_bytes=VMEM_LIMIT),
        name=name,
    )(x, *weights, w['final_g'])


def kernel(x_prompt, x_sample, state_conv, state_lru, state_pool, meta_tokens, norm1_g, w_in, conv_w, conv_b,
           lam, w_r, b_r, w_i, b_i, pool_w, pool_scale, w_br_a, w_br_b, w_out, norm2_g, w_gu, w_down, final_g):
    depth = w_in.shape[0]
    ns, ts, _ = x_sample.shape
    rows = lambda v: v.reshape(depth, 1, -1).astype(f32)
    w = {
        'norm1_g': rows(norm1_g), 'w_in': w_in.astype(bf16), 'conv_w': conv_w, 'conv_b': rows(conv_b),
        'wri': jnp.concatenate([w_r, w_i], axis=-1).astype(bf16),
        'b_r': rows(b_r), 'b_i': rows(b_i), 'lam': rows(lam),
        'pool_w': pool_w.astype(bf16), 'pool_scale': rows(pool_scale),
        'w_br_a': w_br_a.astype(bf16), 'w_br_b': w_br_b.astype(bf16), 'w_out': w_out.astype(bf16),
        'norm2_g': rows(norm2_g), 'w_gu': w_gu.astype(bf16), 'w_down': w_down.astype(bf16),
        'final_g': final_g.reshape(1, -1).astype(f32),
    }

    segs_s = [(s * ts, ts, PAST_LEN, True) for s in range(ns)] + [(ns * ts, N_META, 0, False)]
    tm_s = ns * ts + N_META
    xs = jnp.concatenate([x_sample.reshape(ns * ts, D_MODEL), meta_tokens.astype(f32)], axis=0)
    xp = x_prompt
    state_lru = state_lru[:, :, None, :]

    outs = {k: [] for k in ('cp', 'hp', 'pp', 'cs', 'hs', 'ps')}
    for l in range(depth):
        last = l == depth - 1
        xs, c_s, h_s, p_s, c_m, h_m, p_m = _mixer_step_call(xs, state_conv, state_lru, state_pool, w, l, segs_s,
                                                            f"mixer_step_l{l}")
        xs = _ffn_call(xs[None], w, l, last, tm_s, f"ffn_step_l{l}")[0]
        xp, c_p, h_p, p_p = _layer_stream_call(xp, c_m, h_m, p_m, w, l, last, TM_LAYER, f"layer_prompt_l{l}")

        outs['cp'].append(c_p); outs['hp'].append(h_p[:, 0]); outs['pp'].append(p_p)
        outs['cs'].append(c_s); outs['hs'].append(h_s[:, 0]); outs['ps'].append(p_s)

    y_sample = xs[:ns * ts].reshape(ns, ts, D_MODEL)
    return (xp, y_sample, jnp.stack(outs['cp']), jnp.stack(outs['hp']), jnp.stack(outs['pp']),
            jnp.stack(outs['cs']), jnp.stack(outs['hs']), jnp.stack(outs['ps']))
```

```python
import functools

import jax
import jax.numpy as jnp
from jax import lax
from jax.experimental import pallas as pl
from jax.experimental.pallas import tpu as pltpu

D_MODEL = 1024
D_LRU = 1024
D_POOL = 1024
LRU_HEADS = 8
LRU_BLOCK = D_LRU // LRU_HEADS
LRU_C = 8.0
CONV_W = 4
POOL_WINDOWS = (2, 4, 8, 16)
POOL_GROUP = D_POOL // len(POOL_WINDOWS)
POOL_MAX = 16
D_FF = 2816
N_META = 16
PAST_LEN = 2048
EPS = 1e-6

SUBLANES = 8
BF16_ROWS = 16
HDR = 16
NCH = 256
N_IN_CH = 4 * D_MODEL // NCH
CONV_CH = 0
POOL_CH = D_LRU // NCH
GATE_A_CH = (D_LRU + D_POOL) // NCH
GATE_B_CH = (D_LRU + D_POOL + D_MODEL) // NCH
N_OUT_CH = D_MODEL // NCH
TM_PROMPT = 512
TM_LAYER = 256
ROW_CH = 32
VMEM_LIMIT = 60 * 1024 * 1024

f32 = jnp.float32
bf16 = jnp.bfloat16

assert POOL_GROUP == NCH and HDR >= POOL_MAX - 1 and HDR % BF16_ROWS == 0


def _dot(a, b):
    return jnp.dot(a, b, preferred_element_type=f32)


def _chunk(c):
    return slice(c * NCH, (c + 1) * NCH)


def _rmsnorm(x, g):
    y = x * lax.rsqrt(jnp.mean(x * x, axis=-1, keepdims=True) + EPS)
    return y * g


def _norm_chunk(tm):
    return max(c for c in range(BF16_ROWS, 129, BF16_ROWS) if tm % c == 0)


GATE_PRESCALE = 0.5


def _sigmoid_of_half(x_half):
    return 0.5 * jnp.tanh(x_half) + 0.5


def _shift_rows(x, k):
    return pltpu.roll(x, k, axis=0)


def _for_rows(total, chunk, fn):
    n = total // chunk
    assert n * chunk == total
    if n == 1:
        fn(0)
    else:
        def body(k, c):
            fn(pl.multiple_of(k * chunk, chunk))
            return c
        lax.fori_loop(0, n, body, 0)


def _norm_to_bf16(x_ref, g_ref, xn_s, tm):
    nc = _norm_chunk(tm)

    def norm_rows(r0):
        rows = pl.ds(r0, nc)
        xn_s[rows, :] = _rmsnorm(x_ref[rows, :], g_ref[...]).astype(bf16)
    _for_rows(tm, nc, norm_rows)


def _conv_taps(blk, cw_ref, cb_ref, cs):
    tap = lambda k: _shift_rows(blk, CONV_W - 1 - k)[HDR:] if k < CONV_W - 1 else blk[HDR:]
    acc = cb_ref[:, cs] + cw_ref[0:1, cs] * tap(0)
    for k in range(1, CONV_W):
        acc = acc + cw_ref[k:k + 1, cs] * tap(k)
    return acc


def _lam_half(lam_ref):
    lam = lam_ref[...]
    log_sig_lam = jnp.minimum(lam, 0.0) - jnp.log1p(jnp.exp(-jnp.abs(lam)))
    return (0.5 * LRU_C) * log_sig_lam


def _lru_coeffs(r_half, i_half, c, lam_half, start_mask=None):
    i = _sigmoid_of_half(i_half)
    log_a = jnp.tanh(r_half) * lam_half + lam_half
    a = jnp.exp(log_a)
    th = jnp.tanh(log_a)
    mult = jnp.sqrt(-2.0 * th / (1.0 - th))
    if start_mask is not None:
        mult = jnp.where(start_mask, 1.0, mult)
    return a, mult * (i * c)


def _scan_block(a, b, h):
    sub = lax.broadcasted_iota(jnp.int32, a.shape, 0)
    for d in (1, 2, 4):
        keep = sub >= d
        a_prev = pltpu.roll(a, d, axis=0)
        b_prev = pltpu.roll(b, d, axis=0)
        b = jnp.where(keep, a * b_prev, 0.0) + b
        a = jnp.where(keep, a * a_prev, a)
    return a * h + b


def _scan_rows(a, b, h):
    outs = []
    for k in range(a.shape[0] // SUBLANES):
        blk = slice(k * SUBLANES, (k + 1) * SUBLANES)
        outs.append(_scan_block(a[blk], b[blk], h))
        h = outs[-1][SUBLANES - 1:SUBLANES, :]
    return jnp.concatenate(outs, axis=0), h


def _pool_diff(blk, w, cnt=None):
    xb = blk[HDR:]
    tot, span = blk, 1
    while span < w:
        tot = tot + _shift_rows(tot, span)
        span *= 2
    tot = tot[HDR:]
    pooled = tot * (1.0 / w) if cnt is None else tot / cnt
    return pooled - xb


def _pool_project(g, d_s, yb_s, pw_ref, ps_ref):
    gs = _chunk(g)
    yb_s[:, gs] = (_dot(d_s[:, gs], pw_ref[g]) * ps_ref[:, gs]).astype(bf16)


def _branch_b(c, gate_b_half, yb_s, wb_half_ref, mb_s):
    mb_s[:, _chunk(c)] = (jnp.tanh(gate_b_half) + 1.0) * _dot(yb_s[...], wb_half_ref[:, _chunk(c)])


def _merge(c, gate_a_half, ya_s, wa_half_ref, mb_s, m_s):
    cs = _chunk(c)
    m_s[:, cs] = ((jnp.tanh(gate_a_half) + 1.0) * _dot(ya_s[...], wa_half_ref[:, cs]) + mb_s[:, cs]).astype(bf16)


def _project_out(c, m_s, wo_ref, x_ref, y_ref):
    cs = _chunk(c)
    y_ref[:, cs] = x_ref[:, cs] + _dot(m_s[...], wo_ref[:, cs])


def _mixer_step_kernel(segs, tm,
                       x_ref, convp_ref, hp_ref, poolp_ref, g1_ref, win_ref, cw_ref, cb_ref, wri_ref,
                       br_ref, bi_ref, lam_ref, pw_ref, ps_ref, wa_ref, wb_ref, wo_ref,
                       y_ref, convn_ref, hn_ref, pooln_ref, convm_ref, hm_ref, poolm_ref,
                       xn_s, proj, gbuf, cbuf, cbf, a_s, b_s, ya_s, d_s, yb_s, mb_s, m_s):
    n_hist = GATE_A_CH
    assert sum(1 for seg in segs if not seg[3]) == 1
    for s, (_, _, _, has_state) in enumerate(segs):
        for c in range(n_hist):
            proj[s, c, 0:HDR, :] = jnp.zeros((HDR, NCH), f32)
        if has_state:
            for c in range(POOL_CH):
                proj[s, CONV_CH + c, HDR - (CONV_W - 1):HDR, :] = convp_ref[s, :, _chunk(c)]
                proj[s, POOL_CH + c, HDR - (POOL_MAX - 1):HDR, :] = poolp_ref[s, :, _chunk(c)]

    _norm_to_bf16(x_ref, g1_ref, xn_s, tm)

    for c in range(N_IN_CH):
        p = _dot(xn_s[...], win_ref[:, _chunk(c)])
        if c < n_hist:
            for s, (row0, length, _, _) in enumerate(segs):
                proj[s, c, HDR:HDR + length, :] = p[row0:row0 + length]
        else:
            gbuf[c - n_hist] = p

    for s, (row0, length, _, has_state) in enumerate(segs):
        n = min(ROW_CH, length)

        def conv_rows(t0, s=s, row0=row0, n=n):
            rows = pl.ds(row0 + t0, n)
            for c in range(POOL_CH):
                acc = _conv_taps(proj[s, CONV_CH + c, pl.ds(t0, n + HDR), :], cw_ref, cb_ref, _chunk(c))
                cbuf[rows, _chunk(c)] = acc
                cbf[rows, _chunk(c)] = acc.astype(bf16)
        _for_rows(length, n, conv_rows)
        conv_out, pool_out, k = (convn_ref, pooln_ref, s) if has_state else (convm_ref, poolm_ref, 0)
        for c in range(POOL_CH):
            conv_out[k, :, _chunk(c)] = proj[s, CONV_CH + c, HDR + length - (CONV_W - 1):HDR + length, :]
            pool_out[k, :, _chunk(c)] = proj[s, POOL_CH + c, HDR + length - (POOL_MAX - 1):HDR + length, :]

    for h in range(LRU_HEADS):
        hs = slice(h * LRU_BLOCK, (h + 1) * LRU_BLOCK)
        ri = _dot(cbf[:, hs], wri_ref[h])
        a_s[:, hs] = ri[:, :LRU_BLOCK] + br_ref[:, hs]
        b_s[:, hs] = ri[:, LRU_BLOCK:] + bi_ref[:, hs]
    lam_half = _lam_half(lam_ref)

    for s, (row0, length, pos0, has_state) in enumerate(segs):
        n = min(ROW_CH, length)

        def mix_rows(t0, h, s=s, row0=row0, n=n, pos0=pos0):
            rows = pl.ds(row0 + t0, n)
            pos = pos0 + t0 + lax.broadcasted_iota(jnp.int32, (n, 1), 0)
            start = (pos == 0) if pos0 == 0 else None
            a, b = _lru_coeffs(a_s[rows, :], b_s[rows, :], cbuf[rows, :], lam_half, start)
            states, h = _scan_rows(a, b, h)
            ya_s[rows, :] = states.astype(bf16)
            cnt = None if pos0 >= POOL_MAX - 1 else pos + 1
            for g, w in enumerate(POOL_WINDOWS):
                cnt_g = None if cnt is None else jnp.minimum(w, cnt).astype(f32)
                d_s[rows, _chunk(g)] = _pool_diff(proj[s, POOL_CH + g, pl.ds(t0, n + HDR), :], w, cnt_g).astype(bf16)
            return h

        steps = length // n
        h = hp_ref[s] if has_state else jnp.zeros((1, D_LRU), f32)
        if steps == 1:
            h = mix_rows(0, h)
        else:
            h = lax.fori_loop(0, steps, lambda k, h: mix_rows(pl.multiple_of(k * n, n), h), h)
        if has_state:
            hn_ref[s] = h
        else:
            hm_ref[0] = h

    for g in range(len(POOL_WINDOWS)):
        _pool_project(g, d_s, yb_s, pw_ref, ps_ref)
    for c in range(N_OUT_CH):
        _branch_b(c, gbuf[GATE_B_CH - n_hist + c], yb_s, wb_ref, mb_s)
        _merge(c, gbuf[GATE_A_CH - n_hist + c], ya_s, wa_ref, mb_s, m_s)
    for c in range(N_OUT_CH):
        _project_out(c, m_s, wo_ref, x_ref, y_ref)


def _mixer_stream_kernel(tm, x_ref, convp_ref, hp_ref, poolp_ref, g1_ref, win_ref, cw_ref, cb_ref, wri_ref,
                         br_ref, bi_ref, lam_ref, pw_ref, ps_ref, wa_ref, wb_ref, wo_ref,
                         y_ref, convn_ref, hn_ref, pooln_ref,
                         xn_s, hist, gbuf, cbuf, cbf, ya_s, d_s, yb_s, mb_s, m_s, hcar):
    n_hist = GATE_A_CH

    @pl.when(pl.program_id(1) == 0)
    def _():
        for c in range(POOL_CH):
            hist[CONV_CH + c] = jnp.zeros((HDR, NCH), f32)
            hist[POOL_CH + c] = jnp.zeros((HDR, NCH), f32)
            hist[CONV_CH + c, HDR - (CONV_W - 1):HDR, :] = convp_ref[0, :, _chunk(c)]
            hist[POOL_CH + c, HDR - (POOL_MAX - 1):HDR, :] = poolp_ref[0, :, _chunk(c)]
        hcar[0:1, :] = hp_ref[0]

    _norm_to_bf16(x_ref, g1_ref, xn_s, tm)
    lam_half = _lam_half(lam_ref)

    def project_with_history(c):
        p = _dot(xn_s[...], win_ref[:, _chunk(c)])
        ext = jnp.concatenate([hist[c], p], axis=0)
        hist[c] = p[tm - HDR:]
        return p, ext

    def conv_chunk(c):
        p, ext = project_with_history(CONV_CH + c)
        acc = _conv_taps(ext, cw_ref, cb_ref, _chunk(c))
        cbuf[:, _chunk(c)] = acc
        cbf[:, _chunk(c)] = acc.astype(bf16)
        convn_ref[0, :, _chunk(c)] = p[tm - (CONV_W - 1):]

    def pool_chunk(g):
        p, ext = project_with_history(POOL_CH + g)
        d_s[:, _chunk(g)] = _pool_diff(ext, POOL_WINDOWS[g]).astype(bf16)
        pooln_ref[0, :, _chunk(g)] = p[tm - (POOL_MAX - 1):]

    def gate_chunk(c):
        gbuf[c - n_hist] = _dot(xn_s[...], win_ref[:, _chunk(c)])

    def lru_head(h):
        hs = slice(h * LRU_BLOCK, (h + 1) * LRU_BLOCK)
        ri = _dot(cbf[:, hs], wri_ref[h])
        a, b = _lru_coeffs(ri[:, :LRU_BLOCK] + br_ref[:, hs], ri[:, LRU_BLOCK:] + bi_ref[:, hs],
                           cbuf[:, hs], lam_half[:, hs])
        states, last = _scan_rows(a, b, hcar[0:1, hs])
        ya_s[:, hs] = states.astype(bf16)
        hcar[0:1, hs] = last
        hn_ref[0, :, hs] = last

    for c in range(POOL_CH):
        conv_chunk(c)
    others = [functools.partial(pool_chunk, g) for g in range(len(POOL_WINDOWS))]
    others += [functools.partial(gate_chunk, GATE_B_CH + c) for c in range(N_OUT_CH)]
    others += [functools.partial(_pool_project, g, d_s, yb_s, pw_ref, ps_ref) for g in range(len(POOL_WINDOWS))]
    others += [lambda c=c: _branch_b(c, gbuf[GATE_B_CH - n_hist + c], yb_s, wb_ref, mb_s) for c in range(N_OUT_CH)]
    others += [functools.partial(gate_chunk, GATE_A_CH + c) for c in range(N_OUT_CH)]
    for h in range(LRU_HEADS):
        lru_head(h)
        lo, hi = (h * len(others)) // LRU_HEADS, ((h + 1) * len(others)) // LRU_HEADS
        for fn in others[lo:hi]:
            fn()

    for c in range(N_OUT_CH):
        _merge(c, gbuf[GATE_A_CH - n_hist + c], ya_s, wa_ref, mb_s, m_s)
    for c in range(N_OUT_CH):
        _project_out(c, m_s, wo_ref, x_ref, y_ref)


def _layer_stream_kernel(tm, tiles_per_stream, n_tiles, final,
                         x_ref, convp_ref, hp_ref, poolp_ref, g1_ref, win_ref, cw_ref, cb_ref, wri_ref,
                         br_ref, bi_ref, lam_ref, pw_ref, ps_ref, wa_ref, wb_ref, wo_ref,
                         g2_ref, wgu_ref, wd_ref, fg_ref,
                         y_ref, convn_ref, hn_ref, pooln_ref,
                         xn_s, hist, gbuf, cbuf, cbf, ya_s, d_s, yb_s, mb_s, m_s, hcar, ymid, xn2_s, h_s):
    j = pl.program_id(0)
    new = j % 2
    cur = 1 - new
    n_hist = GATE_A_CH
    nc = _norm_chunk(tm)

    @pl.when(j == 0)
    def _():
        hist[...] = jnp.zeros(hist.shape, f32)
        hcar[...] = jnp.zeros(hcar.shape, f32)
        ymid[1] = jnp.zeros((tm, D_MODEL), f32)

    for c in range(POOL_CH):
        convn_ref[0, :, _chunk(c)] = hist[CONV_CH + c, HDR - (CONV_W - 1):HDR, :]
        pooln_ref[0, :, _chunk(c)] = hist[POOL_CH + c, HDR - (POOL_MAX - 1):HDR, :]
    hn_ref[0] = hcar[0:1, :]

    @pl.when(jnp.logical_and(j % tiles_per_stream == 0, j < n_tiles))
    def _():
        for c in range(POOL_CH):
            hist[CONV_CH + c] = jnp.zeros((HDR, NCH), f32)
            hist[POOL_CH + c] = jnp.zeros((HDR, NCH), f32)
            hist[CONV_CH + c, HDR - (CONV_W - 1):HDR, :] = convp_ref[0, :, _chunk(c)]
            hist[POOL_CH + c, HDR - (POOL_MAX - 1):HDR, :] = poolp_ref[0, :, _chunk(c)]
        hcar[0:1, :] = hp_ref[0]

    lam_half = _lam_half(lam_ref)

    def ffn_norm():
        for r0 in range(0, tm, nc):
            xn2_s[r0:r0 + nc, :] = _rmsnorm(ymid[cur, r0:r0 + nc, :], g2_ref[...]).astype(bf16)

    def ffn_up(c):
        gate = _dot(xn2_s[...], wgu_ref[:, _chunk(c)])
        up = _dot(xn2_s[...], wgu_ref[:, D_FF + c * NCH:D_FF + (c + 1) * NCH])
        h_s[:, _chunk(c)] = (jax.nn.silu(gate) * up).astype(bf16)

    def ffn_down(c):
        y_ref[:, _chunk(c)] = ymid[cur, :, _chunk(c)] + _dot(h_s[...], wd_ref[:, _chunk(c)])

    def ffn_final_norm():
        for r0 in range(0, tm, nc):
            y_ref[r0:r0 + nc, :] = _rmsnorm(y_ref[r0:r0 + nc, :], fg_ref[...])

    def mixer_norm():
        for r0 in range(0, tm, nc):
            xn_s[r0:r0 + nc, :] = _rmsnorm(x_ref[r0:r0 + nc, :], g1_ref[...]).astype(bf16)

    def project_with_history(c):
        p = _dot(xn_s[...], win_ref[:, _chunk(c)])
        ext = jnp.concatenate([hist[c], p], axis=0)
        hist[c] = p[tm - HDR:]
        return ext

    def conv_chunk(c):
        acc = _conv_taps(project_with_history(CONV_CH + c), cw_ref, cb_ref, _chunk(c))
        cbuf[:, _chunk(c)] = acc
        cbf[:, _chunk(c)] = acc.astype(bf16)

    def pool_chunk(g):
        d_s[:, _chunk(g)] = _pool_diff(project_with_history(POOL_CH + g), POOL_WINDOWS[g]).astype(bf16)

    def gate_chunk(c):
        gbuf[c - n_hist] = _dot(xn_s[...], win_ref[:, _chunk(c)])

    def lru_head(h):
        hs = slice(h * LRU_BLOCK, (h + 1) * LRU_BLOCK)
        ri = _dot(cbf[:, hs], wri_ref[h])
        a, b = _lru_coeffs(ri[:, :LRU_BLOCK] + br_ref[:, hs], ri[:, LRU_BLOCK:] + bi_ref[:, hs],
                           cbuf[:, hs], lam_half[:, hs])
        states, last = _scan_rows(a, b, hcar[0:1, hs])
        ya_s[:, hs] = states.astype(bf16)
        hcar[0:1, hs] = last

    def project_out(c):
        ymid[new, :, _chunk(c)] = x_ref[:, _chunk(c)] + _dot(m_s[...], wo_ref[:, _chunk(c)])

    n_ff = D_FF // NCH
    filler = [functools.partial(ffn_up, c) for c in range(n_ff)]
    filler += [functools.partial(ffn_down, c) for c in range(N_OUT_CH)]
    if final:
        filler.append(ffn_final_norm)
    others = [functools.partial(pool_chunk, g) for g in range(len(POOL_WINDOWS))]
    others += [functools.partial(gate_chunk, GATE_B_CH + c) for c in range(N_OUT_CH)]
    others += [functools.partial(_pool_project, g, d_s, yb_s, pw_ref, ps_ref) for g in range(len(POOL_WINDOWS))]
    others += [lambda c=c: _branch_b(c, gbuf[GATE_B_CH - n_hist + c], yb_s, wb_ref, mb_s) for c in range(N_OUT_CH)]
    others += [functools.partial(gate_chunk, GATE_A_CH + c) for c in range(N_OUT_CH)]

    stages = [functools.partial(conv_chunk, c) for c in range(POOL_CH)]
    stages += [functools.partial(lru_head, h) for h in range(LRU_HEADS)]
    first_head = POOL_CH

    ffn_norm()
    mixer_norm()
    for k, stage in enumerate(stages):
        stage()
        lo, hi = (k * len(filler)) // len(stages), ((k + 1) * len(filler)) // len(stages)
        for fn in filler[lo:hi]:
            fn()
        if k >= first_head:
            h = k - first_head
            lo, hi = (h * len(others)) // LRU_HEADS, ((h + 1) * len(others)) // LRU_HEADS
            for fn in others[lo:hi]:
                fn()
    for c in range(N_OUT_CH):
        _merge(c, gbuf[GATE_A_CH - n_hist + c], ya_s, wa_ref, mb_s, m_s)
    for c in range(N_OUT_CH):
        project_out(c)


def _ffn_kernel(final, tm, x_ref, g2_ref, wgu_ref, wd_ref, fg_ref, y_ref, xn_s, h_s):
    nc = _norm_chunk(tm)
    _norm_to_bf16(x_ref, g2_ref, xn_s, tm)

    for c in range(D_FF // NCH):
        gate = _dot(xn_s[...], wgu_ref[:, c * NCH:(c + 1) * NCH])
        up = _dot(xn_s[...], wgu_ref[:, D_FF + c * NCH:D_FF + (c + 1) * NCH])
        h_s[:, c * NCH:(c + 1) * NCH] = (jax.nn.silu(gate) * up).astype(bf16)

    for c in range(N_OUT_CH):
        y_ref[:, _chunk(c)] = x_ref[:, _chunk(c)] + _dot(h_s[...], wd_ref[:, _chunk(c)])

    if final:
        def final_rows(r0):
            rows = pl.ds(r0, nc)
            y_ref[rows, :] = _rmsnorm(y_ref[rows, :], fg_ref[...])
        _for_rows(tm, nc, final_rows)


def _layer_spec(w, layer):
    nd = w.ndim - 1
    return pl.BlockSpec((None,) + w.shape[1:], lambda *_: (layer,) + (0,) * nd, pipeline_mode=pl.Buffered(1))


def _mixer_weights(w):
    return (w['norm1_g'], w['w_in'], w['conv_w'], w['conv_b'], w['wri'], w['b_r'], w['b_i'],
            w['lam'], w['pool_w'], w['pool_scale'], w['w_br_a'], w['w_br_b'], w['w_out'])


def _state_shapes(n):
    return (jax.ShapeDtypeStruct((n, CONV_W - 1, D_LRU), f32),
            jax.ShapeDtypeStruct((n, 1, D_LRU), f32),
            jax.ShapeDtypeStruct((n, POOL_MAX - 1, D_POOL), f32))


def _state_specs(n, index_map):
    return [pl.BlockSpec((n, CONV_W - 1, D_LRU), index_map), pl.BlockSpec((n, 1, D_LRU), index_map),
            pl.BlockSpec((n, POOL_MAX - 1, D_POOL), index_map)]


def _mixer_step_call(x, state_conv, state_lru, state_pool, w, layer, segs, name):
    tm = x.shape[0]
    ns = state_conv.shape[1]
    max_len = max(seg[1] for seg in segs)
    weights = _mixer_weights(w)
    x_spec = pl.BlockSpec((tm, D_MODEL), lambda i: (0, 0))
    whole = lambda i: (0, 0, 0)
    state_in = [pl.BlockSpec((None,) + s.shape[1:], lambda i: (layer, 0, 0, 0))
                for s in (state_conv, state_lru, state_pool)]
    scratch = [
        pltpu.VMEM((tm, D_MODEL), bf16),
        pltpu.VMEM((len(segs), GATE_A_CH, HDR + max_len, NCH), f32),
        pltpu.VMEM((N_IN_CH - GATE_A_CH, tm, NCH), f32),
        pltpu.VMEM((tm, D_LRU), f32),
        pltpu.VMEM((tm, D_LRU), bf16),
        pltpu.VMEM((tm, D_LRU), f32),
        pltpu.VMEM((tm, D_LRU), f32),
        pltpu.VMEM((tm, D_LRU), bf16),
        pltpu.VMEM((tm, D_POOL), bf16),
        pltpu.VMEM((tm, D_POOL), bf16),
        pltpu.VMEM((tm, D_MODEL), f32),
        pltpu.VMEM((tm, D_MODEL), bf16),
    ]
    return pl.pallas_call(
        functools.partial(_mixer_step_kernel, tuple(segs), tm),
        grid=(1,),
        in_specs=[x_spec] + state_in + [_layer_spec(v, layer) for v in weights],
        out_specs=[x_spec] + _state_specs(ns, whole) + _state_specs(1, whole),
        out_shape=(jax.ShapeDtypeStruct(x.shape, f32),) + _state_shapes(ns) + _state_shapes(1),
        scratch_shapes=scratch,
        compiler_params=pltpu.CompilerParams(dimension_semantics=("arbitrary",), vmem_limit_bytes=VMEM_LIMIT),
        name=name,
    )(x, state_conv, state_lru, state_pool, *weights)


def _mixer_stream_call(x, conv_init, h_init, pool_init, w, layer, tm, name):
    nb, t, _ = x.shape
    nt = t // tm
    assert nt * tm == t
    weights = _mixer_weights(w)
    x_spec = pl.BlockSpec((None, tm, D_MODEL), lambda b, i: (b, i, 0))
    scratch = [
        pltpu.VMEM((tm, D_MODEL), bf16),
        pltpu.VMEM((GATE_A_CH, HDR, NCH), f32),
        pltpu.VMEM((N_IN_CH - GATE_A_CH, tm, NCH), f32),
        pltpu.VMEM((tm, D_LRU), f32),
        pltpu.VMEM((tm, D_LRU), bf16),
        pltpu.VMEM((tm, D_LRU), bf16),
        pltpu.VMEM((tm, D_POOL), bf16),
        pltpu.VMEM((tm, D_POOL), bf16),
        pltpu.VMEM((tm, D_MODEL), f32),
        pltpu.VMEM((tm, D_MODEL), bf16),
        pltpu.VMEM((SUBLANES, D_LRU), f32),
    ]
    return pl.pallas_call(
        functools.partial(_mixer_stream_kernel, tm),
        grid=(nb, nt),
        in_specs=[x_spec] + _state_specs(1, lambda b, i: (0, 0, 0)) + [_layer_spec(v, layer) for v in weights],
        out_specs=[x_spec] + _state_specs(1, lambda b, i: (b, 0, 0)),
        out_shape=(jax.ShapeDtypeStruct(x.shape, f32),) + _state_shapes(nb),
        scratch_shapes=scratch,
        compiler_params=pltpu.CompilerParams(
            dimension_semantics=("arbitrary", "arbitrary"), vmem_limit_bytes=VMEM_LIMIT),
        name=name,
    )(x, conv_init, h_init, pool_init, *weights)


def _layer_stream_call(x, conv_init, h_init, pool_init, w, layer, final, tm, name):
    nb, t, _ = x.shape
    tps = t // tm
    n_tiles = nb * tps
    assert tps * tm == t
    weights = _mixer_weights(w) + (w['norm2_g'], w['w_gu'], w['w_down'])
    x_tiles = x.reshape(n_tiles, tm, D_MODEL)
    x_spec = pl.BlockSpec((None, tm, D_MODEL), lambda j: (jnp.minimum(j, n_tiles - 1), 0, 0))
    y_spec = pl.BlockSpec((None, tm, D_MODEL), lambda j: (jnp.maximum(j - 1, 0), 0, 0))
    fg_spec = pl.BlockSpec(w['final_g'].shape, lambda j: (0, 0))
    scratch = [
        pltpu.VMEM((tm, D_MODEL), bf16),
        pltpu.VMEM((GATE_A_CH, HDR, NCH), f32),
        pltpu.VMEM((N_IN_CH - GATE_A_CH, tm, NCH), f32),
        pltpu.VMEM((tm, D_LRU), f32),
        pltpu.VMEM((tm, D_LRU), bf16),
        pltpu.VMEM((tm, D_LRU), bf16),
        pltpu.VMEM((tm, D_POOL), bf16),
        pltpu.VMEM((tm, D_POOL), bf16),
        pltpu.VMEM((tm, D_MODEL), f32),
        pltpu.VMEM((tm, D_MODEL), bf16),
        pltpu.VMEM((SUBLANES, D_LRU), f32),
        pltpu.VMEM((2, tm, D_MODEL), f32),
        pltpu.VMEM((tm, D_MODEL), bf16),
        pltpu.VMEM((tm, D_FF), bf16),
    ]
    y, c_n, h_n, p_n = pl.pallas_call(
        functools.partial(_layer_stream_kernel, tm, tps, n_tiles, final),
        grid=(n_tiles + 1,),
        in_specs=([x_spec] + _state_specs(1, lambda j: (0, 0, 0)) + [_layer_spec(v, layer) for v in weights]
                  + [fg_spec]),
        out_specs=[y_spec] + _state_specs(1, lambda j: (jnp.maximum(j - 1, 0) // tps, 0, 0)),
        out_shape=(jax.ShapeDtypeStruct(x_tiles.shape, f32),) + _state_shapes(nb),
        scratch_shapes=scratch,
        compiler_params=pltpu.CompilerParams(dimension_semantics=("arbitrary",), vmem_limit_bytes=VMEM_LIMIT),
        name=name,
    )(x_tiles, conv_init, h_init, pool_init, *weights, w['final_g'])
    return y.reshape(x.shape), c_n, h_n, p_n


def _ffn_call(x, w, layer, final, tm, name):
    nb, t, _ = x.shape
    nt = t // tm
    assert nt * tm == t
    x_spec = pl.BlockSpec((None, tm, D_MODEL), lambda b, i: (b, i, 0))
    weights = (w['norm2_g'], w['w_gu'], w['w_down'])
    fg_spec = pl.BlockSpec(w['final_g'].shape, lambda b, i: (0, 0))
    return pl.pallas_call(
        functools.partial(_ffn_kernel, final, tm),
        grid=(nb, nt),
        in_specs=[x_spec] + [_layer_spec(v, layer) for v in weights] + [fg_spec],
        out_specs=x_spec,
        out_shape=jax.ShapeDtypeStruct(x.shape, f32),
        scratch_shapes=[pltpu.VMEM((tm, D_MODEL), bf16), pltpu.VMEM((tm, D_FF), bf16)],
        compiler_params=pltpu.CompilerParams(
            dimension_semantics=("arbitrary", "arbitrary"), vmem_limit_bytes=VMEM_LIMIT),
        name=name,
    )(x, *weights, w['final_g'])


def kernel(x_prompt, x_sample, state_conv, state_lru, state_pool, meta_tokens, norm1_g, w_in, conv_w, conv_b,
           lam, w_r, b_r, w_i, b_i, pool_w, pool_scale, w_br_a, w_br_b, w_out, norm2_g, w_gu, w_down, final_g):
    depth = w_in.shape[0]
    ns, ts, _ = x_sample.shape
    rows = lambda v: v.reshape(depth, 1, -1).astype(f32)
    half = GATE_PRESCALE
    in_scale = jnp.concatenate([jnp.ones((D_LRU + D_POOL,), f32), jnp.full((2 * D_MODEL,), half, f32)])
    w = {
        'norm1_g': rows(norm1_g), 'w_in': (w_in * in_scale).astype(bf16), 'conv_w': conv_w, 'conv_b': rows(conv_b),
        'wri': (jnp.concatenate([w_r, w_i], axis=-1) * half).astype(bf16),
        'b_r': rows(b_r * half), 'b_i': rows(b_i * half), 'lam': rows(lam),
        'pool_w': pool_w.astype(bf16), 'pool_scale': rows(pool_scale),
        'w_br_a': (w_br_a * half).astype(bf16), 'w_br_b': (w_br_b * half).astype(bf16), 'w_out': w_out.astype(bf16),
        'norm2_g': rows(norm2_g), 'w_gu': w_gu.astype(bf16), 'w_down': w_down.astype(bf16),
        'final_g': final_g.reshape(1, -1).astype(f32),
    }

    segs_s = [(s * ts, ts, PAST_LEN, True) for s in range(ns)] + [(ns * ts, N_META, 0, False)]
    tm_s = ns * ts + N_META
    xs = jnp.concatenate([x_sample.reshape(ns * ts, D_MODEL), meta_tokens.astype(f32)], axis=0)
    xp = x_prompt
    state_lru = state_lru[:, :, None, :]

    outs = {k: [] for k in ('cp', 'hp', 'pp', 'cs', 'hs', 'ps')}
    for l in range(depth):
        last = l == depth - 1
        xs, c_s, h_s, p_s, c_m, h_m, p_m = _mixer_step_call(xs, state_conv, state_lru, state_pool, w, l, segs_s,
                                                            f"mixer_step_l{l}")
        xs = _ffn_call(xs[None], w, l, last, tm_s, f"ffn_step_l{l}")[0]
        xp, c_p, h_p, p_p = _layer_stream_call(xp, c_m, h_m, p_m, w, l, last, TM_LAYER, f"layer_prompt_l{l}")

        outs['cp'].append(c_p); outs['hp'].append(h_p[:, 0]); outs['pp'].append(p_p)
        outs['cs'].append(c_s); outs['hs'].append(h_s[:, 0]); outs['ps'].append(p_s)

    y_sample = xs[:ns * ts].reshape(ns, ts, D_MODEL)
    return (xp, y_sample, jnp.stack(outs['cp']), jnp.stack(outs['hp']), jnp.stack(outs['pp']),
            jnp.stack(outs['cs']), jnp.stack(outs['hs']), jnp.stack(outs['ps']))
```

```python
import functools

import jax
import jax.numpy as jnp
from jax import lax
from jax.experimental import pallas as pl
from jax.experimental.pallas import tpu as pltpu

D_MODEL = 1024
D_LRU = 1024
D_POOL = 1024
LRU_HEADS = 8
LRU_BLOCK = D_LRU // LRU_HEADS
LRU_C = 8.0
CONV_W = 4
POOL_WINDOWS = (2, 4, 8, 16)
POOL_GROUP = D_POOL // len(POOL_WINDOWS)
POOL_MAX = 16
D_FF = 2816
N_META = 16
PAST_LEN = 2048
EPS = 1e-6

SUBLANES = 8
BF16_ROWS = 16
HDR = 16
NCH = 256
N_IN_CH = 4 * D_MODEL // NCH
CONV_CH = 0
POOL_CH = D_LRU // NCH
GATE_A_CH = (D_LRU + D_POOL) // NCH
GATE_B_CH = (D_LRU + D_POOL + D_MODEL) // NCH
N_OUT_CH = D_MODEL // NCH
TM_PROMPT = 512
TM_LAYER = 256
ROW_CH = 32
VMEM_LIMIT = 60 * 1024 * 1024

f32 = jnp.float32
bf16 = jnp.bfloat16

assert POOL_GROUP == NCH and HDR >= POOL_MAX - 1 and HDR % BF16_ROWS == 0


def _dot(a, b):
    return jnp.dot(a, b, preferred_element_type=f32)


def _chunk(c):
    return slice(c * NCH, (c + 1) * NCH)


def _rmsnorm(x, g):
    y = x * lax.rsqrt(jnp.mean(x * x, axis=-1, keepdims=True) + EPS)
    return y * g


def _norm_chunk(tm):
    return max(c for c in range(BF16_ROWS, 129, BF16_ROWS) if tm % c == 0)


def _sigmoid(x):
    return 0.5 * jnp.tanh(0.5 * x) + 0.5


def _shift_rows(x, k):
    return pltpu.roll(x, k, axis=0)


def _for_rows(total, chunk, fn):
    n = total // chunk
    assert n * chunk == total
    if n == 1:
        fn(0)
    else:
        def body(k, c):
            fn(pl.multiple_of(k * chunk, chunk))
            return c
        lax.fori_loop(0, n, body, 0)


def _norm_to_bf16(x_ref, g_ref, xn_s, tm):
    nc = _norm_chunk(tm)

    def norm_rows(r0):
        rows = pl.ds(r0, nc)
        xn_s[rows, :] = _rmsnorm(x_ref[rows, :], g_ref[...]).astype(bf16)
    _for_rows(tm, nc, norm_rows)


def _conv_taps(blk, cw_ref, cb_ref, cs):
    tap = lambda k: _shift_rows(blk, CONV_W - 1 - k)[HDR:] if k < CONV_W - 1 else blk[HDR:]
    acc = cb_ref[:, cs] + cw_ref[0:1, cs] * tap(0)
    for k in range(1, CONV_W):
        acc = acc + cw_ref[k:k + 1, cs] * tap(k)
    return acc


def _lam_scaled(lam_ref):
    lam = lam_ref[...]
    log_sig_lam = jnp.minimum(lam, 0.0) - jnp.log1p(jnp.exp(-jnp.abs(lam)))
    return LRU_C * log_sig_lam


def _lru_coeffs(r_pre, i_pre, c, lam_c, start_mask=None):
    r = _sigmoid(r_pre)
    i = _sigmoid(i_pre)
    log_a = r * lam_c
    a = jnp.exp(log_a)
    th = jnp.tanh(log_a)
    mult = jnp.sqrt(-2.0 * th / (1.0 - th))
    if start_mask is not None:
        mult = jnp.where(start_mask, 1.0, mult)
    return a, mult * (i * c)


def _scan_block(a, b, h):
    sub = lax.broadcasted_iota(jnp.int32, a.shape, 0)
    for d in (1, 2, 4):
        keep = sub >= d
        a_prev = pltpu.roll(a, d, axis=0)
        b_prev = pltpu.roll(b, d, axis=0)
        b = jnp.where(keep, a * b_prev, 0.0) + b
        a = jnp.where(keep, a * a_prev, a)
    return a * h + b


def _scan_rows(a, b, h):
    outs = []
    for k in range(a.shape[0] // SUBLANES):
        blk = slice(k * SUBLANES, (k + 1) * SUBLANES)
        outs.append(_scan_block(a[blk], b[blk], h))
        h = outs[-1][SUBLANES - 1:SUBLANES, :]
    return jnp.concatenate(outs, axis=0), h


def _pool_diff(blk, w, cnt=None):
    xb = blk[HDR:]
    tot, span = blk, 1
    while span < w:
        tot = tot + _shift_rows(tot, span)
        span *= 2
    tot = tot[HDR:]
    pooled = tot * (1.0 / w) if cnt is None else tot / cnt
    return pooled - xb


def _pool_project(g, d_s, yb_s, pw_ref, ps_ref):
    gs = _chunk(g)
    yb_s[:, gs] = (_dot(d_s[:, gs], pw_ref[g]) * ps_ref[:, gs]).astype(bf16)


def _branch_b(c, gate_b, yb_s, wb_ref, mb_s):
    mb_s[:, _chunk(c)] = _sigmoid(gate_b) * _dot(yb_s[...], wb_ref[:, _chunk(c)])


def _merge(c, gate_a, ya_s, wa_ref, mb_s, m_s):
    cs = _chunk(c)
    m_s[:, cs] = (_sigmoid(gate_a) * _dot(ya_s[...], wa_ref[:, cs]) + mb_s[:, cs]).astype(bf16)


def _project_out(c, m_s, wo_ref, x_ref, y_ref):
    cs = _chunk(c)
    y_ref[:, cs] = x_ref[:, cs] + _dot(m_s[...], wo_ref[:, cs])


def _mixer_step_kernel(segs, tm,
                       x_ref, convp_ref, hp_ref, poolp_ref, g1_ref, win_ref, cw_ref, cb_ref, wri_ref,
                       br_ref, bi_ref, lam_ref, pw_ref, ps_ref, wa_ref, wb_ref, wo_ref,
                       y_ref, convn_ref, hn_ref, pooln_ref, convm_ref, hm_ref, poolm_ref,
                       xn_s, proj, gbuf, cbuf, cbf, a_s, b_s, ya_s, d_s, yb_s, mb_s, m_s):
    n_hist = GATE_A_CH
    assert sum(1 for seg in segs if not seg[3]) == 1
    for s, (_, _, _, has_state) in enumerate(segs):
        for c in range(n_hist):
            proj[s, c, 0:HDR, :] = jnp.zeros((HDR, NCH), f32)
        if has_state:
            for c in range(POOL_CH):
                proj[s, CONV_CH + c, HDR - (CONV_W - 1):HDR, :] = convp_ref[s, :, _chunk(c)]
                proj[s, POOL_CH + c, HDR - (POOL_MAX - 1):HDR, :] = poolp_ref[s, :, _chunk(c)]

    _norm_to_bf16(x_ref, g1_ref, xn_s, tm)

    for c in range(N_IN_CH):
        p = _dot(xn_s[...], win_ref[:, _chunk(c)])
        if c < n_hist:
            for s, (row0, length, _, _) in enumerate(segs):
                proj[s, c, HDR:HDR + length, :] = p[row0:row0 + length]
        else:
            gbuf[c - n_hist] = p

    for s, (row0, length, _, has_state) in enumerate(segs):
        n = min(ROW_CH, length)

        def conv_rows(t0, s=s, row0=row0, n=n):
            rows = pl.ds(row0 + t0, n)
            for c in range(POOL_CH):
                acc = _conv_taps(proj[s, CONV_CH + c, pl.ds(t0, n + HDR), :], cw_ref, cb_ref, _chunk(c))
                cbuf[rows, _chunk(c)] = acc
                cbf[rows, _chunk(c)] = acc.astype(bf16)
        _for_rows(length, n, conv_rows)
        conv_out, pool_out, k = (convn_ref, pooln_ref, s) if has_state else (convm_ref, poolm_ref, 0)
        for c in range(POOL_CH):
            conv_out[k, :, _chunk(c)] = proj[s, CONV_CH + c, HDR + length - (CONV_W - 1):HDR + length, :]
            pool_out[k, :, _chunk(c)] = proj[s, POOL_CH + c, HDR + length - (POOL_MAX - 1):HDR + length, :]

    for h in range(LRU_HEADS):
        hs = slice(h * LRU_BLOCK, (h + 1) * LRU_BLOCK)
        ri = _dot(cbf[:, hs], wri_ref[h])
        a_s[:, hs] = ri[:, :LRU_BLOCK] + br_ref[:, hs]
        b_s[:, hs] = ri[:, LRU_BLOCK:] + bi_ref[:, hs]
    lam_c = _lam_scaled(lam_ref)

    for s, (row0, length, pos0, has_state) in enumerate(segs):
        n = min(ROW_CH, length)

        def mix_rows(t0, h, s=s, row0=row0, n=n, pos0=pos0):
            rows = pl.ds(row0 + t0, n)
            pos = pos0 + t0 + lax.broadcasted_iota(jnp.int32, (n, 1), 0)
            start = (pos == 0) if pos0 == 0 else None
            a, b = _lru_coeffs(a_s[rows, :], b_s[rows, :], cbuf[rows, :], lam_c, start)
            states, h = _scan_rows(a, b, h)
            ya_s[rows, :] = states.astype(bf16)
            cnt = None if pos0 >= POOL_MAX - 1 else pos + 1
            for g, w in enumerate(POOL_WINDOWS):
                cnt_g = None if cnt is None else jnp.minimum(w, cnt).astype(f32)
                d_s[rows, _chunk(g)] = _pool_diff(proj[s, POOL_CH + g, pl.ds(t0, n + HDR), :], w, cnt_g).astype(bf16)
            return h

        steps = length // n
        h = hp_ref[s] if has_state else jnp.zeros((1, D_LRU), f32)
        if steps == 1:
            h = mix_rows(0, h)
        else:
            h = lax.fori_loop(0, steps, lambda k, h: mix_rows(pl.multiple_of(k * n, n), h), h)
        if has_state:
            hn_ref[s] = h
        else:
            hm_ref[0] = h

    for g in range(len(POOL_WINDOWS)):
        _pool_project(g, d_s, yb_s, pw_ref, ps_ref)
    for c in range(N_OUT_CH):
        _branch_b(c, gbuf[GATE_B_CH - n_hist + c], yb_s, wb_ref, mb_s)
        _merge(c, gbuf[GATE_A_CH - n_hist + c], ya_s, wa_ref, mb_s, m_s)
    for c in range(N_OUT_CH):
        _project_out(c, m_s, wo_ref, x_ref, y_ref)


def _mixer_stream_kernel(tm, x_ref, convp_ref, hp_ref, poolp_ref, g1_ref, win_ref, cw_ref, cb_ref, wri_ref,
                         br_ref, bi_ref, lam_ref, pw_ref, ps_ref, wa_ref, wb_ref, wo_ref,
                         y_ref, convn_ref, hn_ref, pooln_ref,
                         xn_s, hist, gbuf, cbuf, cbf, ya_s, d_s, yb_s, mb_s, m_s, hcar):
    n_hist = GATE_A_CH

    @pl.when(pl.program_id(1) == 0)
    def _():
        for c in range(POOL_CH):
            hist[CONV_CH + c] = jnp.zeros((HDR, NCH), f32)
            hist[POOL_CH + c] = jnp.zeros((HDR, NCH), f32)
            hist[CONV_CH + c, HDR - (CONV_W - 1):HDR, :] = convp_ref[0, :, _chunk(c)]
            hist[POOL_CH + c, HDR - (POOL_MAX - 1):HDR, :] = poolp_ref[0, :, _chunk(c)]
        hcar[0:1, :] = hp_ref[0]

    _norm_to_bf16(x_ref, g1_ref, xn_s, tm)
    lam_c = _lam_scaled(lam_ref)

    def project_with_history(c):
        p = _dot(xn_s[...], win_ref[:, _chunk(c)])
        ext = jnp.concatenate([hist[c], p], axis=0)
        hist[c] = p[tm - HDR:]
        return p, ext

    def conv_chunk(c):
        p, ext = project_with_history(CONV_CH + c)
        acc = _conv_taps(ext, cw_ref, cb_ref, _chunk(c))
        cbuf[:, _chunk(c)] = acc
        cbf[:, _chunk(c)] = acc.astype(bf16)
        convn_ref[0, :, _chunk(c)] = p[tm - (CONV_W - 1):]

    def pool_chunk(g):
        p, ext = project_with_history(POOL_CH + g)
        d_s[:, _chunk(g)] = _pool_diff(ext, POOL_WINDOWS[g]).astype(bf16)
        pooln_ref[0, :, _chunk(g)] = p[tm - (POOL_MAX - 1):]

    def gate_chunk(c):
        gbuf[c - n_hist] = _dot(xn_s[...], win_ref[:, _chunk(c)])

    def lru_head(h):
        hs = slice(h * LRU_BLOCK, (h + 1) * LRU_BLOCK)
        ri = _dot(cbf[:, hs], wri_ref[h])
        a, b = _lru_coeffs(ri[:, :LRU_BLOCK] + br_ref[:, hs], ri[:, LRU_BLOCK:] + bi_ref[:, hs],
                           cbuf[:, hs], lam_c[:, hs])
        states, last = _scan_rows(a, b, hcar[0:1, hs])
        ya_s[:, hs] = states.astype(bf16)
        hcar[0:1, hs] = last
        hn_ref[0, :, hs] = last

    for c in range(POOL_CH):
        conv_chunk(c)
    others = [functools.partial(pool_chunk, g) for g in range(len(POOL_WINDOWS))]
    others += [functools.partial(gate_chunk, GATE_B_CH + c) for c in range(N_OUT_CH)]
    others += [functools.partial(_pool_project, g, d_s, yb_s, pw_ref, ps_ref) for g in range(len(POOL_WINDOWS))]
    others += [lambda c=c: _branch_b(c, gbuf[GATE_B_CH - n_hist + c], yb_s, wb_ref, mb_s) for c in range(N_OUT_CH)]
    others += [functools.partial(gate_chunk, GATE_A_CH + c) for c in range(N_OUT_CH)]
    for h in range(LRU_HEADS):
        lru_head(h)
        lo, hi = (h * len(others)) // LRU_HEADS, ((h + 1) * len(others)) // LRU_HEADS
        for fn in others[lo:hi]:
            fn()

    for c in range(N_OUT_CH):
        _merge(c, gbuf[GATE_A_CH - n_hist + c], ya_s, wa_ref, mb_s, m_s)
    for c in range(N_OUT_CH):
        _project_out(c, m_s, wo_ref, x_ref, y_ref)


def _layer_stream_kernel(tm, tiles_per_stream, n_tiles, final,
                         x_ref, convp_ref, hp_ref, poolp_ref, g1_ref, win_ref, cw_ref, cb_ref, wri_ref,
                         br_ref, bi_ref, lam_ref, pw_ref, ps_ref, wa_ref, wb_ref, wo_ref,
                         g2_ref, wgu_ref, wd_ref, fg_ref,
                         y_ref, convn_ref, hn_ref, pooln_ref,
                         xn_s, hist, gbuf, cbuf, cbf, ya_s, d_s, yb_s, mb_s, m_s, hcar, ymid, xn2_s, h_s):
    j = pl.program_id(0)
    new = j % 2
    cur = 1 - new
    n_hist = GATE_A_CH
    nc = _norm_chunk(tm)

    @pl.when(j == 0)
    def _():
        hist[...] = jnp.zeros(hist.shape, f32)
        hcar[...] = jnp.zeros(hcar.shape, f32)

    for c in range(POOL_CH):
        convn_ref[0, :, _chunk(c)] = hist[CONV_CH + c, HDR - (CONV_W - 1):HDR, :]
        pooln_ref[0, :, _chunk(c)] = hist[POOL_CH + c, HDR - (POOL_MAX - 1):HDR, :]
    hn_ref[0] = hcar[0:1, :]

    @pl.when(jnp.logical_and(j % tiles_per_stream == 0, j < n_tiles))
    def _():
        for c in range(POOL_CH):
            hist[CONV_CH + c] = jnp.zeros((HDR, NCH), f32)
            hist[POOL_CH + c] = jnp.zeros((HDR, NCH), f32)
            hist[CONV_CH + c, HDR - (CONV_W - 1):HDR, :] = convp_ref[0, :, _chunk(c)]
            hist[POOL_CH + c, HDR - (POOL_MAX - 1):HDR, :] = poolp_ref[0, :, _chunk(c)]
        hcar[0:1, :] = hp_ref[0]

    lam_c = _lam_scaled(lam_ref)

    def ffn_norm():
        for r0 in range(0, tm, nc):
            xn2_s[r0:r0 + nc, :] = _rmsnorm(ymid[cur, r0:r0 + nc, :], g2_ref[...]).astype(bf16)

    def ffn_up(c):
        gate = _dot(xn2_s[...], wgu_ref[:, _chunk(c)])
        up = _dot(xn2_s[...], wgu_ref[:, D_FF + c * NCH:D_FF + (c + 1) * NCH])
        h_s[:, _chunk(c)] = (jax.nn.silu(gate) * up).astype(bf16)

    def ffn_down(c):
        y_ref[:, _chunk(c)] = ymid[cur, :, _chunk(c)] + _dot(h_s[...], wd_ref[:, _chunk(c)])

    def ffn_final_norm():
        for r0 in range(0, tm, nc):
            y_ref[r0:r0 + nc, :] = _rmsnorm(y_ref[r0:r0 + nc, :], fg_ref[...])

    def mixer_norm():
        for r0 in range(0, tm, nc):
            xn_s[r0:r0 + nc, :] = _rmsnorm(x_ref[r0:r0 + nc, :], g1_ref[...]).astype(bf16)

    def project_with_history(c):
        p = _dot(xn_s[...], win_ref[:, _chunk(c)])
        ext = jnp.concatenate([hist[c], p], axis=0)
        hist[c] = p[tm - HDR:]
        return ext

    def conv_chunk(c):
        acc = _conv_taps(project_with_history(CONV_CH + c), cw_ref, cb_ref, _chunk(c))
        cbuf[:, _chunk(c)] = acc
        cbf[:, _chunk(c)] = acc.astype(bf16)

    def pool_chunk(g):
        d_s[:, _chunk(g)] = _pool_diff(project_with_history(POOL_CH + g), POOL_WINDOWS[g]).astype(bf16)

    def gate_chunk(c):
        gbuf[c - n_hist] = _dot(xn_s[...], win_ref[:, _chunk(c)])

    def lru_head(h):
        hs = slice(h * LRU_BLOCK, (h + 1) * LRU_BLOCK)
        ri = _dot(cbf[:, hs], wri_ref[h])
        a, b = _lru_coeffs(ri[:, :LRU_BLOCK] + br_ref[:, hs], ri[:, LRU_BLOCK:] + bi_ref[:, hs],
                           cbuf[:, hs], lam_c[:, hs])
        states, last = _scan_rows(a, b, hcar[0:1, hs])
        ya_s[:, hs] = states.astype(bf16)
        hcar[0:1, hs] = last

    def project_out(c):
        ymid[new, :, _chunk(c)] = x_ref[:, _chunk(c)] + _dot(m_s[...], wo_ref[:, _chunk(c)])

    n_ff = D_FF // NCH
    filler = [functools.partial(ffn_up, c) for c in range(n_ff)]
    filler += [functools.partial(ffn_down, c) for c in range(N_OUT_CH)]
    if final:
        filler.append(ffn_final_norm)
    others = [functools.partial(pool_chunk, g) for g in range(len(POOL_WINDOWS))]
    others += [functools.partial(gate_chunk, GATE_B_CH + c) for c in range(N_OUT_CH)]
    others += [functools.partial(_pool_project, g, d_s, yb_s, pw_ref, ps_ref) for g in range(len(POOL_WINDOWS))]
    others += [lambda c=c: _branch_b(c, gbuf[GATE_B_CH - n_hist + c], yb_s, wb_ref, mb_s) for c in range(N_OUT_CH)]
    others += [functools.partial(gate_chunk, GATE_A_CH + c) for c in range(N_OUT_CH)]

    stages = [functools.partial(conv_chunk, c) for c in range(POOL_CH)]
    stages += [functools.partial(lru_head, h) for h in range(LRU_HEADS)]
    first_head = POOL_CH

    def mixer_and_ffn():
        ffn_norm()
        mixer_norm()
        for k, stage in enumerate(stages):
            stage()
            lo, hi = (k * len(filler)) // len(stages), ((k + 1) * len(filler)) // len(stages)
            for fn in filler[lo:hi]:
                fn()
            if k >= first_head:
                h = k - first_head
                lo, hi = (h * len(others)) // LRU_HEADS, ((h + 1) * len(others)) // LRU_HEADS
                for fn in others[lo:hi]:
                    fn()
        for c in range(N_OUT_CH):
            _merge(c, gbuf[GATE_A_CH - n_hist + c], ya_s, wa_ref, mb_s, m_s)
        for c in range(N_OUT_CH):
            project_out(c)

    def mixer_only():
        mixer_norm()
        for k, stage in enumerate(stages):
            stage()
            if k >= first_head:
                h = k - first_head
                lo, hi = (h * len(others)) // LRU_HEADS, ((h + 1) * len(others)) // LRU_HEADS
                for fn in others[lo:hi]:
                    fn()
        for c in range(N_OUT_CH):
            _merge(c, gbuf[GATE_A_CH - n_hist + c], ya_s, wa_ref, mb_s, m_s)
        for c in range(N_OUT_CH):
            project_out(c)

    def ffn_only():
        ffn_norm()
        for fn in filler:
            fn()

    pl.when(j == 0)(mixer_only)
    pl.when(jnp.logical_and(j > 0, j < n_tiles))(mixer_and_ffn)
    pl.when(j == n_tiles)(ffn_only)


def _ffn_kernel(final, tm, x_ref, g2_ref, wgu_ref, wd_ref, fg_ref, y_ref, xn_s, h_s):
    nc = _norm_chunk(tm)
    _norm_to_bf16(x_ref, g2_ref, xn_s, tm)

    for c in range(D_FF // NCH):
        gate = _dot(xn_s[...], wgu_ref[:, c * NCH:(c + 1) * NCH])
        up = _dot(xn_s[...], wgu_ref[:, D_FF + c * NCH:D_FF + (c + 1) * NCH])
        h_s[:, c * NCH:(c + 1) * NCH] = (jax.nn.silu(gate) * up).astype(bf16)

    for c in range(N_OUT_CH):
        y_ref[:, _chunk(c)] = x_ref[:, _chunk(c)] + _dot(h_s[...], wd_ref[:, _chunk(c)])

    if final:
        def final_rows(r0):
            rows = pl.ds(r0, nc)
            y_ref[rows, :] = _rmsnorm(y_ref[rows, :], fg_ref[...])
        _for_rows(tm, nc, final_rows)


def _layer_spec(w, layer):
    nd = w.ndim - 1
    return pl.BlockSpec((None,) + w.shape[1:], lambda *_: (layer,) + (0,) * nd, pipeline_mode=pl.Buffered(1))


def _mixer_weights(w):
    return (w['norm1_g'], w['w_in'], w['conv_w'], w['conv_b'], w['wri'], w['b_r'], w['b_i'],
            w['lam'], w['pool_w'], w['pool_scale'], w['w_br_a'], w['w_br_b'], w['w_out'])


def _state_shapes(n):
    return (jax.ShapeDtypeStruct((n, CONV_W - 1, D_LRU), f32),
            jax.ShapeDtypeStruct((n, 1, D_LRU), f32),
            jax.ShapeDtypeStruct((n, POOL_MAX - 1, D_POOL), f32))


def _state_specs(n, index_map):
    return [pl.BlockSpec((n, CONV_W - 1, D_LRU), index_map), pl.BlockSpec((n, 1, D_LRU), index_map),
            pl.BlockSpec((n, POOL_MAX - 1, D_POOL), index_map)]


def _mixer_step_call(x, state_conv, state_lru, state_pool, w, layer, segs, name):
    tm = x.shape[0]
    ns = state_conv.shape[1]
    max_len = max(seg[1] for seg in segs)
    weights = _mixer_weights(w)
    x_spec = pl.BlockSpec((tm, D_MODEL), lambda i: (0, 0))
    whole = lambda i: (0, 0, 0)
    state_in = [pl.BlockSpec((None,) + s.shape[1:], lambda i: (layer, 0, 0, 0))
                for s in (state_conv, state_lru, state_pool)]
    scratch = [
        pltpu.VMEM((tm, D_MODEL), bf16),
        pltpu.VMEM((len(segs), GATE_A_CH, HDR + max_len, NCH), f32),
        pltpu.VMEM((N_IN_CH - GATE_A_CH, tm, NCH), f32),
        pltpu.VMEM((tm, D_LRU), f32),
        pltpu.VMEM((tm, D_LRU), bf16),
        pltpu.VMEM((tm, D_LRU), f32),
        pltpu.VMEM((tm, D_LRU), f32),
        pltpu.VMEM((tm, D_LRU), bf16),
        pltpu.VMEM((tm, D_POOL), bf16),
        pltpu.VMEM((tm, D_POOL), bf16),
        pltpu.VMEM((tm, D_MODEL), f32),
        pltpu.VMEM((tm, D_MODEL), bf16),
    ]
    return pl.pallas_call(
        functools.partial(_mixer_step_kernel, tuple(segs), tm),
        grid=(1,),
        in_specs=[x_spec] + state_in + [_layer_spec(v, layer) for v in weights],
        out_specs=[x_spec] + _state_specs(ns, whole) + _state_specs(1, whole),
        out_shape=(jax.ShapeDtypeStruct(x.shape, f32),) + _state_shapes(ns) + _state_shapes(1),
        scratch_shapes=scratch,
        compiler_params=pltpu.CompilerParams(dimension_semantics=("arbitrary",), vmem_limit_bytes=VMEM_LIMIT),
        name=name,
    )(x, state_conv, state_lru, state_pool, *weights)


def _mixer_stream_call(x, conv_init, h_init, pool_init, w, layer, tm, name):
    nb, t, _ = x.shape
    nt = t // tm
    assert nt * tm == t
    weights = _mixer_weights(w)
    x_spec = pl.BlockSpec((None, tm, D_MODEL), lambda b, i: (b, i, 0))
    scratch = [
        pltpu.VMEM((tm, D_MODEL), bf16),
        pltpu.VMEM((GATE_A_CH, HDR, NCH), f32),
        pltpu.VMEM((N_IN_CH - GATE_A_CH, tm, NCH), f32),
        pltpu.VMEM((tm, D_LRU), f32),
        pltpu.VMEM((tm, D_LRU), bf16),
        pltpu.VMEM((tm, D_LRU), bf16),
        pltpu.VMEM((tm, D_POOL), bf16),
        pltpu.VMEM((tm, D_POOL), bf16),
        pltpu.VMEM((tm, D_MODEL), f32),
        pltpu.VMEM((tm, D_MODEL), bf16),
        pltpu.VMEM((SUBLANES, D_LRU), f32),
    ]
    return pl.pallas_call(
        functools.partial(_mixer_stream_kernel, tm),
        grid=(nb, nt),
        in_specs=[x_spec] + _state_specs(1, lambda b, i: (0, 0, 0)) + [_layer_spec(v, layer) for v in weights],
        out_specs=[x_spec] + _state_specs(1, lambda b, i: (b, 0, 0)),
        out_shape=(jax.ShapeDtypeStruct(x.shape, f32),) + _state_shapes(nb),
        scratch_shapes=scratch,
        compiler_params=pltpu.CompilerParams(
            dimension_semantics=("arbitrary", "arbitrary"), vmem_limit_bytes=VMEM_LIMIT),
        name=name,
    )(x, conv_init, h_init, pool_init, *weights)


def _layer_stream_call(x, conv_init, h_init, pool_init, w, layer, final, tm, name):
    nb, t, _ = x.shape
    tps = t // tm
    n_tiles = nb * tps
    assert tps * tm == t
    weights = _mixer_weights(w) + (w['norm2_g'], w['w_gu'], w['w_down'])
    x_tiles = x.reshape(n_tiles, tm, D_MODEL)
    x_spec = pl.BlockSpec((None, tm, D_MODEL), lambda j: (jnp.minimum(j, n_tiles - 1), 0, 0))
    y_spec = pl.BlockSpec((None, tm, D_MODEL), lambda j: (jnp.maximum(j - 1, 0), 0, 0))
    fg_spec = pl.BlockSpec(w['final_g'].shape, lambda j: (0, 0))
    scratch = [
        pltpu.VMEM((tm, D_MODEL), bf16),
        pltpu.VMEM((GATE_A_CH, HDR, NCH), f32),
        pltpu.VMEM((N_IN_CH - GATE_A_CH, tm, NCH), f32),
        pltpu.VMEM((tm, D_LRU), f32),
        pltpu.VMEM((tm, D_LRU), bf16),
        pltpu.VMEM((tm, D_LRU), bf16),
        pltpu.VMEM((tm, D_POOL), bf16),
        pltpu.VMEM((tm, D_POOL), bf16),
        pltpu.VMEM((tm, D_MODEL), f32),
        pltpu.VMEM((tm, D_MODEL), bf16),
        pltpu.VMEM((SUBLANES, D_LRU), f32),
        pltpu.VMEM((2, tm, D_MODEL), f32),
        pltpu.VMEM((tm, D_MODEL), bf16),
        pltpu.VMEM((tm, D_FF), bf16),
    ]
    y, c_n, h_n, p_n = pl.pallas_call(
        functools.partial(_layer_stream_kernel, tm, tps, n_tiles, final),
        grid=(n_tiles + 1,),
        in_specs=([x_spec] + _state_specs(1, lambda j: (0, 0, 0)) + [_layer_spec(v, layer) for v in weights]
                  + [fg_spec]),
        out_specs=[y_spec] + _state_specs(1, lambda j: (jnp.maximum(j - 1, 0) // tps, 0, 0)),
        out_shape=(jax.ShapeDtypeStruct(x_tiles.shape, f32),) + _state_shapes(nb),
        scratch_shapes=scratch,
        compiler_params=pltpu.CompilerParams(dimension_semantics=("arbitrary",), vmem_limit_bytes=VMEM_LIMIT),
        name=name,
    )(x_tiles, conv_init, h_init, pool_init, *weights, w['final_g'])
    return y.reshape(x.shape), c_n, h_n, p_n


def _ffn_call(x, w, layer, final, tm, name):
    nb, t, _ = x.shape
    nt = t // tm
    assert nt * tm == t
    x_spec = pl.BlockSpec((None, tm, D_MODEL), lambda b, i: (b, i, 0))
    weights = (w['norm2_g'], w['w_gu'], w['w_down'])
    fg_spec = pl.BlockSpec(w['final_g'].shape, lambda b, i: (0, 0))
    return pl.pallas_call(
        functools.partial(_ffn_kernel, final, tm),
        grid=(nb, nt),
        in_specs=[x_spec] + [_layer_spec(v, layer) for v in weights] + [fg_spec],
        out_specs=x_spec,
        out_shape=jax.ShapeDtypeStruct(x.shape, f32),
        scratch_shapes=[pltpu.VMEM((tm, D_MODEL), bf16), pltpu.VMEM((tm, D_FF), bf16)],
        compiler_params=pltpu.CompilerParams(
            dimension_semantics=("arbitrary", "arbitrary"), vmem_limit_bytes=VMEM_LIMIT),
        name=name,
    )(x, *weights, w['final_g'])


def kernel(x_prompt, x_sample, state_conv, state_lru, state_pool, meta_tokens, norm1_g, w_in, conv_w, conv_b,
           lam, w_r, b_r, w_i, b_i, pool_w, pool_scale, w_br_a, w_br_b, w_out, norm2_g, w_gu, w_down, final_g):
    depth = w_in.shape[0]
    ns, ts, _ = x_sample.shape
    rows = lambda v: v.reshape(depth, 1, -1).astype(f32)
    w = {
        'norm1_g': rows(norm1_g), 'w_in': w_in.astype(bf16), 'conv_w': conv_w, 'conv_b': rows(conv_b),
        'wri': jnp.concatenate([w_r, w_i], axis=-1).astype(bf16),
        'b_r': rows(b_r), 'b_i': rows(b_i), 'lam': rows(lam),
        'pool_w': pool_w.astype(bf16), 'pool_scale': rows(pool_scale),
        'w_br_a': w_br_a.astype(bf16), 'w_br_b': w_br_b.astype(bf16), 'w_out': w_out.astype(bf16),
        'norm2_g': rows(norm2_g), 'w_gu': w_gu.astype(bf16), 'w_down': w_down.astype(bf16),
        'final_g': final_g.reshape(1, -1).astype(f32),
    }

    segs_s = [(s * ts, ts, PAST_LEN, True) for s in range(ns)] + [(ns * ts, N_META, 0, False)]
    tm_s = ns * ts + N_META
    xs = jnp.concatenate([x_sample.reshape(ns * ts, D_MODEL), meta_tokens.astype(f32)], axis=0)
    xp = x_prompt
    state_lru = state_lru[:, :, None, :]

    outs = {k: [] for k in ('cp', 'hp', 'pp', 'cs', 'hs', 'ps')}
    for l in range(depth):
        last = l == depth - 1
        xs, c_s, h_s, p_s, c_m, h_m, p_m = _mixer_step_call(xs, state_conv, state_lru, state_pool, w, l, segs_s,
                                                            f"mixer_step_l{l}")
        xs = _ffn_call(xs[None], w, l, last, tm_s, f"ffn_step_l{l}")[0]
        xp, c_p, h_p, p_p = _layer_stream_call(xp, c_m, h_m, p_m, w, l, last, TM_LAYER, f"layer_prompt_l{l}")

        outs['cp'].append(c_p); outs['hp'].append(h_p[:, 0]); outs['pp'].append(p_p)
        outs['cs'].append(c_s); outs['hs'].append(h_s[:, 0]); outs['ps'].append(p_s)

    y_sample = xs[:ns * ts].reshape(ns, ts, D_MODEL)
    return (xp, y_sample, jnp.stack(outs['cp']), jnp.stack(outs['hp']), jnp.stack(outs['pp']),
            jnp.stack(outs['cs']), jnp.stack(outs['hs']), jnp.stack(outs['ps']))
```

```python
import functools

import jax
import jax.numpy as jnp
from jax import lax
from jax.experimental import pallas as pl
from jax.experimental.pallas import tpu as pltpu

D_MODEL = 1024
D_LRU = 1024
D_POOL = 1024
LRU_HEADS = 8
LRU_BLOCK = D_LRU // LRU_HEADS
LRU_C = 8.0
CONV_W = 4
POOL_WINDOWS = (2, 4, 8, 16)
POOL_GROUP = D_POOL // len(POOL_WINDOWS)
POOL_MAX = 16
D_FF = 2816
N_META = 16
PAST_LEN = 2048
EPS = 1e-6

SUBLANES = 8
BF16_ROWS = 16
HDR = 16
NCH = 256
N_IN_CH = 4 * D_MODEL // NCH
CONV_CH = 0
POOL_CH = D_LRU // NCH
GATE_A_CH = (D_LRU + D_POOL) // NCH
GATE_B_CH = (D_LRU + D_POOL + D_MODEL) // NCH
N_OUT_CH = D_MODEL // NCH
TM_PROMPT = 512
TM_LAYER = 256
ROW_CH = 32
VMEM_LIMIT = 60 * 1024 * 1024

f32 = jnp.float32
bf16 = jnp.bfloat16

assert POOL_GROUP == NCH and HDR >= POOL_MAX - 1 and HDR % BF16_ROWS == 0


def _dot(a, b):
    return jnp.dot(a, b, preferred_element_type=f32)


def _chunk(c):
    return slice(c * NCH, (c + 1) * NCH)


def _rmsnorm(x, g):
    y = x * lax.rsqrt(jnp.mean(x * x, axis=-1, keepdims=True) + EPS)
    return y * g


def _norm_chunk(tm):
    return max(c for c in range(BF16_ROWS, 129, BF16_ROWS) if tm % c == 0)


def _sigmoid(x):
    return 0.5 * jnp.tanh(0.5 * x) + 0.5


def _shift_rows(x, k):
    return pltpu.roll(x, k, axis=0)


def _for_rows(total, chunk, fn):
    n = total // chunk
    assert n * chunk == total
    if n == 1:
        fn(0)
    else:
        def body(k, c):
            fn(pl.multiple_of(k * chunk, chunk))
            return c
        lax.fori_loop(0, n, body, 0)


def _norm_to_bf16(x_ref, g_ref, xn_s, tm):
    nc = _norm_chunk(tm)

    def norm_rows(r0):
        rows = pl.ds(r0, nc)
        xn_s[rows, :] = _rmsnorm(x_ref[rows, :], g_ref[...]).astype(bf16)
    _for_rows(tm, nc, norm_rows)


def _conv_taps(blk, cw_ref, cb_ref, cs):
    tap = lambda k: _shift_rows(blk, CONV_W - 1 - k)[HDR:] if k < CONV_W - 1 else blk[HDR:]
    acc = cb_ref[:, cs] + cw_ref[0:1, cs] * tap(0)
    for k in range(1, CONV_W):
        acc = acc + cw_ref[k:k + 1, cs] * tap(k)
    return acc


def _lam_scaled(lam_ref):
    lam = lam_ref[...]
    log_sig_lam = jnp.minimum(lam, 0.0) - jnp.log1p(jnp.exp(-jnp.abs(lam)))
    return LRU_C * log_sig_lam


def _lru_coeffs(r_pre, i_pre, c, lam_c, start_mask=None):
    r = _sigmoid(r_pre)
    i = _sigmoid(i_pre)
    log_a = r * lam_c
    a = jnp.exp(log_a)
    th = jnp.tanh(log_a)
    mult = jnp.sqrt(-2.0 * th / (1.0 - th))
    if start_mask is not None:
        mult = jnp.where(start_mask, 1.0, mult)
    return a, mult * (i * c)


def _scan_block(a, b, h):
    sub = lax.broadcasted_iota(jnp.int32, a.shape, 0)
    for d in (1, 2, 4):
        keep = sub >= d
        a_prev = pltpu.roll(a, d, axis=0)
        b_prev = pltpu.roll(b, d, axis=0)
        b = jnp.where(keep, a * b_prev, 0.0) + b
        a = jnp.where(keep, a * a_prev, a)
    return a * h + b


def _scan_rows(a, b, h):
    outs = []
    for k in range(a.shape[0] // SUBLANES):
        blk = slice(k * SUBLANES, (k + 1) * SUBLANES)
        outs.append(_scan_block(a[blk], b[blk], h))
        h = outs[-1][SUBLANES - 1:SUBLANES, :]
    return jnp.concatenate(outs, axis=0), h


def _pool_diff(blk, w, cnt=None):
    xb = blk[HDR:]
    tot, span = blk, 1
    while span < w:
        tot = tot + _shift_rows(tot, span)
        span *= 2
    tot = tot[HDR:]
    pooled = tot * (1.0 / w) if cnt is None else tot / cnt
    return pooled - xb


def _pool_project(g, d_s, yb_s, pw_ref, ps_ref):
    gs = _chunk(g)
    yb_s[:, gs] = (_dot(d_s[:, gs], pw_ref[g]) * ps_ref[:, gs]).astype(bf16)


def _branch_b(c, gate_b, yb_s, wb_ref, mb_s):
    mb_s[:, _chunk(c)] = _sigmoid(gate_b) * _dot(yb_s[...], wb_ref[:, _chunk(c)])


def _merge(c, gate_a, ya_s, wa_ref, mb_s, m_s):
    cs = _chunk(c)
    m_s[:, cs] = (_sigmoid(gate_a) * _dot(ya_s[...], wa_ref[:, cs]) + mb_s[:, cs]).astype(bf16)


def _project_out(c, m_s, wo_ref, x_ref, y_ref):
    cs = _chunk(c)
    y_ref[:, cs] = x_ref[:, cs] + _dot(m_s[...], wo_ref[:, cs])


def _mixer_step_kernel(segs, tm,
                       x_ref, convp_ref, hp_ref, poolp_ref, g1_ref, win_ref, cw_ref, cb_ref, wri_ref,
                       br_ref, bi_ref, lam_ref, pw_ref, ps_ref, wa_ref, wb_ref, wo_ref,
                       y_ref, convn_ref, hn_ref, pooln_ref, convm_ref, hm_ref, poolm_ref,
                       xn_s, proj, gbuf, cbuf, cbf, a_s, b_s, ya_s, d_s, yb_s, mb_s, m_s):
    n_hist = GATE_A_CH
    assert sum(1 for seg in segs if not seg[3]) == 1
    for s, (_, _, _, has_state) in enumerate(segs):
        for c in range(n_hist):
            proj[s, c, 0:HDR, :] = jnp.zeros((HDR, NCH), f32)
        if has_state:
            for c in range(POOL_CH):
                proj[s, CONV_CH + c, HDR - (CONV_W - 1):HDR, :] = convp_ref[s, :, _chunk(c)]
                proj[s, POOL_CH + c, HDR - (POOL_MAX - 1):HDR, :] = poolp_ref[s, :, _chunk(c)]

    _norm_to_bf16(x_ref, g1_ref, xn_s, tm)

    for c in range(N_IN_CH):
        p = _dot(xn_s[...], win_ref[:, _chunk(c)])
        if c < n_hist:
            for s, (row0, length, _, _) in enumerate(segs):
                proj[s, c, HDR:HDR + length, :] = p[row0:row0 + length]
        else:
            gbuf[c - n_hist] = p

    for s, (row0, length, _, has_state) in enumerate(segs):
        n = min(ROW_CH, length)

        def conv_rows(t0, s=s, row0=row0, n=n):
            rows = pl.ds(row0 + t0, n)
            for c in range(POOL_CH):
                acc = _conv_taps(proj[s, CONV_CH + c, pl.ds(t0, n + HDR), :], cw_ref, cb_ref, _chunk(c))
                cbuf[rows, _chunk(c)] = acc
                cbf[rows, _chunk(c)] = acc.astype(bf16)
        _for_rows(length, n, conv_rows)
        conv_out, pool_out, k = (convn_ref, pooln_ref, s) if has_state else (convm_ref, poolm_ref, 0)
        for c in range(POOL_CH):
            conv_out[k, :, _chunk(c)] = proj[s, CONV_CH + c, HDR + length - (CONV_W - 1):HDR + length, :]
            pool_out[k, :, _chunk(c)] = proj[s, POOL_CH + c, HDR + length - (POOL_MAX - 1):HDR + length, :]

    for h in range(LRU_HEADS):
        hs = slice(h * LRU_BLOCK, (h + 1) * LRU_BLOCK)
        ri = _dot(cbf[:, hs], wri_ref[h])
        a_s[:, hs] = ri[:, :LRU_BLOCK] + br_ref[:, hs]
        b_s[:, hs] = ri[:, LRU_BLOCK:] + bi_ref[:, hs]
    lam_c = _lam_scaled(lam_ref)

    for s, (row0, length, pos0, has_state) in enumerate(segs):
        n = min(ROW_CH, length)

        def mix_rows(t0, h, s=s, row0=row0, n=n, pos0=pos0):
            rows = pl.ds(row0 + t0, n)
            pos = pos0 + t0 + lax.broadcasted_iota(jnp.int32, (n, 1), 0)
            start = (pos == 0) if pos0 == 0 else None
            a, b = _lru_coeffs(a_s[rows, :], b_s[rows, :], cbuf[rows, :], lam_c, start)
            states, h = _scan_rows(a, b, h)
            ya_s[rows, :] = states.astype(bf16)
            cnt = None if pos0 >= POOL_MAX - 1 else pos + 1
            for g, w in enumerate(POOL_WINDOWS):
                cnt_g = None if cnt is None else jnp.minimum(w, cnt).astype(f32)
                d_s[rows, _chunk(g)] = _pool_diff(proj[s, POOL_CH + g, pl.ds(t0, n + HDR), :], w, cnt_g).astype(bf16)
            return h

        steps = length // n
        h = hp_ref[s] if has_state else jnp.zeros((1, D_LRU), f32)
        if steps == 1:
            h = mix_rows(0, h)
        else:
            h = lax.fori_loop(0, steps, lambda k, h: mix_rows(pl.multiple_of(k * n, n), h), h)
        if has_state:
            hn_ref[s] = h
        else:
            hm_ref[0] = h

    for g in range(len(POOL_WINDOWS)):
        _pool_project(g, d_s, yb_s, pw_ref, ps_ref)
    for c in range(N_OUT_CH):
        _branch_b(c, gbuf[GATE_B_CH - n_hist + c], yb_s, wb_ref, mb_s)
        _merge(c, gbuf[GATE_A_CH - n_hist + c], ya_s, wa_ref, mb_s, m_s)
    for c in range(N_OUT_CH):
        _project_out(c, m_s, wo_ref, x_ref, y_ref)


def _mixer_stream_kernel(tm, x_ref, convp_ref, hp_ref, poolp_ref, g1_ref, win_ref, cw_ref, cb_ref, wri_ref,
                         br_ref, bi_ref, lam_ref, pw_ref, ps_ref, wa_ref, wb_ref, wo_ref,
                         y_ref, convn_ref, hn_ref, pooln_ref,
                         xn_s, hist, gbuf, cbuf, cbf, ya_s, d_s, yb_s, mb_s, m_s, hcar):
    n_hist = GATE_A_CH

    @pl.when(pl.program_id(1) == 0)
    def _():
        for c in range(POOL_CH):
            hist[CONV_CH + c] = jnp.zeros((HDR, NCH), f32)
            hist[POOL_CH + c] = jnp.zeros((HDR, NCH), f32)
            hist[CONV_CH + c, HDR - (CONV_W - 1):HDR, :] = convp_ref[0, :, _chunk(c)]
            hist[POOL_CH + c, HDR - (POOL_MAX - 1):HDR, :] = poolp_ref[0, :, _chunk(c)]
        hcar[0:1, :] = hp_ref[0]

    _norm_to_bf16(x_ref, g1_ref, xn_s, tm)
    lam_c = _lam_scaled(lam_ref)

    def project_with_history(c):
        p = _dot(xn_s[...], win_ref[:, _chunk(c)])
        ext = jnp.concatenate([hist[c], p], axis=0)
        hist[c] = p[tm - HDR:]
        return p, ext

    def conv_chunk(c):
        p, ext = project_with_history(CONV_CH + c)
        acc = _conv_taps(ext, cw_ref, cb_ref, _chunk(c))
        cbuf[:, _chunk(c)] = acc
        cbf[:, _chunk(c)] = acc.astype(bf16)
        convn_ref[0, :, _chunk(c)] = p[tm - (CONV_W - 1):]

    def pool_chunk(g):
        p, ext = project_with_history(POOL_CH + g)
        d_s[:, _chunk(g)] = _pool_diff(ext, POOL_WINDOWS[g]).astype(bf16)
        pooln_ref[0, :, _chunk(g)] = p[tm - (POOL_MAX - 1):]

    def gate_chunk(c):
        gbuf[c - n_hist] = _dot(xn_s[...], win_ref[:, _chunk(c)])

    def lru_head(h):
        hs = slice(h * LRU_BLOCK, (h + 1) * LRU_BLOCK)
        ri = _dot(cbf[:, hs], wri_ref[h])
        a, b = _lru_coeffs(ri[:, :LRU_BLOCK] + br_ref[:, hs], ri[:, LRU_BLOCK:] + bi_ref[:, hs],
                           cbuf[:, hs], lam_c[:, hs])
        states, last = _scan_rows(a, b, hcar[0:1, hs])
        ya_s[:, hs] = states.astype(bf16)
        hcar[0:1, hs] = last
        hn_ref[0, :, hs] = last

    for c in range(POOL_CH):
        conv_chunk(c)
    others = [functools.partial(pool_chunk, g) for g in range(len(POOL_WINDOWS))]
    others += [functools.partial(gate_chunk, GATE_B_CH + c) for c in range(N_OUT_CH)]
    others += [functools.partial(_pool_project, g, d_s, yb_s, pw_ref, ps_ref) for g in range(len(POOL_WINDOWS))]
    others += [lambda c=c: _branch_b(c, gbuf[GATE_B_CH - n_hist + c], yb_s, wb_ref, mb_s) for c in range(N_OUT_CH)]
    others += [functools.partial(gate_chunk, GATE_A_CH + c) for c in range(N_OUT_CH)]
    for h in range(LRU_HEADS):
        lru_head(h)
        lo, hi = (h * len(others)) // LRU_HEADS, ((h + 1) * len(others)) // LRU_HEADS
        for fn in others[lo:hi]:
            fn()

    for c in range(N_OUT_CH):
        _merge(c, gbuf[GATE_A_CH - n_hist + c], ya_s, wa_ref, mb_s, m_s)
    for c in range(N_OUT_CH):
        _project_out(c, m_s, wo_ref, x_ref, y_ref)


def _layer_stream_kernel(tm, tiles_per_stream, n_tiles, final,
                         x_ref, convp_ref, hp_ref, poolp_ref, g1_ref, win_ref, cw_ref, cb_ref, wri_ref,
                         br_ref, bi_ref, lam_ref, pw_ref, ps_ref, wa_ref, wb_ref, wo_ref,
                         g2_ref, wgu_ref, wd_ref, fg_ref,
                         y_ref, convn_ref, hn_ref, pooln_ref,
                         xn_s, hist, gbuf, cbuf, cbf, ya_s, d_s, yb_s, mb_s, m_s, hcar, ymid, xn2_s, h_s):
    j = pl.program_id(0)
    new = j % 2
    cur = 1 - new
    n_hist = GATE_A_CH
    nc = _norm_chunk(tm)

    @pl.when(j == 0)
    def _():
        hist[...] = jnp.zeros(hist.shape, f32)
        hcar[...] = jnp.zeros(hcar.shape, f32)

    for c in range(POOL_CH):
        convn_ref[0, :, _chunk(c)] = hist[CONV_CH + c, HDR - (CONV_W - 1):HDR, :]
        pooln_ref[0, :, _chunk(c)] = hist[POOL_CH + c, HDR - (POOL_MAX - 1):HDR, :]
    hn_ref[0] = hcar[0:1, :]

    @pl.when(jnp.logical_and(j % tiles_per_stream == 0, j < n_tiles))
    def _():
        for c in range(POOL_CH):
            hist[CONV_CH + c] = jnp.zeros((HDR, NCH), f32)
            hist[POOL_CH + c] = jnp.zeros((HDR, NCH), f32)
            hist[CONV_CH + c, HDR - (CONV_W - 1):HDR, :] = convp_ref[0, :, _chunk(c)]
            hist[POOL_CH + c, HDR - (POOL_MAX - 1):HDR, :] = poolp_ref[0, :, _chunk(c)]
        hcar[0:1, :] = hp_ref[0]

    lam_c = _lam_scaled(lam_ref)

    def ffn_norm():
        for r0 in range(0, tm, nc):
            xn2_s[r0:r0 + nc, :] = _rmsnorm(ymid[cur, r0:r0 + nc, :], g2_ref[...]).astype(bf16)

    def ffn_up(c):
        gate = _dot(xn2_s[...], wgu_ref[:, _chunk(c)])
        up = _dot(xn2_s[...], wgu_ref[:, D_FF + c * NCH:D_FF + (c + 1) * NCH])
        h_s[:, _chunk(c)] = (jax.nn.silu(gate) * up).astype(bf16)

    def ffn_down(c):
        y_ref[:, _chunk(c)] = ymid[cur, :, _chunk(c)] + _dot(h_s[...], wd_ref[:, _chunk(c)])

    def ffn_final_norm():
        for r0 in range(0, tm, nc):
            y_ref[r0:r0 + nc, :] = _rmsnorm(y_ref[r0:r0 + nc, :], fg_ref[...])

    def mixer_norm():
        for r0 in range(0, tm, nc):
            xn_s[r0:r0 + nc, :] = _rmsnorm(x_ref[r0:r0 + nc, :], g1_ref[...]).astype(bf16)

    def project_with_history(c):
        p = _dot(xn_s[...], win_ref[:, _chunk(c)])
        ext = jnp.concatenate([hist[c], p], axis=0)
        hist[c] = p[tm - HDR:]
        return ext

    def conv_chunk(c):
        acc = _conv_taps(project_with_history(CONV_CH + c), cw_ref, cb_ref, _chunk(c))
        cbuf[:, _chunk(c)] = acc
        cbf[:, _chunk(c)] = acc.astype(bf16)

    def pool_chunk(g):
        d_s[:, _chunk(g)] = _pool_diff(project_with_history(POOL_CH + g), POOL_WINDOWS[g]).astype(bf16)

    def gate_chunk(c):
        gbuf[c - n_hist] = _dot(xn_s[...], win_ref[:, _chunk(c)])

    def lru_head(h):
        hs = slice(h * LRU_BLOCK, (h + 1) * LRU_BLOCK)
        ri = _dot(cbf[:, hs], wri_ref[h])
        a, b = _lru_coeffs(ri[:, :LRU_BLOCK] + br_ref[:, hs], ri[:, LRU_BLOCK:] + bi_ref[:, hs],
                           cbuf[:, hs], lam_c[:, hs])
        states, last = _scan_rows(a, b, hcar[0:1, hs])
        ya_s[:, hs] = states.astype(bf16)
        hcar[0:1, hs] = last

    def project_out(c):
        ymid[new, :, _chunk(c)] = x_ref[:, _chunk(c)] + _dot(m_s[...], wo_ref[:, _chunk(c)])

    n_ff = D_FF // NCH
    filler = [functools.partial(ffn_up, c) for c in range(n_ff)]
    filler += [functools.partial(ffn_down, c) for c in range(N_OUT_CH)]
    if final:
        filler.append(ffn_final_norm)
    others = [functools.partial(pool_chunk, g) for g in range(len(POOL_WINDOWS))]
    others += [functools.partial(gate_chunk, GATE_B_CH + c) for c in range(N_OUT_CH)]
    others += [functools.partial(_pool_project, g, d_s, yb_s, pw_ref, ps_ref) for g in range(len(POOL_WINDOWS))]
    others += [lambda c=c: _branch_b(c, gbuf[GATE_B_CH - n_hist + c], yb_s, wb_ref, mb_s) for c in range(N_OUT_CH)]
    others += [functools.partial(gate_chunk, GATE_A_CH + c) for c in range(N_OUT_CH)]

    stages = [functools.partial(conv_chunk, c) for c in range(POOL_CH)]
    stages += [functools.partial(lru_head, h) for h in range(LRU_HEADS)]
    first_head = POOL_CH

    def mixer_and_ffn():
        ffn_norm()
        mixer_norm()
        for k, stage in enumerate(stages):
            stage()
            lo, hi = (k * len(filler)) // len(stages), ((k + 1) * len(filler)) // len(stages)
            for fn in filler[lo:hi]:
                fn()
            if k >= first_head:
                h = k - first_head
                lo, hi = (h * len(others)) // LRU_HEADS, ((h + 1) * len(others)) // LRU_HEADS
                for fn in others[lo:hi]:
                    fn()
        for c in range(N_OUT_CH):
            _merge(c, gbuf[GATE_A_CH - n_hist + c], ya_s, wa_ref, mb_s, m_s)
        for c in range(N_OUT_CH):
            project_out(c)

    def mixer_only():
        mixer_norm()
        for k, stage in enumerate(stages):
            stage()
            if k >= first_head:
                h = k - first_head
                lo, hi = (h * len(others)) // LRU_HEADS, ((h + 1) * len(others)) // LRU_HEADS
                for fn in others[lo:hi]:
                    fn()
        for c in range(N_OUT_CH):
            _merge(c, gbuf[GATE_A_CH - n_hist + c], ya_s, wa_ref, mb_s, m_s)
        for c in range(N_OUT_CH):
            project_out(c)

    def ffn_only():
        ffn_norm()
        for fn in filler:
            fn()

    pl.when(j == 0)(mixer_only)
    pl.when(jnp.logical_and(j > 0, j < n_tiles))(mixer_and_ffn)
    pl.when(j == n_tiles)(ffn_only)


def _ffn_kernel(final, tm, x_ref, g2_ref, wgu_ref, wd_ref, fg_ref, y_ref, xn_s, h_s):
    nc = _norm_chunk(tm)
    _norm_to_bf16(x_ref, g2_ref, xn_s, tm)

    for c in range(D_FF // NCH):
        gate = _dot(xn_s[...], wgu_ref[:, c * NCH:(c + 1) * NCH])
        up = _dot(xn_s[...], wgu_ref[:, D_FF + c * NCH:D_FF + (c + 1) * NCH])
        h_s[:, c * NCH:(c + 1) * NCH] = (jax.nn.silu(gate) * up).astype(bf16)

    for c in range(N_OUT_CH):
        y_ref[:, _chunk(c)] = x_ref[:, _chunk(c)] + _dot(h_s[...], wd_ref[:, _chunk(c)])

    if final:
        def final_rows(r0):
            rows = pl.ds(r0, nc)
            y_ref[rows, :] = _rmsnorm(y_ref[rows, :], fg_ref[...])
        _for_rows(tm, nc, final_rows)


def _layer_spec(w, layer):
    nd = w.ndim - 1
    return pl.BlockSpec((None,) + w.shape[1:], lambda *_: (layer,) + (0,) * nd, pipeline_mode=pl.Buffered(1))


def _mixer_weights(w):
    return (w['norm1_g'], w['w_in'], w['conv_w'], w['conv_b'], w['wri'], w['b_r'], w['b_i'],
            w['lam'], w['pool_w'], w['pool_scale'], w['w_br_a'], w['w_br_b'], w['w_out'])


def _state_shapes(n):
    return (jax.ShapeDtypeStruct((n, CONV_W - 1, D_LRU), f32),
            jax.ShapeDtypeStruct((n, 1, D_LRU), f32),
            jax.ShapeDtypeStruct((n, POOL_MAX - 1, D_POOL), f32))


def _state_specs(n, index_map):
    return [pl.BlockSpec((n, CONV_W - 1, D_LRU), index_map), pl.BlockSpec((n, 1, D_LRU), index_map),
            pl.BlockSpec((n, POOL_MAX - 1, D_POOL), index_map)]


def _mixer_step_call(x, state_conv, state_lru, state_pool, w, layer, segs, name):
    tm = x.shape[0]
    ns = state_conv.shape[1]
    max_len = max(seg[1] for seg in segs)
    weights = _mixer_weights(w)
    x_spec = pl.BlockSpec((tm, D_MODEL), lambda i: (0, 0))
    whole = lambda i: (0, 0, 0)
    state_in = [pl.BlockSpec((None,) + s.shape[1:], lambda i: (layer, 0, 0, 0))
                for s in (state_conv, state_lru, state_pool)]
    scratch = [
        pltpu.VMEM((tm, D_MODEL), bf16),
        pltpu.VMEM((len(segs), GATE_A_CH, HDR + max_len, NCH), f32),
        pltpu.VMEM((N_IN_CH - GATE_A_CH, tm, NCH), f32),
        pltpu.VMEM((tm, D_LRU), f32),
        pltpu.VMEM((tm, D_LRU), bf16),
        pltpu.VMEM((tm, D_LRU), f32),
        pltpu.VMEM((tm, D_LRU), f32),
        pltpu.VMEM((tm, D_LRU), bf16),
        pltpu.VMEM((tm, D_POOL), bf16),
        pltpu.VMEM((tm, D_POOL), bf16),
        pltpu.VMEM((tm, D_MODEL), f32),
        pltpu.VMEM((tm, D_MODEL), bf16),
    ]
    return pl.pallas_call(
        functools.partial(_mixer_step_kernel, tuple(segs), tm),
        grid=(1,),
        in_specs=[x_spec] + state_in + [_layer_spec(v, layer) for v in weights],
        out_specs=[x_spec] + _state_specs(ns, whole) + _state_specs(1, whole),
        out_shape=(jax.ShapeDtypeStruct(x.shape, f32),) + _state_shapes(ns) + _state_shapes(1),
        scratch_shapes=scratch,
        compiler_params=pltpu.CompilerParams(dimension_semantics=("arbitrary",), vmem_limit_bytes=VMEM_LIMIT),
        name=name,
    )(x, state_conv, state_lru, state_pool, *weights)


def _mixer_stream_call(x, conv_init, h_init, pool_init, w, layer, tm, name):
    nb, t, _ = x.shape
    nt = t // tm
    assert nt * tm == t
    weights = _mixer_weights(w)
    x_spec = pl.BlockSpec((None, tm, D_MODEL), lambda b, i: (b, i, 0))
    scratch = [
        pltpu.VMEM((tm, D_MODEL), bf16),
        pltpu.VMEM((GATE_A_CH, HDR, NCH), f32),
        pltpu.VMEM((N_IN_CH - GATE_A_CH, tm, NCH), f32),
        pltpu.VMEM((tm, D_LRU), f32),
        pltpu.VMEM((tm, D_LRU), bf16),
        pltpu.VMEM((tm, D_LRU), bf16),
        pltpu.VMEM((tm, D_POOL), bf16),
        pltpu.VMEM((tm, D_POOL), bf16),
        pltpu.VMEM((tm, D_MODEL), f32),
        pltpu.VMEM((tm, D_MODEL), bf16),
        pltpu.VMEM((SUBLANES, D_LRU), f32),
    ]
    return pl.pallas_call(
        functools.partial(_mixer_stream_kernel, tm),
        grid=(nb, nt),
        in_specs=[x_spec] + _state_specs(1, lambda b, i: (0, 0, 0)) + [_layer_spec(v, layer) for v in weights],
        out_specs=[x_spec] + _state_specs(1, lambda b, i: (b, 0, 0)),
        out_shape=(jax.ShapeDtypeStruct(x.shape, f32),) + _state_shapes(nb),
        scratch_shapes=scratch,
        compiler_params=pltpu.CompilerParams(
            dimension_semantics=("arbitrary", "arbitrary"), vmem_limit_bytes=VMEM_LIMIT),
        name=name,
    )(x, conv_init, h_init, pool_init, *weights)


def _layer_stream_call(x, conv_init, h_init, pool_init, w, layer, final, tm, name):
    nb, t, _ = x.shape
    tps = t // tm
    n_tiles = nb * tps
    assert tps * tm == t
    weights = _mixer_weights(w) + (w['norm2_g'], w['w_gu'], w['w_down'])
    x_tiles = x.reshape(n_tiles, tm, D_MODEL)
    x_spec = pl.BlockSpec((None, tm, D_MODEL), lambda j: (jnp.minimum(j, n_tiles - 1), 0, 0))
    y_spec = pl.BlockSpec((None, tm, D_MODEL), lambda j: (jnp.maximum(j - 1, 0), 0, 0))
    fg_spec = pl.BlockSpec(w['final_g'].shape, lambda j: (0, 0))
    scratch = [
        pltpu.VMEM((tm, D_MODEL), bf16),
        pltpu.VMEM((GATE_A_CH, HDR, NCH), f32),
        pltpu.VMEM((N_IN_CH - GATE_A_CH, tm, NCH), f32),
        pltpu.VMEM((tm, D_LRU), f32),
        pltpu.VMEM((tm, D_LRU), bf16),
        pltpu.VMEM((tm, D_LRU), bf16),
        pltpu.VMEM((tm, D_POOL), bf16),
        pltpu.VMEM((tm, D_POOL), bf16),
        pltpu.VMEM((tm, D_MODEL), f32),
        pltpu.VMEM((tm, D_MODEL), bf16),
        pltpu.VMEM((SUBLANES, D_LRU), f32),
        pltpu.VMEM((2, tm, D_MODEL), f32),
        pltpu.VMEM((tm, D_MODEL), bf16),
        pltpu.VMEM((tm, D_FF), bf16),
    ]
    y, c_n, h_n, p_n = pl.pallas_call(
        functools.partial(_layer_stream_kernel, tm, tps, n_tiles, final),
        grid=(n_tiles + 1,),
        in_specs=([x_spec] + _state_specs(1, lambda j: (0, 0, 0)) + [_layer_spec(v, layer) for v in weights]
                  + [fg_spec]),
        out_specs=[y_spec] + _state_specs(1, lambda j: (jnp.maximum(j - 1, 0) // tps, 0, 0)),
        out_shape=(jax.ShapeDtypeStruct(x_tiles.shape, f32),) + _state_shapes(nb),
        scratch_shapes=scratch,
        compiler_params=pltpu.CompilerParams(dimension_semantics=("arbitrary",), vmem_limit_bytes=VMEM_LIMIT),
        name=name,
    )(x_tiles, conv_init, h_init, pool_init, *weights, w['final_g'])
    return y.reshape(x.shape), c_n, h_n, p_n


def _ffn_call(x, w, layer, final, tm, name):
    nb, t, _ = x.shape
    nt = t // tm
    assert nt * tm == t
    x_spec = pl.BlockSpec((None, tm, D_MODEL), lambda b, i: (b, i, 0))
    weights = (w['norm2_g'], w['w_gu'], w['w_down'])
    fg_spec = pl.BlockSpec(w['final_g'].shape, lambda b, i: (0, 0))
    return pl.pallas_call(
        functools.partial(_ffn_kernel, final, tm),
        grid=(nb, nt),
        in_specs=[x_spec] + [_layer_spec(v, layer) for v in weights] + [fg_spec],
        out_specs=x_spec,
        out_shape=jax.ShapeDtypeStruct(x.shape, f32),
        scratch_shapes=[pltpu.VMEM((tm, D_MODEL), bf16), pltpu.VMEM((tm, D_FF), bf16)],
        compiler_params=pltpu.CompilerParams(
            dimension_semantics=("arbitrary", "arbitrary"), vmem_limit_bytes=VMEM_LIMIT),
        name=name,
    )(x, *weights, w['final_g'])


def kernel(x_prompt, x_sample, state_conv, state_lru, state_pool, meta_tokens, norm1_g, w_in, conv_w, conv_b,
           lam, w_r, b_r, w_i, b_i, pool_w, pool_scale, w_br_a, w_br_b, w_out, norm2_g, w_gu, w_down, final_g):
    depth = w_in.shape[0]
    ns, ts, _ = x_sample.shape
    rows = lambda v: v.reshape(depth, 1, -1).astype(f32)
    w = {
        'norm1_g': rows(norm1_g), 'w_in': w_in.astype(bf16), 'conv_w': conv_w, 'conv_b': rows(conv_b),
        'wri': jnp.concatenate([w_r, w_i], axis=-1).astype(bf16),
        'b_r': rows(b_r), 'b_i': rows(b_i), 'lam': rows(lam),
        'pool_w': pool_w.astype(bf16), 'pool_scale': rows(pool_scale),
        'w_br_a': w_br_a.astype(bf16), 'w_br_b': w_br_b.astype(bf16), 'w_out': w_out.astype(bf16),
        'norm2_g': rows(norm2_g), 'w_gu': w_gu.astype(bf16), 'w_down': w_down.astype(bf16),
        'final_g': final_g.reshape(1, -1).astype(f32),
    }

    segs_s = [(s * ts, ts, PAST_LEN, True) for s in range(ns)] + [(ns * ts, N_META, 0, False)]
    tm_s = ns * ts + N_META
    xs = jnp.concatenate([x_sample.reshape(ns * ts, D_MODEL), meta_tokens.astype(f32)], axis=0)
    xp = x_prompt
    state_lru = state_lru[:, :, None, :]

    outs = {k: [] for k in ('cp', 'hp', 'pp', 'cs', 'hs', 'ps')}
    for l in range(depth):
        last = l == depth - 1
        xs, c_s, h_s, p_s, c_m, h_m, p_m = _mixer_step_call(xs, state_conv, state_lru, state_pool, w, l, segs_s,
                                                            f"mixer_step_l{l}")
        xs = _ffn_call(xs[None], w, l, last, tm_s, f"ffn_step_l{l}")[0]
        xp, c_p, h_p, p_p = _mixer_stream_call(xp, c_m, h_m, p_m, w, l, TM_LAYER, f"mixer_prompt_l{l}")
        xp = _ffn_call(xp, w, l, last, TM_PROMPT, f"ffn_prompt_l{l}")

        outs['cp'].append(c_p); outs['hp'].append(h_p[:, 0]); outs['pp'].append(p_p)
        outs['cs'].append(c_s); outs['hs'].append(h_s[:, 0]); outs['ps'].append(p_s)

    y_sample = xs[:ns * ts].reshape(ns, ts, D_MODEL)
    return (xp, y_sample, jnp.stack(outs['cp']), jnp.stack(outs['hp']), jnp.stack(outs['pp']),
            jnp.stack(outs['cs']), jnp.stack(outs['hs']), jnp.stack(outs['ps']))
```

```python
import functools

import jax
import jax.numpy as jnp
from jax import lax
from jax.experimental import pallas as pl
from jax.experimental.pallas import tpu as pltpu

D_MODEL = 1024
D_LRU = 1024
D_POOL = 1024
LRU_HEADS = 8
LRU_BLOCK = D_LRU // LRU_HEADS
LRU_C = 8.0
CONV_W = 4
POOL_WINDOWS = (2, 4, 8, 16)
POOL_GROUP = D_POOL // len(POOL_WINDOWS)
POOL_MAX = 16
D_FF = 2816
N_META = 16
PAST_LEN = 2048
EPS = 1e-6

SUBLANES = 8
BF16_ROWS = 16
HDR = 16
NCH = 256
N_IN_CH = 4 * D_MODEL // NCH
CONV_CH = 0
POOL_CH = D_LRU // NCH
GATE_A_CH = (D_LRU + D_POOL) // NCH
GATE_B_CH = (D_LRU + D_POOL + D_MODEL) // NCH
N_OUT_CH = D_MODEL // NCH
N_FF_CH = D_FF // NCH
TM_LAYER = 256
ROW_CH = 32
VMEM_LIMIT = 60 * 1024 * 1024

V_NORM1, V_CONV_B, V_BR, V_BI, V_LAM, V_POOL_SCALE, V_NORM2, V_FINAL, V_CONV_W = range(9)
VEC_ROWS = 16
SQ_BRANCH_A, SQ_BRANCH_B, SQ_OUT = range(3)

f32 = jnp.float32
bf16 = jnp.bfloat16

assert POOL_GROUP == NCH and HDR >= POOL_MAX - 1 and HDR % BF16_ROWS == 0
assert V_CONV_W + CONV_W <= VEC_ROWS and D_LRU == D_POOL == D_MODEL


def _dot(a, b):
    return jnp.dot(a, b, preferred_element_type=f32)


def _chunk(c):
    return slice(c * NCH, (c + 1) * NCH)


def _vec(v_ref, row, cols=slice(None)):
    return v_ref[row:row + 1, cols]


def _rmsnorm(x, g):
    y = x * lax.rsqrt(jnp.mean(x * x, axis=-1, keepdims=True) + EPS)
    return y * g


def _norm_chunk(tm):
    return max(c for c in range(BF16_ROWS, 129, BF16_ROWS) if tm % c == 0)


def _sigmoid(x):
    return 0.5 * jnp.tanh(0.5 * x) + 0.5


def _shift_rows(x, k):
    return pltpu.roll(x, k, axis=0)


def _for_rows(total, chunk, fn):
    n = total // chunk
    assert n * chunk == total
    if n == 1:
        fn(0)
    else:
        def body(k, c):
            fn(pl.multiple_of(k * chunk, chunk))
            return c
        lax.fori_loop(0, n, body, 0)


def _norm_to_bf16(x_ref, g, xn_s, tm):
    nc = _norm_chunk(tm)

    def norm_rows(r0):
        rows = pl.ds(r0, nc)
        xn_s[rows, :] = _rmsnorm(x_ref[rows, :], g).astype(bf16)
    _for_rows(tm, nc, norm_rows)


def _conv_taps(blk, v_ref, cs):
    tap = lambda k: _shift_rows(blk, CONV_W - 1 - k)[HDR:] if k < CONV_W - 1 else blk[HDR:]
    acc = _vec(v_ref, V_CONV_B, cs) + _vec(v_ref, V_CONV_W, cs) * tap(0)
    for k in range(1, CONV_W):
        acc = acc + _vec(v_ref, V_CONV_W + k, cs) * tap(k)
    return acc


def _lam_scaled(v_ref):
    lam = _vec(v_ref, V_LAM)
    log_sig_lam = jnp.minimum(lam, 0.0) - jnp.log1p(jnp.exp(-jnp.abs(lam)))
    return LRU_C * log_sig_lam


def _lru_coeffs(r_pre, i_pre, c, lam_c, start_mask=None):
    r = _sigmoid(r_pre)
    i = _sigmoid(i_pre)
    log_a = r * lam_c
    a = jnp.exp(log_a)
    th = jnp.tanh(log_a)
    mult = jnp.sqrt(-2.0 * th / (1.0 - th))
    if start_mask is not None:
        mult = jnp.where(start_mask, 1.0, mult)
    return a, mult * (i * c)


def _scan_block(a, b, h):
    sub = lax.broadcasted_iota(jnp.int32, a.shape, 0)
    for d in (1, 2, 4):
        keep = sub >= d
        a_prev = pltpu.roll(a, d, axis=0)
        b_prev = pltpu.roll(b, d, axis=0)
        b = jnp.where(keep, a * b_prev, 0.0) + b
        a = jnp.where(keep, a * a_prev, a)
    return a * h + b


def _scan_rows(a, b, h):
    outs = []
    for k in range(a.shape[0] // SUBLANES):
        blk = slice(k * SUBLANES, (k + 1) * SUBLANES)
        outs.append(_scan_block(a[blk], b[blk], h))
        h = outs[-1][SUBLANES - 1:SUBLANES, :]
    return jnp.concatenate(outs, axis=0), h


def _pool_diff(blk, w, cnt=None):
    xb = blk[HDR:]
    tot, span = blk, 1
    while span < w:
        tot = tot + _shift_rows(tot, span)
        span *= 2
    tot = tot[HDR:]
    pooled = tot * (1.0 / w) if cnt is None else tot / cnt
    return pooled - xb


def _lru_gate_preact(h, cbf, wri_ref, v_ref):
    hs = slice(h * LRU_BLOCK, (h + 1) * LRU_BLOCK)
    ri = _dot(cbf[:, hs], wri_ref[h])
    return ri[:, :LRU_BLOCK] + _vec(v_ref, V_BR, hs), ri[:, LRU_BLOCK:] + _vec(v_ref, V_BI, hs)


def _pool_project(g, d_s, yb_s, pw_ref, v_ref):
    gs = _chunk(g)
    yb_s[:, gs] = (_dot(d_s[:, gs], pw_ref[g]) * _vec(v_ref, V_POOL_SCALE, gs)).astype(bf16)


def _branch_b(c, gate_b, yb_s, wsq_ref, mb_s):
    mb_s[:, _chunk(c)] = _sigmoid(gate_b) * _dot(yb_s[...], wsq_ref[SQ_BRANCH_B, :, _chunk(c)])


def _merge(c, gate_a, ya_s, wsq_ref, mb_s, m_s):
    cs = _chunk(c)
    m_s[:, cs] = (_sigmoid(gate_a) * _dot(ya_s[...], wsq_ref[SQ_BRANCH_A, :, cs]) + mb_s[:, cs]).astype(bf16)


def _mixer_step_kernel(segs, tm, x_ref, convp_ref, hp_ref, poolp_ref, v_ref, win_ref, wri_ref, pw_ref, wsq_ref,
                       y_ref, convn_ref, hn_ref, pooln_ref, convm_ref, hm_ref, poolm_ref,
                       xn_s, proj, gbuf, cbuf, cbf, a_s, b_s, ya_s, d_s, yb_s, mb_s, m_s):
    n_hist = GATE_A_CH
    assert sum(1 for seg in segs if not seg[3]) == 1
    for s, (_, _, _, has_state) in enumerate(segs):
        for c in range(n_hist):
            proj[s, c, 0:HDR, :] = jnp.zeros((HDR, NCH), f32)
        if has_state:
            for c in range(POOL_CH):
                proj[s, CONV_CH + c, HDR - (CONV_W - 1):HDR, :] = convp_ref[s, :, _chunk(c)]
                proj[s, POOL_CH + c, HDR - (POOL_MAX - 1):HDR, :] = poolp_ref[s, :, _chunk(c)]

    _norm_to_bf16(x_ref, _vec(v_ref, V_NORM1), xn_s, tm)

    for c in range(N_IN_CH):
        p = _dot(xn_s[...], win_ref[:, _chunk(c)])
        if c < n_hist:
            for s, (row0, length, _, _) in enumerate(segs):
                proj[s, c, HDR:HDR + length, :] = p[row0:row0 + length]
        else:
            gbuf[c - n_hist] = p

    for s, (row0, length, _, has_state) in enumerate(segs):
        n = min(ROW_CH, length)

        def conv_rows(t0, s=s, row0=row0, n=n):
            rows = pl.ds(row0 + t0, n)
            for c in range(POOL_CH):
                acc = _conv_taps(proj[s, CONV_CH + c, pl.ds(t0, n + HDR), :], v_ref, _chunk(c))
                cbuf[rows, _chunk(c)] = acc
                cbf[rows, _chunk(c)] = acc.astype(bf16)
        _for_rows(length, n, conv_rows)
        conv_out, pool_out, k = (convn_ref, pooln_ref, s) if has_state else (convm_ref, poolm_ref, 0)
        for c in range(POOL_CH):
            conv_out[k, :, _chunk(c)] = proj[s, CONV_CH + c, HDR + length - (CONV_W - 1):HDR + length, :]
            pool_out[k, :, _chunk(c)] = proj[s, POOL_CH + c, HDR + length - (POOL_MAX - 1):HDR + length, :]

    for h in range(LRU_HEADS):
        hs = slice(h * LRU_BLOCK, (h + 1) * LRU_BLOCK)
        a_s[:, hs], b_s[:, hs] = _lru_gate_preact(h, cbf, wri_ref, v_ref)
    lam_c = _lam_scaled(v_ref)

    for s, (row0, length, pos0, has_state) in enumerate(segs):
        n = min(ROW_CH, length)

        def mix_rows(t0, h, s=s, row0=row0, n=n, pos0=pos0):
            rows = pl.ds(row0 + t0, n)
            pos = pos0 + t0 + lax.broadcasted_iota(jnp.int32, (n, 1), 0)
            start = (pos == 0) if pos0 == 0 else None
            a, b = _lru_coeffs(a_s[rows, :], b_s[rows, :], cbuf[rows, :], lam_c, start)
            states, h = _scan_rows(a, b, h)
            ya_s[rows, :] = states.astype(bf16)
            cnt = None if pos0 >= POOL_MAX - 1 else pos + 1
            for g, w in enumerate(POOL_WINDOWS):
                cnt_g = None if cnt is None else jnp.minimum(w, cnt).astype(f32)
                d_s[rows, _chunk(g)] = _pool_diff(proj[s, POOL_CH + g, pl.ds(t0, n + HDR), :], w, cnt_g).astype(bf16)
            return h

        steps = length // n
        h = hp_ref[s] if has_state else jnp.zeros((1, D_LRU), f32)
        if steps == 1:
            h = mix_rows(0, h)
        else:
            h = lax.fori_loop(0, steps, lambda k, h: mix_rows(pl.multiple_of(k * n, n), h), h)
        if has_state:
            hn_ref[s] = h
        else:
            hm_ref[0] = h

    for g in range(len(POOL_WINDOWS)):
        _pool_project(g, d_s, yb_s, pw_ref, v_ref)
    for c in range(N_OUT_CH):
        _branch_b(c, gbuf[GATE_B_CH - n_hist + c], yb_s, wsq_ref, mb_s)
        _merge(c, gbuf[GATE_A_CH - n_hist + c], ya_s, wsq_ref, mb_s, m_s)
    for c in range(N_OUT_CH):
        y_ref[:, _chunk(c)] = x_ref[:, _chunk(c)] + _dot(m_s[...], wsq_ref[SQ_OUT, :, _chunk(c)])


def _layer_stream_kernel(tm, tiles_per_stream, n_tiles, final,
                         x_ref, convp_ref, hp_ref, poolp_ref, v_ref, win_ref, wri_ref, pw_ref, wsq_ref,
                         wgu_ref, wd_ref,
                         y_ref, convn_ref, hn_ref, pooln_ref,
                         xn_s, hist, gbuf, cbuf, cbf, ya_s, d_s, yb_s, mb_s, m_s, hcar, ymid, xn2_s, h_s):
    j = pl.program_id(0)
    new = j % 2
    cur = 1 - new
    n_hist = GATE_A_CH
    nc = _norm_chunk(tm)

    @pl.when(j == 0)
    def _():
        hist[...] = jnp.zeros(hist.shape, f32)
        hcar[...] = jnp.zeros(hcar.shape, f32)
        ymid[1] = jnp.zeros((tm, D_MODEL), f32)

    for c in range(POOL_CH):
        convn_ref[0, :, _chunk(c)] = hist[CONV_CH + c, HDR - (CONV_W - 1):HDR, :]
        pooln_ref[0, :, _chunk(c)] = hist[POOL_CH + c, HDR - (POOL_MAX - 1):HDR, :]
    hn_ref[0] = hcar[0:1, :]

    @pl.when(jnp.logical_and(j % tiles_per_stream == 0, j < n_tiles))
    def _():
        for c in range(POOL_CH):
            hist[CONV_CH + c] = jnp.zeros((HDR, NCH), f32)
            hist[POOL_CH + c] = jnp.zeros((HDR, NCH), f32)
            hist[CONV_CH + c, HDR - (CONV_W - 1):HDR, :] = convp_ref[0, :, _chunk(c)]
            hist[POOL_CH + c, HDR - (POOL_MAX - 1):HDR, :] = poolp_ref[0, :, _chunk(c)]
        hcar[0:1, :] = hp_ref[0]

    lam_c = _lam_scaled(v_ref)

    def ffn_norm():
        for r0 in range(0, tm, nc):
            xn2_s[r0:r0 + nc, :] = _rmsnorm(ymid[cur, r0:r0 + nc, :], _vec(v_ref, V_NORM2)).astype(bf16)

    def ffn_up(c):
        gate = _dot(xn2_s[...], wgu_ref[:, _chunk(c)])
        up = _dot(xn2_s[...], wgu_ref[:, D_FF + c * NCH:D_FF + (c + 1) * NCH])
        h_s[:, _chunk(c)] = (jax.nn.silu(gate) * up).astype(bf16)

    def ffn_down(c):
        y_ref[:, _chunk(c)] = ymid[cur, :, _chunk(c)] + _dot(h_s[...], wd_ref[:, _chunk(c)])

    def ffn_final_norm():
        for r0 in range(0, tm, nc):
            y_ref[r0:r0 + nc, :] = _rmsnorm(y_ref[r0:r0 + nc, :], _vec(v_ref, V_FINAL))

    def mixer_norm():
        for r0 in range(0, tm, nc):
            xn_s[r0:r0 + nc, :] = _rmsnorm(x_ref[r0:r0 + nc, :], _vec(v_ref, V_NORM1)).astype(bf16)

    def project_with_history(c):
        p = _dot(xn_s[...], win_ref[:, _chunk(c)])
        ext = jnp.concatenate([hist[c], p], axis=0)
        hist[c] = p[tm - HDR:]
        return ext

    def conv_chunk(c):
        acc = _conv_taps(project_with_history(CONV_CH + c), v_ref, _chunk(c))
        cbuf[:, _chunk(c)] = acc
        cbf[:, _chunk(c)] = acc.astype(bf16)

    def pool_chunk(g):
        d_s[:, _chunk(g)] = _pool_diff(project_with_history(POOL_CH + g), POOL_WINDOWS[g]).astype(bf16)

    def gate_chunk(c):
        gbuf[c - n_hist] = _dot(xn_s[...], win_ref[:, _chunk(c)])

    def lru_head(h):
        hs = slice(h * LRU_BLOCK, (h + 1) * LRU_BLOCK)
        r_pre, i_pre = _lru_gate_preact(h, cbf, wri_ref, v_ref)
        a, b = _lru_coeffs(r_pre, i_pre, cbuf[:, hs], lam_c[:, hs])
        states, last = _scan_rows(a, b, hcar[0:1, hs])
        ya_s[:, hs] = states.astype(bf16)
        hcar[0:1, hs] = last

    def project_out(c):
        ymid[new, :, _chunk(c)] = x_ref[:, _chunk(c)] + _dot(m_s[...], wsq_ref[SQ_OUT, :, _chunk(c)])

    filler = [functools.partial(ffn_up, c) for c in range(N_FF_CH)]
    filler += [functools.partial(ffn_down, c) for c in range(N_OUT_CH)]
    if final:
        filler.append(ffn_final_norm)
    others = [functools.partial(pool_chunk, g) for g in range(len(POOL_WINDOWS))]
    others += [functools.partial(gate_chunk, GATE_B_CH + c) for c in range(N_OUT_CH)]
    others += [functools.partial(_pool_project, g, d_s, yb_s, pw_ref, v_ref) for g in range(len(POOL_WINDOWS))]
    others += [lambda c=c: _branch_b(c, gbuf[GATE_B_CH - n_hist + c], yb_s, wsq_ref, mb_s) for c in range(N_OUT_CH)]
    others += [functools.partial(gate_chunk, GATE_A_CH + c) for c in range(N_OUT_CH)]

    stages = [functools.partial(conv_chunk, c) for c in range(POOL_CH)]
    stages += [functools.partial(lru_head, h) for h in range(LRU_HEADS)]
    first_head = POOL_CH

    ffn_norm()
    mixer_norm()
    for k, stage in enumerate(stages):
        stage()
        lo, hi = (k * len(filler)) // len(stages), ((k + 1) * len(filler)) // len(stages)
        for fn in filler[lo:hi]:
            fn()
        if k >= first_head:
            h = k - first_head
            lo, hi = (h * len(others)) // LRU_HEADS, ((h + 1) * len(others)) // LRU_HEADS
            for fn in others[lo:hi]:
                fn()
    for c in range(N_OUT_CH):
        _merge(c, gbuf[GATE_A_CH - n_hist + c], ya_s, wsq_ref, mb_s, m_s)
    for c in range(N_OUT_CH):
        project_out(c)


def _ffn_kernel(final, tm, x_ref, v_ref, wgu_ref, wd_ref, y_ref, xn_s, h_s):
    nc = _norm_chunk(tm)
    _norm_to_bf16(x_ref, _vec(v_ref, V_NORM2), xn_s, tm)

    for c in range(N_FF_CH):
        gate = _dot(xn_s[...], wgu_ref[:, _chunk(c)])
        up = _dot(xn_s[...], wgu_ref[:, D_FF + c * NCH:D_FF + (c + 1) * NCH])
        h_s[:, _chunk(c)] = (jax.nn.silu(gate) * up).astype(bf16)

    for c in range(N_OUT_CH):
        y_ref[:, _chunk(c)] = x_ref[:, _chunk(c)] + _dot(h_s[...], wd_ref[:, _chunk(c)])

    if final:
        def final_rows(r0):
            rows = pl.ds(r0, nc)
            y_ref[rows, :] = _rmsnorm(y_ref[rows, :], _vec(v_ref, V_FINAL))
        _for_rows(tm, nc, final_rows)


def _layer_spec(w, layer):
    nd = w.ndim - 1
    return pl.BlockSpec((None,) + w.shape[1:], lambda *_: (layer,) + (0,) * nd, pipeline_mode=pl.Buffered(1))


def _mixer_weights(w):
    return (w['vec'], w['w_in'], w['wri'], w['pool_w'], w['wsq'])


def _state_shapes(n):
    return (jax.ShapeDtypeStruct((n, CONV_W - 1, D_LRU), f32),
            jax.ShapeDtypeStruct((n, 1, D_LRU), f32),
            jax.ShapeDtypeStruct((n, POOL_MAX - 1, D_POOL), f32))


def _state_specs(n, index_map):
    return [pl.BlockSpec((n, CONV_W - 1, D_LRU), index_map), pl.BlockSpec((n, 1, D_LRU), index_map),
            pl.BlockSpec((n, POOL_MAX - 1, D_POOL), index_map)]


def _mixer_step_call(x, state_conv, state_lru, state_pool, w, layer, segs, name):
    tm = x.shape[0]
    ns = state_conv.shape[1]
    max_len = max(seg[1] for seg in segs)
    weights = _mixer_weights(w)
    x_spec = pl.BlockSpec((tm, D_MODEL), lambda i: (0, 0))
    whole = lambda i: (0, 0, 0)
    state_in = [pl.BlockSpec((None,) + s.shape[1:], lambda i: (layer, 0, 0, 0))
                for s in (state_conv, state_lru, state_pool)]
    scratch = [
        pltpu.VMEM((tm, D_MODEL), bf16),
        pltpu.VMEM((len(segs), GATE_A_CH, HDR + max_len, NCH), f32),
        pltpu.VMEM((N_IN_CH - GATE_A_CH, tm, NCH), f32),
        pltpu.VMEM((tm, D_LRU), f32),
        pltpu.VMEM((tm, D_LRU), bf16),
        pltpu.VMEM((tm, D_LRU), f32),
        pltpu.VMEM((tm, D_LRU), f32),
        pltpu.VMEM((tm, D_LRU), bf16),
        pltpu.VMEM((tm, D_POOL), bf16),
        pltpu.VMEM((tm, D_POOL), bf16),
        pltpu.VMEM((tm, D_MODEL), f32),
        pltpu.VMEM((tm, D_MODEL), bf16),
    ]
    return pl.pallas_call(
        functools.partial(_mixer_step_kernel, tuple(segs), tm),
        grid=(1,),
        in_specs=[x_spec] + state_in + [_layer_spec(v, layer) for v in weights],
        out_specs=[x_spec] + _state_specs(ns, whole) + _state_specs(1, whole),
        out_shape=(jax.ShapeDtypeStruct(x.shape, f32),) + _state_shapes(ns) + _state_shapes(1),
        scratch_shapes=scratch,
        compiler_params=pltpu.CompilerParams(dimension_semantics=("arbitrary",), vmem_limit_bytes=VMEM_LIMIT),
        name=name,
    )(x, state_conv, state_lru, state_pool, *weights)


def _layer_stream_call(x, conv_init, h_init, pool_init, w, layer, final, tm, name):
    nb, t, _ = x.shape
    tps = t // tm
    n_tiles = nb * tps
    assert tps * tm == t
    weights = _mixer_weights(w) + (w['w_gu'], w['w_down'])
    x_tiles = x.reshape(n_tiles, tm, D_MODEL)
    x_spec = pl.BlockSpec((None, tm, D_MODEL), lambda j: (jnp.minimum(j, n_tiles - 1), 0, 0))
    y_spec = pl.BlockSpec((None, tm, D_MODEL), lambda j: (jnp.maximum(j - 1, 0), 0, 0))
    scratch = [
        pltpu.VMEM((tm, D_MODEL), bf16),
        pltpu.VMEM((GATE_A_CH, HDR, NCH), f32),
        pltpu.VMEM((N_IN_CH - GATE_A_CH, tm, NCH), f32),
        pltpu.VMEM((tm, D_LRU), f32),
        pltpu.VMEM((tm, D_LRU), bf16),
        pltpu.VMEM((tm, D_LRU), bf16),
        pltpu.VMEM((tm, D_POOL), bf16),
        pltpu.VMEM((tm, D_POOL), bf16),
        pltpu.VMEM((tm, D_MODEL), f32),
        pltpu.VMEM((tm, D_MODEL), bf16),
        pltpu.VMEM((SUBLANES, D_LRU), f32),
        pltpu.VMEM((2, tm, D_MODEL), f32),
        pltpu.VMEM((tm, D_MODEL), bf16),
        pltpu.VMEM((tm, D_FF), bf16),
    ]
    y, c_n, h_n, p_n = pl.pallas_call(
        functools.partial(_layer_stream_kernel, tm, tps, n_tiles, final),
        grid=(n_tiles + 1,),
        in_specs=[x_spec] + _state_specs(1, lambda j: (0, 0, 0)) + [_layer_spec(v, layer) for v in weights],
        out_specs=[y_spec] + _state_specs(1, lambda j: (jnp.maximum(j - 1, 0) // tps, 0, 0)),
        out_shape=(jax.ShapeDtypeStruct(x_tiles.shape, f32),) + _state_shapes(nb),
        scratch_shapes=scratch,
        compiler_params=pltpu.CompilerParams(dimension_semantics=("arbitrary",), vmem_limit_bytes=VMEM_LIMIT),
        name=name,
    )(x_tiles, conv_init, h_init, pool_init, *weights)
    return y.reshape(x.shape), c_n, h_n, p_n


def _ffn_call(x, w, layer, final, name):
    tm = x.shape[0]
    x_spec = pl.BlockSpec((tm, D_MODEL), lambda i: (0, 0))
    weights = (w['vec'], w['w_gu'], w['w_down'])
    return pl.pallas_call(
        functools.partial(_ffn_kernel, final, tm),
        grid=(1,),
        in_specs=[x_spec] + [_layer_spec(v, layer) for v in weights],
        out_specs=x_spec,
        out_shape=jax.ShapeDtypeStruct(x.shape, f32),
        scratch_shapes=[pltpu.VMEM((tm, D_MODEL), bf16), pltpu.VMEM((tm, D_FF), bf16)],
        compiler_params=pltpu.CompilerParams(dimension_semantics=("arbitrary",), vmem_limit_bytes=VMEM_LIMIT),
        name=name,
    )(x, *weights)


def kernel(x_prompt, x_sample, state_conv, state_lru, state_pool, meta_tokens, norm1_g, w_in, conv_w, conv_b,
           lam, w_r, b_r, w_i, b_i, pool_w, pool_scale, w_br_a, w_br_b, w_out, norm2_g, w_gu, w_down, final_g):
    depth = w_in.shape[0]
    ns, ts, _ = x_sample.shape
    vec_rows = [None] * V_CONV_W
    for row, v in ((V_NORM1, norm1_g), (V_CONV_B, conv_b), (V_BR, b_r), (V_BI, b_i), (V_LAM, lam),
                   (V_POOL_SCALE, pool_scale), (V_NORM2, norm2_g),
                   (V_FINAL, jnp.broadcast_to(final_g, (depth, D_MODEL)))):
        vec_rows[row] = v.astype(f32)[:, None, :]
    pad = jnp.zeros((depth, VEC_ROWS - V_CONV_W - CONV_W, D_MODEL), f32)
    w = {
        'vec': jnp.concatenate(vec_rows + [conv_w.astype(f32), pad], axis=1),
        'w_in': w_in.astype(bf16),
        'wri': jnp.concatenate([w_r, w_i], axis=-1).astype(bf16),
        'pool_w': pool_w.astype(bf16),
        'wsq': jnp.stack([w_br_a, w_br_b, w_out], axis=1).astype(bf16),
        'w_gu': w_gu.astype(bf16), 'w_down': w_down.astype(bf16),
    }

    segs_s = [(s * ts, ts, PAST_LEN, True) for s in range(ns)] + [(ns * ts, N_META, 0, False)]
    xs = jnp.concatenate([x_sample.reshape(ns * ts, D_MODEL), meta_tokens.astype(f32)], axis=0)
    xp = x_prompt
    state_lru = state_lru[:, :, None, :]

    outs = {k: [] for k in ('cp', 'hp', 'pp', 'cs', 'hs', 'ps')}
    for l in range(depth):
        last = l == depth - 1
        xs, c_s, h_s, p_s, c_m, h_m, p_m = _mixer_step_call(xs, state_conv, state_lru, state_pool, w, l, segs_s,
                                                            f"mixer_step_l{l}")
        xs = _ffn_call(xs, w, l, last, f"ffn_step_l{l}")
        xp, c_p, h_p, p_p = _layer_stream_call(xp, c_m, h_m, p_m, w, l, last, TM_LAYER, f"layer_prompt_l{l}")

        outs['cp'].append(c_p); outs['hp'].append(h_p[:, 0]); outs['pp'].append(p_p)
        outs['cs'].append(c_s); outs['hs'].append(h_s[:, 0]); outs['ps'].append(p_s)

    y_sample = xs[:ns * ts].reshape(ns, ts, D_MODEL)
    return (xp, y_sample, jnp.stack(outs['cp']), jnp.stack(outs['hp']), jnp.stack(outs['pp']),
            jnp.stack(outs['cs']), jnp.stack(outs['hs']), jnp.stack(outs['ps']))
```

```python
import functools

import jax
import jax.numpy as jnp
from jax import lax
from jax.experimental import pallas as pl
from jax.experimental.pallas import tpu as pltpu

D_MODEL = 1024
D_LRU = 1024
D_POOL = 1024
LRU_HEADS = 8
LRU_BLOCK = D_LRU // LRU_HEADS
LRU_C = 8.0
CONV_W = 4
POOL_WINDOWS = (2, 4, 8, 16)
POOL_GROUP = D_POOL // len(POOL_WINDOWS)
POOL_MAX = 16
D_FF = 2816
N_META = 16
PAST_LEN = 2048
EPS = 1e-6

SUBLANES = 8
BF16_ROWS = 16
HDR = 16
NCH = 256
N_IN_CH = 4 * D_MODEL // NCH
CONV_CH = 0
POOL_CH = D_LRU // NCH
GATE_A_CH = (D_LRU + D_POOL) // NCH
GATE_B_CH = (D_LRU + D_POOL + D_MODEL) // NCH
N_OUT_CH = D_MODEL // NCH
N_FF_CH = D_FF // NCH
TM_LAYER = 256
ROW_CH = 32
VMEM_LIMIT = 60 * 1024 * 1024

V_NORM1, V_CONV_B, V_BR, V_BI, V_LAM, V_POOL_SCALE, V_NORM2, V_FINAL = range(8)
VEC_ROWS = 8
SQ_BRANCH_A, SQ_BRANCH_B, SQ_OUT = range(3)

f32 = jnp.float32
bf16 = jnp.bfloat16

assert POOL_GROUP == NCH and HDR >= POOL_MAX - 1 and HDR % BF16_ROWS == 0
assert D_LRU == D_POOL == D_MODEL


def _dot(a, b):
    return jnp.dot(a, b, preferred_element_type=f32)


def _chunk(c):
    return slice(c * NCH, (c + 1) * NCH)


def _vec(v_ref, row, cols=slice(None)):
    return v_ref[row:row + 1, cols]


def _rmsnorm(x, g):
    y = x * lax.rsqrt(jnp.mean(x * x, axis=-1, keepdims=True) + EPS)
    return y * g


def _norm_chunk(tm):
    return max(c for c in range(BF16_ROWS, 129, BF16_ROWS) if tm % c == 0)


def _sigmoid(x):
    return 0.5 * jnp.tanh(0.5 * x) + 0.5


def _shift_rows(x, k):
    return pltpu.roll(x, k, axis=0)


def _for_rows(total, chunk, fn):
    n = total // chunk
    assert n * chunk == total
    if n == 1:
        fn(0)
    else:
        def body(k, c):
            fn(pl.multiple_of(k * chunk, chunk))
            return c
        lax.fori_loop(0, n, body, 0)


def _norm_to_bf16(x_ref, g, xn_s, tm):
    nc = _norm_chunk(tm)

    def norm_rows(r0):
        rows = pl.ds(r0, nc)
        xn_s[rows, :] = _rmsnorm(x_ref[rows, :], g).astype(bf16)
    _for_rows(tm, nc, norm_rows)


def _conv_taps(blk, v_ref, cw_ref, cs):
    tap = lambda k: _shift_rows(blk, CONV_W - 1 - k)[HDR:] if k < CONV_W - 1 else blk[HDR:]
    acc = _vec(v_ref, V_CONV_B, cs) + cw_ref[0:1, cs] * tap(0)
    for k in range(1, CONV_W):
        acc = acc + cw_ref[k:k + 1, cs] * tap(k)
    return acc


def _lam_scaled(v_ref):
    lam = _vec(v_ref, V_LAM)
    log_sig_lam = jnp.minimum(lam, 0.0) - jnp.log1p(jnp.exp(-jnp.abs(lam)))
    return LRU_C * log_sig_lam


def _lru_coeffs(r_pre, i_pre, c, lam_c, start_mask=None):
    r = _sigmoid(r_pre)
    i = _sigmoid(i_pre)
    log_a = r * lam_c
    a = jnp.exp(log_a)
    th = jnp.tanh(log_a)
    mult = jnp.sqrt(-2.0 * th / (1.0 - th))
    if start_mask is not None:
        mult = jnp.where(start_mask, 1.0, mult)
    return a, mult * (i * c)


def _scan_block(a, b, h):
    sub = lax.broadcasted_iota(jnp.int32, a.shape, 0)
    for d in (1, 2, 4):
        keep = sub >= d
        a_prev = pltpu.roll(a, d, axis=0)
        b_prev = pltpu.roll(b, d, axis=0)
        b = jnp.where(keep, a * b_prev, 0.0) + b
        a = jnp.where(keep, a * a_prev, a)
    return a * h + b


def _scan_rows(a, b, h):
    outs = []
    for k in range(a.shape[0] // SUBLANES):
        blk = slice(k * SUBLANES, (k + 1) * SUBLANES)
        outs.append(_scan_block(a[blk], b[blk], h))
        h = outs[-1][SUBLANES - 1:SUBLANES, :]
    return jnp.concatenate(outs, axis=0), h


def _pool_diff(blk, w, cnt=None):
    xb = blk[HDR:]
    tot, span = blk, 1
    while span < w:
        tot = tot + _shift_rows(tot, span)
        span *= 2
    tot = tot[HDR:]
    pooled = tot * (1.0 / w) if cnt is None else tot / cnt
    return pooled - xb


def _lru_gate_preact(h, cbf, wri_ref, v_ref):
    hs = slice(h * LRU_BLOCK, (h + 1) * LRU_BLOCK)
    ri = _dot(cbf[:, hs], wri_ref[h])
    return ri[:, :LRU_BLOCK] + _vec(v_ref, V_BR, hs), ri[:, LRU_BLOCK:] + _vec(v_ref, V_BI, hs)


def _pool_project(g, d_s, yb_s, pw_ref, v_ref):
    gs = _chunk(g)
    yb_s[:, gs] = (_dot(d_s[:, gs], pw_ref[g]) * _vec(v_ref, V_POOL_SCALE, gs)).astype(bf16)


def _branch_b(c, gate_b, yb_s, wsq_ref, mb_s):
    mb_s[:, _chunk(c)] = _sigmoid(gate_b) * _dot(yb_s[...], wsq_ref[SQ_BRANCH_B, :, _chunk(c)])


def _merge(c, gate_a, ya_s, wsq_ref, mb_s, m_s):
    cs = _chunk(c)
    m_s[:, cs] = (_sigmoid(gate_a) * _dot(ya_s[...], wsq_ref[SQ_BRANCH_A, :, cs]) + mb_s[:, cs]).astype(bf16)


def _mixer_step_kernel(segs, tm, x_ref, convp_ref, hp_ref, poolp_ref, v_ref, cw_ref, win_ref, wri_ref, pw_ref,
                       wsq_ref,
                       y_ref, convn_ref, hn_ref, pooln_ref, convm_ref, hm_ref, poolm_ref,
                       xn_s, proj, gbuf, cbuf, cbf, a_s, b_s, ya_s, d_s, yb_s, mb_s, m_s):
    n_hist = GATE_A_CH
    assert sum(1 for seg in segs if not seg[3]) == 1
    for s, (_, _, _, has_state) in enumerate(segs):
        for c in range(n_hist):
            proj[s, c, 0:HDR, :] = jnp.zeros((HDR, NCH), f32)
        if has_state:
            for c in range(POOL_CH):
                proj[s, CONV_CH + c, HDR - (CONV_W - 1):HDR, :] = convp_ref[s, :, _chunk(c)]
                proj[s, POOL_CH + c, HDR - (POOL_MAX - 1):HDR, :] = poolp_ref[s, :, _chunk(c)]

    _norm_to_bf16(x_ref, _vec(v_ref, V_NORM1), xn_s, tm)

    for c in range(N_IN_CH):
        p = _dot(xn_s[...], win_ref[:, _chunk(c)])
        if c < n_hist:
            for s, (row0, length, _, _) in enumerate(segs):
                proj[s, c, HDR:HDR + length, :] = p[row0:row0 + length]
        else:
            gbuf[c - n_hist] = p

    for s, (row0, length, _, has_state) in enumerate(segs):
        n = min(ROW_CH, length)

        def conv_rows(t0, s=s, row0=row0, n=n):
            rows = pl.ds(row0 + t0, n)
            for c in range(POOL_CH):
                acc = _conv_taps(proj[s, CONV_CH + c, pl.ds(t0, n + HDR), :], v_ref, cw_ref, _chunk(c))
                cbuf[rows, _chunk(c)] = acc
                cbf[rows, _chunk(c)] = acc.astype(bf16)
        _for_rows(length, n, conv_rows)
        conv_out, pool_out, k = (convn_ref, pooln_ref, s) if has_state else (convm_ref, poolm_ref, 0)
        for c in range(POOL_CH):
            conv_out[k, :, _chunk(c)] = proj[s, CONV_CH + c, HDR + length - (CONV_W - 1):HDR + length, :]
            pool_out[k, :, _chunk(c)] = proj[s, POOL_CH + c, HDR + length - (POOL_MAX - 1):HDR + length, :]

    for h in range(LRU_HEADS):
        hs = slice(h * LRU_BLOCK, (h + 1) * LRU_BLOCK)
        a_s[:, hs], b_s[:, hs] = _lru_gate_preact(h, cbf, wri_ref, v_ref)
    lam_c = _lam_scaled(v_ref)

    for s, (row0, length, pos0, has_state) in enumerate(segs):
        n = min(ROW_CH, length)

        def mix_rows(t0, h, s=s, row0=row0, n=n, pos0=pos0):
            rows = pl.ds(row0 + t0, n)
            pos = pos0 + t0 + lax.broadcasted_iota(jnp.int32, (n, 1), 0)
            start = (pos == 0) if pos0 == 0 else None
            a, b = _lru_coeffs(a_s[rows, :], b_s[rows, :], cbuf[rows, :], lam_c, start)
            states, h = _scan_rows(a, b, h)
            ya_s[rows, :] = states.astype(bf16)
            cnt = None if pos0 >= POOL_MAX - 1 else pos + 1
            for g, w in enumerate(POOL_WINDOWS):
                cnt_g = None if cnt is None else jnp.minimum(w, cnt).astype(f32)
                d_s[rows, _chunk(g)] = _pool_diff(proj[s, POOL_CH + g, pl.ds(t0, n + HDR), :], w, cnt_g).astype(bf16)
            return h

        steps = length // n
        h = hp_ref[s] if has_state else jnp.zeros((1, D_LRU), f32)
        if steps == 1:
            h = mix_rows(0, h)
        else:
            h = lax.fori_loop(0, steps, lambda k, h: mix_rows(pl.multiple_of(k * n, n), h), h)
        if has_state:
            hn_ref[s] = h
        else:
            hm_ref[0] = h

    for g in range(len(POOL_WINDOWS)):
        _pool_project(g, d_s, yb_s, pw_ref, v_ref)
    for c in range(N_OUT_CH):
        _branch_b(c, gbuf[GATE_B_CH - n_hist + c], yb_s, wsq_ref, mb_s)
        _merge(c, gbuf[GATE_A_CH - n_hist + c], ya_s, wsq_ref, mb_s, m_s)
    for c in range(N_OUT_CH):
        y_ref[:, _chunk(c)] = x_ref[:, _chunk(c)] + _dot(m_s[...], wsq_ref[SQ_OUT, :, _chunk(c)])


def _layer_stream_kernel(tm, tiles_per_stream, n_tiles, final,
                         x_ref, convp_ref, hp_ref, poolp_ref, v_ref, cw_ref, win_ref, wri_ref, pw_ref, wsq_ref,
                         wgu_ref, wd_ref,
                         y_ref, convn_ref, hn_ref, pooln_ref,
                         xn_s, hist, gbuf, cbuf, cbf, ya_s, d_s, yb_s, mb_s, m_s, hcar, ymid, xn2_s, h_s):
    j = pl.program_id(0)
    new = j % 2
    cur = 1 - new
    n_hist = GATE_A_CH
    nc = _norm_chunk(tm)

    @pl.when(j == 0)
    def _():
        hist[...] = jnp.zeros(hist.shape, f32)
        hcar[...] = jnp.zeros(hcar.shape, f32)
        ymid[1] = jnp.zeros((tm, D_MODEL), f32)

    for c in range(POOL_CH):
        convn_ref[0, :, _chunk(c)] = hist[CONV_CH + c, HDR - (CONV_W - 1):HDR, :]
        pooln_ref[0, :, _chunk(c)] = hist[POOL_CH + c, HDR - (POOL_MAX - 1):HDR, :]
    hn_ref[0] = hcar[0:1, :]

    @pl.when(jnp.logical_and(j % tiles_per_stream == 0, j < n_tiles))
    def _():
        for c in range(POOL_CH):
            hist[CONV_CH + c] = jnp.zeros((HDR, NCH), f32)
            hist[POOL_CH + c] = jnp.zeros((HDR, NCH), f32)
            hist[CONV_CH + c, HDR - (CONV_W - 1):HDR, :] = convp_ref[0, :, _chunk(c)]
            hist[POOL_CH + c, HDR - (POOL_MAX - 1):HDR, :] = poolp_ref[0, :, _chunk(c)]
        hcar[0:1, :] = hp_ref[0]

    lam_c = _lam_scaled(v_ref)

    def ffn_norm():
        for r0 in range(0, tm, nc):
            xn2_s[r0:r0 + nc, :] = _rmsnorm(ymid[cur, r0:r0 + nc, :], _vec(v_ref, V_NORM2)).astype(bf16)

    def ffn_up(c):
        gate = _dot(xn2_s[...], wgu_ref[:, _chunk(c)])
        up = _dot(xn2_s[...], wgu_ref[:, D_FF + c * NCH:D_FF + (c + 1) * NCH])
        h_s[:, _chunk(c)] = (jax.nn.silu(gate) * up).astype(bf16)

    def ffn_down(c):
        y_ref[:, _chunk(c)] = ymid[cur, :, _chunk(c)] + _dot(h_s[...], wd_ref[:, _chunk(c)])

    def ffn_final_norm():
        for r0 in range(0, tm, nc):
            y_ref[r0:r0 + nc, :] = _rmsnorm(y_ref[r0:r0 + nc, :], _vec(v_ref, V_FINAL))

    def mixer_norm():
        for r0 in range(0, tm, nc):
            xn_s[r0:r0 + nc, :] = _rmsnorm(x_ref[r0:r0 + nc, :], _vec(v_ref, V_NORM1)).astype(bf16)

    def project_with_history(c):
        p = _dot(xn_s[...], win_ref[:, _chunk(c)])
        ext = jnp.concatenate([hist[c], p], axis=0)
        hist[c] = p[tm - HDR:]
        return ext

    def conv_chunk(c):
        acc = _conv_taps(project_with_history(CONV_CH + c), v_ref, cw_ref, _chunk(c))
        cbuf[:, _chunk(c)] = acc
        cbf[:, _chunk(c)] = acc.astype(bf16)

    def pool_chunk(g):
        d_s[:, _chunk(g)] = _pool_diff(project_with_history(POOL_CH + g), POOL_WINDOWS[g]).astype(bf16)

    def gate_chunk(c):
        gbuf[c - n_hist] = _dot(xn_s[...], win_ref[:, _chunk(c)])

    def lru_head(h):
        hs = slice(h * LRU_BLOCK, (h + 1) * LRU_BLOCK)
        r_pre, i_pre = _lru_gate_preact(h, cbf, wri_ref, v_ref)
        a, b = _lru_coeffs(r_pre, i_pre, cbuf[:, hs], lam_c[:, hs])
        states, last = _scan_rows(a, b, hcar[0:1, hs])
        ya_s[:, hs] = states.astype(bf16)
        hcar[0:1, hs] = last

    def project_out(c):
        ymid[new, :, _chunk(c)] = x_ref[:, _chunk(c)] + _dot(m_s[...], wsq_ref[SQ_OUT, :, _chunk(c)])

    filler = [functools.partial(ffn_up, c) for c in range(N_FF_CH)]
    filler += [functools.partial(ffn_down, c) for c in range(N_OUT_CH)]
    if final:
        filler.append(ffn_final_norm)
    others = [functools.partial(pool_chunk, g) for g in range(len(POOL_WINDOWS))]
    others += [functools.partial(gate_chunk, GATE_B_CH + c) for c in range(N_OUT_CH)]
    others += [functools.partial(_pool_project, g, d_s, yb_s, pw_ref, v_ref) for g in range(len(POOL_WINDOWS))]
    others += [lambda c=c: _branch_b(c, gbuf[GATE_B_CH - n_hist + c], yb_s, wsq_ref, mb_s) for c in range(N_OUT_CH)]
    others += [functools.partial(gate_chunk, GATE_A_CH + c) for c in range(N_OUT_CH)]

    stages = [functools.partial(conv_chunk, c) for c in range(POOL_CH)]
    stages += [functools.partial(lru_head, h) for h in range(LRU_HEADS)]
    first_head = POOL_CH

    ffn_norm()
    mixer_norm()
    for k, stage in enumerate(stages):
        stage()
        lo, hi = (k * len(filler)) // len(stages), ((k + 1) * len(filler)) // len(stages)
        for fn in filler[lo:hi]:
            fn()
        if k >= first_head:
            h = k - first_head
            lo, hi = (h * len(others)) // LRU_HEADS, ((h + 1) * len(others)) // LRU_HEADS
            for fn in others[lo:hi]:
                fn()
    for c in range(N_OUT_CH):
        _merge(c, gbuf[GATE_A_CH - n_hist + c], ya_s, wsq_ref, mb_s, m_s)
    for c in range(N_OUT_CH):
        project_out(c)


def _ffn_kernel(final, tm, x_ref, v_ref, wgu_ref, wd_ref, y_ref, xn_s, h_s):
    nc = _norm_chunk(tm)
    _norm_to_bf16(x_ref, _vec(v_ref, V_NORM2), xn_s, tm)

    for c in range(N_FF_CH):
        gate = _dot(xn_s[...], wgu_ref[:, _chunk(c)])
        up = _dot(xn_s[...], wgu_ref[:, D_FF + c * NCH:D_FF + (c + 1) * NCH])
        h_s[:, _chunk(c)] = (jax.nn.silu(gate) * up).astype(bf16)

    for c in range(N_OUT_CH):
        y_ref[:, _chunk(c)] = x_ref[:, _chunk(c)] + _dot(h_s[...], wd_ref[:, _chunk(c)])

    if final:
        def final_rows(r0):
            rows = pl.ds(r0, nc)
            y_ref[rows, :] = _rmsnorm(y_ref[rows, :], _vec(v_ref, V_FINAL))
        _for_rows(tm, nc, final_rows)


def _layer_spec(w, layer):
    nd = w.ndim - 1
    return pl.BlockSpec((None,) + w.shape[1:], lambda *_: (layer,) + (0,) * nd, pipeline_mode=pl.Buffered(1))


def _mixer_weights(w):
    return (w['vec'], w['conv_w'], w['w_in'], w['wri'], w['pool_w'], w['wsq'])


def _state_shapes(n):
    return (jax.ShapeDtypeStruct((n, CONV_W - 1, D_LRU), f32),
            jax.ShapeDtypeStruct((n, 1, D_LRU), f32),
            jax.ShapeDtypeStruct((n, POOL_MAX - 1, D_POOL), f32))


def _state_specs(n, index_map):
    return [pl.BlockSpec((n, CONV_W - 1, D_LRU), index_map), pl.BlockSpec((n, 1, D_LRU), index_map),
            pl.BlockSpec((n, POOL_MAX - 1, D_POOL), index_map)]


def _mixer_step_call(x, state_conv, state_lru, state_pool, w, layer, segs, name):
    tm = x.shape[0]
    ns = state_conv.shape[1]
    max_len = max(seg[1] for seg in segs)
    weights = _mixer_weights(w)
    x_spec = pl.BlockSpec((tm, D_MODEL), lambda i: (0, 0))
    whole = lambda i: (0, 0, 0)
    state_in = [pl.BlockSpec((None,) + s.shape[1:], lambda i: (layer, 0, 0, 0))
                for s in (state_conv, state_lru, state_pool)]
    scratch = [
        pltpu.VMEM((tm, D_MODEL), bf16),
        pltpu.VMEM((len(segs), GATE_A_CH, HDR + max_len, NCH), f32),
        pltpu.VMEM((N_IN_CH - GATE_A_CH, tm, NCH), f32),
        pltpu.VMEM((tm, D_LRU), f32),
        pltpu.VMEM((tm, D_LRU), bf16),
        pltpu.VMEM((tm, D_LRU), f32),
        pltpu.VMEM((tm, D_LRU), f32),
        pltpu.VMEM((tm, D_LRU), bf16),
        pltpu.VMEM((tm, D_POOL), bf16),
        pltpu.VMEM((tm, D_POOL), bf16),
        pltpu.VMEM((tm, D_MODEL), f32),
        pltpu.VMEM((tm, D_MODEL), bf16),
    ]
    return pl.pallas_call(
        functools.partial(_mixer_step_kernel, tuple(segs), tm),
        grid=(1,),
        in_specs=[x_spec] + state_in + [_layer_spec(v, layer) for v in weights],
        out_specs=[x_spec] + _state_specs(ns, whole) + _state_specs(1, whole),
        out_shape=(jax.ShapeDtypeStruct(x.shape, f32),) + _state_shapes(ns) + _state_shapes(1),
        scratch_shapes=scratch,
        compiler_params=pltpu.CompilerParams(dimension_semantics=("arbitrary",), vmem_limit_bytes=VMEM_LIMIT),
        name=name,
    )(x, state_conv, state_lru, state_pool, *weights)


def _layer_stream_call(x, conv_init, h_init, pool_init, w, layer, final, tm, name):
    nb, t, _ = x.shape
    tps = t // tm
    n_tiles = nb * tps
    assert tps * tm == t
    weights = _mixer_weights(w) + (w['w_gu'], w['w_down'])
    x_tiles = x.reshape(n_tiles, tm, D_MODEL)
    x_spec = pl.BlockSpec((None, tm, D_MODEL), lambda j: (jnp.minimum(j, n_tiles - 1), 0, 0))
    y_spec = pl.BlockSpec((None, tm, D_MODEL), lambda j: (jnp.maximum(j - 1, 0), 0, 0))
    scratch = [
        pltpu.VMEM((tm, D_MODEL), bf16),
        pltpu.VMEM((GATE_A_CH, HDR, NCH), f32),
        pltpu.VMEM((N_IN_CH - GATE_A_CH, tm, NCH), f32),
        pltpu.VMEM((tm, D_LRU), f32),
        pltpu.VMEM((tm, D_LRU), bf16),
        pltpu.VMEM((tm, D_LRU), bf16),
        pltpu.VMEM((tm, D_POOL), bf16),
        pltpu.VMEM((tm, D_POOL), bf16),
        pltpu.VMEM((tm, D_MODEL), f32),
        pltpu.VMEM((tm, D_MODEL), bf16),
        pltpu.VMEM((SUBLANES, D_LRU), f32),
        pltpu.VMEM((2, tm, D_MODEL), f32),
        pltpu.VMEM((tm, D_MODEL), bf16),
        pltpu.VMEM((tm, D_FF), bf16),
    ]
    y, c_n, h_n, p_n = pl.pallas_call(
        functools.partial(_layer_stream_kernel, tm, tps, n_tiles, final),
        grid=(n_tiles + 1,),
        in_specs=[x_spec] + _state_specs(1, lambda j: (0, 0, 0)) + [_layer_spec(v, layer) for v in weights],
        out_specs=[y_spec] + _state_specs(1, lambda j: (jnp.maximum(j - 1, 0) // tps, 0, 0)),
        out_shape=(jax.ShapeDtypeStruct(x_tiles.shape, f32),) + _state_shapes(nb),
        scratch_shapes=scratch,
        compiler_params=pltpu.CompilerParams(dimension_semantics=("arbitrary",), vmem_limit_bytes=VMEM_LIMIT),
        name=name,
    )(x_tiles, conv_init, h_init, pool_init, *weights)
    return y.reshape(x.shape), c_n, h_n, p_n


def _ffn_call(x, w, layer, final, name):
    tm = x.shape[0]
    x_spec = pl.BlockSpec((tm, D_MODEL), lambda i: (0, 0))
    weights = (w['vec'], w['w_gu'], w['w_down'])
    return pl.pallas_call(
        functools.partial(_ffn_kernel, final, tm),
        grid=(1,),
        in_specs=[x_spec] + [_layer_spec(v, layer) for v in weights],
        out_specs=x_spec,
        out_shape=jax.ShapeDtypeStruct(x.shape, f32),
        scratch_shapes=[pltpu.VMEM((tm, D_MODEL), bf16), pltpu.VMEM((tm, D_FF), bf16)],
        compiler_params=pltpu.CompilerParams(dimension_semantics=("arbitrary",), vmem_limit_bytes=VMEM_LIMIT),
        name=name,
    )(x, *weights)


def kernel(x_prompt, x_sample, state_conv, state_lru, state_pool, meta_tokens, norm1_g, w_in, conv_w, conv_b,
           lam, w_r, b_r, w_i, b_i, pool_w, pool_scale, w_br_a, w_br_b, w_out, norm2_g, w_gu, w_down, final_g):
    depth = w_in.shape[0]
    ns, ts, _ = x_sample.shape
    vec_rows = [None] * VEC_ROWS
    for row, v in ((V_NORM1, norm1_g), (V_CONV_B, conv_b), (V_BR, b_r), (V_BI, b_i), (V_LAM, lam),
                   (V_POOL_SCALE, pool_scale), (V_NORM2, norm2_g),
                   (V_FINAL, jnp.broadcast_to(final_g, (depth, D_MODEL)))):
        vec_rows[row] = v.astype(f32)
    w = {
        'vec': jnp.stack(vec_rows, axis=1), 'conv_w': conv_w.astype(f32),
        'w_in': w_in.astype(bf16),
        'wri': jnp.concatenate([w_r, w_i], axis=-1).astype(bf16),
        'pool_w': pool_w.astype(bf16),
        'wsq': jnp.stack([w_br_a, w_br_b, w_out], axis=1).astype(bf16),
        'w_gu': w_gu.astype(bf16), 'w_down': w_down.astype(bf16),
    }

    segs_s = [(s * ts, ts, PAST_LEN, True) for s in range(ns)] + [(ns * ts, N_META, 0, False)]
    xs = jnp.concatenate([x_sample.reshape(ns * ts, D_MODEL), meta_tokens.astype(f32)], axis=0)
    xp = x_prompt
    state_lru = state_lru[:, :, None, :]

    outs = {k: [] for k in ('cp', 'hp', 'pp', 'cs', 'hs', 'ps')}
    for l in range(depth):
        last = l == depth - 1
        xs, c_s, h_s, p_s, c_m, h_m, p_m = _mixer_step_call(xs, state_conv, state_lru, state_pool, w, l, segs_s,
                                                            f"mixer_step_l{l}")
        xs = _ffn_call(xs, w, l, last, f"ffn_step_l{l}")
        xp, c_p, h_p, p_p = _layer_stream_call(xp, c_m, h_m, p_m, w, l, last, TM_LAYER, f"layer_prompt_l{l}")

        outs['cp'].append(c_p); outs['hp'].append(h_p[:, 0]); outs['pp'].append(p_p)
        outs['cs'].append(c_s); outs['hs'].append(h_s[:, 0]); outs['ps'].append(p_s)

    y_sample = xs[:ns * ts].reshape(ns, ts, D_MODEL)
    return (xp, y_sample, jnp.stack(outs['cp']), jnp.stack(outs['hp']), jnp.stack(outs['pp']),
            jnp.stack(outs['cs']), jnp.stack(outs['hs']), jnp.stack(outs['ps']))
```

```python
import functools

import jax
import jax.numpy as jnp
from jax import lax
from jax.experimental import pallas as pl
from jax.experimental.pallas import tpu as pltpu

D_MODEL = 1024
D_LRU = 1024
D_POOL = 1024
LRU_HEADS = 8
LRU_BLOCK = D_LRU // LRU_HEADS
LRU_C = 8.0
CONV_W = 4
POOL_WINDOWS = (2, 4, 8, 16)
POOL_GROUP = D_POOL // len(POOL_WINDOWS)
POOL_MAX = 16
D_FF = 2816
N_META = 16
PAST_LEN = 2048
EPS = 1e-6

SUBLANES = 8
BF16_ROWS = 16
HDR = 16
NCH = 256
N_IN_CH = 4 * D_MODEL // NCH
CONV_CH = 0
POOL_CH = D_LRU // NCH
GATE_A_CH = (D_LRU + D_POOL) // NCH
GATE_B_CH = (D_LRU + D_POOL + D_MODEL) // NCH
N_OUT_CH = D_MODEL // NCH
N_FF_CH = D_FF // NCH
TM_LAYER = 256
ROW_CH = 32
VMEM_LIMIT = 60 * 1024 * 1024

V_NORM1, V_CONV_B, V_BR, V_BI, V_LAM, V_POOL_SCALE, V_NORM2, V_FINAL = range(8)
VEC_ROWS = 8
SQ_BRANCH_A, SQ_BRANCH_B, SQ_OUT = range(3)

f32 = jnp.float32
bf16 = jnp.bfloat16

assert POOL_GROUP == NCH and HDR >= POOL_MAX - 1 and HDR % BF16_ROWS == 0
assert D_LRU == D_POOL == D_MODEL


def _dot(a, b):
    return jnp.dot(a, b, preferred_element_type=f32)


def _chunk(c):
    return slice(c * NCH, (c + 1) * NCH)


def _vec(v_ref, row, cols=slice(None)):
    return v_ref[row:row + 1, cols]


def _rmsnorm(x, g):
    y = x * lax.rsqrt(jnp.mean(x * x, axis=-1, keepdims=True) + EPS)
    return y * g


def _norm_chunk(tm):
    return max(c for c in range(BF16_ROWS, 129, BF16_ROWS) if tm % c == 0)


def _sigmoid(x):
    return 0.5 * jnp.tanh(0.5 * x) + 0.5


def _shift_rows(x, k):
    return pltpu.roll(x, k, axis=0)


def _for_rows(total, chunk, fn):
    n = total // chunk
    assert n * chunk == total
    if n == 1:
        fn(0)
    else:
        def body(k, c):
            fn(pl.multiple_of(k * chunk, chunk))
            return c
        lax.fori_loop(0, n, body, 0)


def _norm_to_bf16(x_ref, g, xn_s, tm):
    nc = _norm_chunk(tm)

    def norm_rows(r0):
        rows = pl.ds(r0, nc)
        xn_s[rows, :] = _rmsnorm(x_ref[rows, :], g).astype(bf16)
    _for_rows(tm, nc, norm_rows)


def _conv_taps(blk, v_ref, cw_ref, cs):
    tap = lambda k: _shift_rows(blk, CONV_W - 1 - k)[HDR:] if k < CONV_W - 1 else blk[HDR:]
    acc = _vec(v_ref, V_CONV_B, cs) + cw_ref[0:1, cs] * tap(0)
    for k in range(1, CONV_W):
        acc = acc + cw_ref[k:k + 1, cs] * tap(k)
    return acc


def _lam_scaled(v_ref):
    lam = _vec(v_ref, V_LAM)
    log_sig_lam = jnp.minimum(lam, 0.0) - jnp.log1p(jnp.exp(-jnp.abs(lam)))
    return LRU_C * log_sig_lam


def _lru_coeffs(r_pre, i_pre, c, lam_c, start_mask=None):
    r = _sigmoid(r_pre)
    i = _sigmoid(i_pre)
    log_a = r * lam_c
    a = jnp.exp(log_a)
    th = jnp.tanh(log_a)
    mult = jnp.sqrt(-2.0 * th / (1.0 - th))
    if start_mask is not None:
        mult = jnp.where(start_mask, 1.0, mult)
    return a, mult * (i * c)


def _scan_block(a, b, h):
    sub = lax.broadcasted_iota(jnp.int32, a.shape, 0)
    for d in (1, 2, 4):
        keep = sub >= d
        a_prev = pltpu.roll(a, d, axis=0)
        b_prev = pltpu.roll(b, d, axis=0)
        b = jnp.where(keep, a * b_prev, 0.0) + b
        a = jnp.where(keep, a * a_prev, a)
    return a * h + b


def _scan_rows(a, b, h):
    outs = []
    for k in range(a.shape[0] // SUBLANES):
        blk = slice(k * SUBLANES, (k + 1) * SUBLANES)
        outs.append(_scan_block(a[blk], b[blk], h))
        h = outs[-1][SUBLANES - 1:SUBLANES, :]
    return jnp.concatenate(outs, axis=0), h


def _pool_diff(blk, w, cnt=None):
    xb = blk[HDR:]
    tot, span = blk, 1
    while span < w:
        tot = tot + _shift_rows(tot, span)
        span *= 2
    tot = tot[HDR:]
    pooled = tot * (1.0 / w) if cnt is None else tot / cnt
    return pooled - xb


def _lru_gate_preact(h, cbf, wri_ref, v_ref):
    hs = slice(h * LRU_BLOCK, (h + 1) * LRU_BLOCK)
    ri = _dot(cbf[:, hs], wri_ref[h])
    return ri[:, :LRU_BLOCK] + _vec(v_ref, V_BR, hs), ri[:, LRU_BLOCK:] + _vec(v_ref, V_BI, hs)


def _pool_project(g, d_s, yb_s, pw_ref, v_ref):
    gs = _chunk(g)
    yb_s[:, gs] = (_dot(d_s[:, gs], pw_ref[g]) * _vec(v_ref, V_POOL_SCALE, gs)).astype(bf16)


def _branch_b(c, gate_b, yb_s, wsq, mb_s):
    mb_s[:, _chunk(c)] = _sigmoid(gate_b) * _dot(yb_s[...], wsq[SQ_BRANCH_B][:, _chunk(c)])


def _merge(c, gate_a, ya_s, wsq, mb_s, m_s):
    cs = _chunk(c)
    m_s[:, cs] = (_sigmoid(gate_a) * _dot(ya_s[...], wsq[SQ_BRANCH_A][:, cs]) + mb_s[:, cs]).astype(bf16)


def _mixer_step_kernel(segs, tm, x_ref, convp_ref, hp_ref, poolp_ref, v_ref, cw_ref, win_ref, wri_ref, pw_ref,
                       wa_ref, wb_ref, wo_ref,
                       y_ref, convn_ref, hn_ref, pooln_ref, convm_ref, hm_ref, poolm_ref,
                       xn_s, proj, gbuf, cbuf, cbf, a_s, b_s, ya_s, d_s, yb_s, mb_s, m_s):
    n_hist = GATE_A_CH
    wsq = (wa_ref, wb_ref, wo_ref)
    assert sum(1 for seg in segs if not seg[3]) == 1
    for s, (_, _, _, has_state) in enumerate(segs):
        for c in range(n_hist):
            proj[s, c, 0:HDR, :] = jnp.zeros((HDR, NCH), f32)
        if has_state:
            for c in range(POOL_CH):
                proj[s, CONV_CH + c, HDR - (CONV_W - 1):HDR, :] = convp_ref[s, :, _chunk(c)]
                proj[s, POOL_CH + c, HDR - (POOL_MAX - 1):HDR, :] = poolp_ref[s, :, _chunk(c)]

    _norm_to_bf16(x_ref, _vec(v_ref, V_NORM1), xn_s, tm)

    for c in range(N_IN_CH):
        p = _dot(xn_s[...], win_ref[:, _chunk(c)])
        if c < n_hist:
            for s, (row0, length, _, _) in enumerate(segs):
                proj[s, c, HDR:HDR + length, :] = p[row0:row0 + length]
        else:
            gbuf[c - n_hist] = p

    for s, (row0, length, _, has_state) in enumerate(segs):
        n = min(ROW_CH, length)

        def conv_rows(t0, s=s, row0=row0, n=n):
            rows = pl.ds(row0 + t0, n)
            for c in range(POOL_CH):
                acc = _conv_taps(proj[s, CONV_CH + c, pl.ds(t0, n + HDR), :], v_ref, cw_ref, _chunk(c))
                cbuf[rows, _chunk(c)] = acc
                cbf[rows, _chunk(c)] = acc.astype(bf16)
        _for_rows(length, n, conv_rows)
        conv_out, pool_out, k = (convn_ref, pooln_ref, s) if has_state else (convm_ref, poolm_ref, 0)
        for c in range(POOL_CH):
            conv_out[k, :, _chunk(c)] = proj[s, CONV_CH + c, HDR + length - (CONV_W - 1):HDR + length, :]
            pool_out[k, :, _chunk(c)] = proj[s, POOL_CH + c, HDR + length - (POOL_MAX - 1):HDR + length, :]

    for h in range(LRU_HEADS):
        hs = slice(h * LRU_BLOCK, (h + 1) * LRU_BLOCK)
        a_s[:, hs], b_s[:, hs] = _lru_gate_preact(h, cbf, wri_ref, v_ref)
    lam_c = _lam_scaled(v_ref)

    for s, (row0, length, pos0, has_state) in enumerate(segs):
        n = min(ROW_CH, length)

        def mix_rows(t0, h, s=s, row0=row0, n=n, pos0=pos0):
            rows = pl.ds(row0 + t0, n)
            pos = pos0 + t0 + lax.broadcasted_iota(jnp.int32, (n, 1), 0)
            start = (pos == 0) if pos0 == 0 else None
            a, b = _lru_coeffs(a_s[rows, :], b_s[rows, :], cbuf[rows, :], lam_c, start)
            states, h = _scan_rows(a, b, h)
            ya_s[rows, :] = states.astype(bf16)
            cnt = None if pos0 >= POOL_MAX - 1 else pos + 1
            for g, w in enumerate(POOL_WINDOWS):
                cnt_g = None if cnt is None else jnp.minimum(w, cnt).astype(f32)
                d_s[rows, _chunk(g)] = _pool_diff(proj[s, POOL_CH + g, pl.ds(t0, n + HDR), :], w, cnt_g).astype(bf16)
            return h

        steps = length // n
        h = hp_ref[s] if has_state else jnp.zeros((1, D_LRU), f32)
        if steps == 1:
            h = mix_rows(0, h)
        else:
            h = lax.fori_loop(0, steps, lambda k, h: mix_rows(pl.multiple_of(k * n, n), h), h)
        if has_state:
            hn_ref[s] = h
        else:
            hm_ref[0] = h

    for g in range(len(POOL_WINDOWS)):
        _pool_project(g, d_s, yb_s, pw_ref, v_ref)
    for c in range(N_OUT_CH):
        _branch_b(c, gbuf[GATE_B_CH - n_hist + c], yb_s, wsq, mb_s)
        _merge(c, gbuf[GATE_A_CH - n_hist + c], ya_s, wsq, mb_s, m_s)
    for c in range(N_OUT_CH):
        y_ref[:, _chunk(c)] = x_ref[:, _chunk(c)] + _dot(m_s[...], wsq[SQ_OUT][:, _chunk(c)])


def _layer_stream_kernel(tm, tiles_per_stream, n_tiles, final,
                         x_ref, convp_ref, hp_ref, poolp_ref, v_ref, cw_ref, win_ref, wri_ref, pw_ref,
                         wa_ref, wb_ref, wo_ref, wgu_ref, wd_ref,
                         y_ref, convn_ref, hn_ref, pooln_ref,
                         xn_s, hist, gbuf, cbuf, cbf, ya_s, d_s, yb_s, mb_s, m_s, hcar, ymid, xn2_s, h_s):
    j = pl.program_id(0)
    new = j % 2
    cur = 1 - new
    n_hist = GATE_A_CH
    nc = _norm_chunk(tm)
    wsq = (wa_ref, wb_ref, wo_ref)

    @pl.when(j == 0)
    def _():
        hist[...] = jnp.zeros(hist.shape, f32)
        hcar[...] = jnp.zeros(hcar.shape, f32)
        ymid[1] = jnp.zeros((tm, D_MODEL), f32)

    for c in range(POOL_CH):
        convn_ref[0, :, _chunk(c)] = hist[CONV_CH + c, HDR - (CONV_W - 1):HDR, :]
        pooln_ref[0, :, _chunk(c)] = hist[POOL_CH + c, HDR - (POOL_MAX - 1):HDR, :]
    hn_ref[0] = hcar[0:1, :]

    @pl.when(jnp.logical_and(j % tiles_per_stream == 0, j < n_tiles))
    def _():
        for c in range(POOL_CH):
            hist[CONV_CH + c] = jnp.zeros((HDR, NCH), f32)
            hist[POOL_CH + c] = jnp.zeros((HDR, NCH), f32)
            hist[CONV_CH + c, HDR - (CONV_W - 1):HDR, :] = convp_ref[0, :, _chunk(c)]
            hist[POOL_CH + c, HDR - (POOL_MAX - 1):HDR, :] = poolp_ref[0, :, _chunk(c)]
        hcar[0:1, :] = hp_ref[0]

    lam_c = _lam_scaled(v_ref)

    def ffn_norm():
        for r0 in range(0, tm, nc):
            xn2_s[r0:r0 + nc, :] = _rmsnorm(ymid[cur, r0:r0 + nc, :], _vec(v_ref, V_NORM2)).astype(bf16)

    def ffn_up(c):
        gate = _dot(xn2_s[...], wgu_ref[:, _chunk(c)])
        up = _dot(xn2_s[...], wgu_ref[:, D_FF + c * NCH:D_FF + (c + 1) * NCH])
        h_s[:, _chunk(c)] = (jax.nn.silu(gate) * up).astype(bf16)

    def ffn_down(c):
        y_ref[:, _chunk(c)] = ymid[cur, :, _chunk(c)] + _dot(h_s[...], wd_ref[:, _chunk(c)])

    def ffn_final_norm():
        for r0 in range(0, tm, nc):
            y_ref[r0:r0 + nc, :] = _rmsnorm(y_ref[r0:r0 + nc, :], _vec(v_ref, V_FINAL))

    def mixer_norm():
        for r0 in range(0, tm, nc):
            xn_s[r0:r0 + nc, :] = _rmsnorm(x_ref[r0:r0 + nc, :], _vec(v_ref, V_NORM1)).astype(bf16)

    def project_with_history(c):
        p = _dot(xn_s[...], win_ref[:, _chunk(c)])
        ext = jnp.concatenate([hist[c], p], axis=0)
        hist[c] = p[tm - HDR:]
        return ext

    def conv_chunk(c):
        acc = _conv_taps(project_with_history(CONV_CH + c), v_ref, cw_ref, _chunk(c))
        cbuf[:, _chunk(c)] = acc
        cbf[:, _chunk(c)] = acc.astype(bf16)

    def pool_chunk(g):
        d_s[:, _chunk(g)] = _pool_diff(project_with_history(POOL_CH + g), POOL_WINDOWS[g]).astype(bf16)

    def gate_chunk(c):
        gbuf[c - n_hist] = _dot(xn_s[...], win_ref[:, _chunk(c)])

    def lru_head(h):
        hs = slice(h * LRU_BLOCK, (h + 1) * LRU_BLOCK)
        r_pre, i_pre = _lru_gate_preact(h, cbf, wri_ref, v_ref)
        a, b = _lru_coeffs(r_pre, i_pre, cbuf[:, hs], lam_c[:, hs])
        states, last = _scan_rows(a, b, hcar[0:1, hs])
        ya_s[:, hs] = states.astype(bf16)
        hcar[0:1, hs] = last

    def project_out(c):
        ymid[new, :, _chunk(c)] = x_ref[:, _chunk(c)] + _dot(m_s[...], wsq[SQ_OUT][:, _chunk(c)])

    filler = [functools.partial(ffn_up, c) for c in range(N_FF_CH)]
    filler += [functools.partial(ffn_down, c) for c in range(N_OUT_CH)]
    if final:
        filler.append(ffn_final_norm)
    others = [functools.partial(pool_chunk, g) for g in range(len(POOL_WINDOWS))]
    others += [functools.partial(gate_chunk, GATE_B_CH + c) for c in range(N_OUT_CH)]
    others += [functools.partial(_pool_project, g, d_s, yb_s, pw_ref, v_ref) for g in range(len(POOL_WINDOWS))]
    others += [lambda c=c: _branch_b(c, gbuf[GATE_B_CH - n_hist + c], yb_s, wsq, mb_s) for c in range(N_OUT_CH)]
    others += [functools.partial(gate_chunk, GATE_A_CH + c) for c in range(N_OUT_CH)]

    stages = [functools.partial(conv_chunk, c) for c in range(POOL_CH)]
    stages += [functools.partial(lru_head, h) for h in range(LRU_HEADS)]
    first_head = POOL_CH

    ffn_norm()
    mixer_norm()
    for k, stage in enumerate(stages):
        stage()
        lo, hi = (k * len(filler)) // len(stages), ((k + 1) * len(filler)) // len(stages)
        for fn in filler[lo:hi]:
            fn()
        if k >= first_head:
            h = k - first_head
            lo, hi = (h * len(others)) // LRU_HEADS, ((h + 1) * len(others)) // LRU_HEADS
            for fn in others[lo:hi]:
                fn()
    for c in range(N_OUT_CH):
        _merge(c, gbuf[GATE_A_CH - n_hist + c], ya_s, wsq, mb_s, m_s)
    for c in range(N_OUT_CH):
        project_out(c)


def _ffn_kernel(final, tm, x_ref, v_ref, wgu_ref, wd_ref, y_ref, xn_s, h_s):
    nc = _norm_chunk(tm)
    _norm_to_bf16(x_ref, _vec(v_ref, V_NORM2), xn_s, tm)

    for c in range(N_FF_CH):
        gate = _dot(xn_s[...], wgu_ref[:, _chunk(c)])
        up = _dot(xn_s[...], wgu_ref[:, D_FF + c * NCH:D_FF + (c + 1) * NCH])
        h_s[:, _chunk(c)] = (jax.nn.silu(gate) * up).astype(bf16)

    for c in range(N_OUT_CH):
        y_ref[:, _chunk(c)] = x_ref[:, _chunk(c)] + _dot(h_s[...], wd_ref[:, _chunk(c)])

    if final:
        def final_rows(r0):
            rows = pl.ds(r0, nc)
            y_ref[rows, :] = _rmsnorm(y_ref[rows, :], _vec(v_ref, V_FINAL))
        _for_rows(tm, nc, final_rows)


def _layer_spec(w, layer):
    nd = w.ndim - 1
    return pl.BlockSpec((None,) + w.shape[1:], lambda *_: (layer,) + (0,) * nd, pipeline_mode=pl.Buffered(1))


def _mixer_weights(w):
    return (w['vec'], w['conv_w'], w['w_in'], w['wri'], w['pool_w'], w['w_br_a'], w['w_br_b'], w['w_out'])


def _state_shapes(n):
    return (jax.ShapeDtypeStruct((n, CONV_W - 1, D_LRU), f32),
            jax.ShapeDtypeStruct((n, 1, D_LRU), f32),
            jax.ShapeDtypeStruct((n, POOL_MAX - 1, D_POOL), f32))


def _state_specs(n, index_map):
    return [pl.BlockSpec((n, CONV_W - 1, D_LRU), index_map), pl.BlockSpec((n, 1, D_LRU), index_map),
            pl.BlockSpec((n, POOL_MAX - 1, D_POOL), index_map)]


def _mixer_step_call(x, state_conv, state_lru, state_pool, w, layer, segs, name):
    tm = x.shape[0]
    ns = state_conv.shape[1]
    max_len = max(seg[1] for seg in segs)
    weights = _mixer_weights(w)
    x_spec = pl.BlockSpec((tm, D_MODEL), lambda i: (0, 0))
    whole = lambda i: (0, 0, 0)
    state_in = [pl.BlockSpec((None,) + s.shape[1:], lambda i: (layer, 0, 0, 0))
                for s in (state_conv, state_lru, state_pool)]
    scratch = [
        pltpu.VMEM((tm, D_MODEL), bf16),
        pltpu.VMEM((len(segs), GATE_A_CH, HDR + max_len, NCH), f32),
        pltpu.VMEM((N_IN_CH - GATE_A_CH, tm, NCH), f32),
        pltpu.VMEM((tm, D_LRU), f32),
        pltpu.VMEM((tm, D_LRU), bf16),
        pltpu.VMEM((tm, D_LRU), f32),
        pltpu.VMEM((tm, D_LRU), f32),
        pltpu.VMEM((tm, D_LRU), bf16),
        pltpu.VMEM((tm, D_POOL), bf16),
        pltpu.VMEM((tm, D_POOL), bf16),
        pltpu.VMEM((tm, D_MODEL), f32),
        pltpu.VMEM((tm, D_MODEL), bf16),
    ]
    return pl.pallas_call(
        functools.partial(_mixer_step_kernel, tuple(segs), tm),
        grid=(1,),
        in_specs=[x_spec] + state_in + [_layer_spec(v, layer) for v in weights],
        out_specs=[x_spec] + _state_specs(ns, whole) + _state_specs(1, whole),
        out_shape=(jax.ShapeDtypeStruct(x.shape, f32),) + _state_shapes(ns) + _state_shapes(1),
        scratch_shapes=scratch,
        compiler_params=pltpu.CompilerParams(dimension_semantics=("arbitrary",), vmem_limit_bytes=VMEM_LIMIT),
        name=name,
    )(x, state_conv, state_lru, state_pool, *weights)


def _layer_stream_call(x, conv_init, h_init, pool_init, w, layer, final, tm, name):
    nb, t, _ = x.shape
    tps = t // tm
    n_tiles = nb * tps
    assert tps * tm == t
    weights = _mixer_weights(w) + (w['w_gu'], w['w_down'])
    x_tiles = x.reshape(n_tiles, tm, D_MODEL)
    x_spec = pl.BlockSpec((None, tm, D_MODEL), lambda j: (jnp.minimum(j, n_tiles - 1), 0, 0))
    y_spec = pl.BlockSpec((None, tm, D_MODEL), lambda j: (jnp.maximum(j - 1, 0), 0, 0))
    scratch = [
        pltpu.VMEM((tm, D_MODEL), bf16),
        pltpu.VMEM((GATE_A_CH, HDR, NCH), f32),
        pltpu.VMEM((N_IN_CH - GATE_A_CH, tm, NCH), f32),
        pltpu.VMEM((tm, D_LRU), f32),
        pltpu.VMEM((tm, D_LRU), bf16),
        pltpu.VMEM((tm, D_LRU), bf16),
        pltpu.VMEM((tm, D_POOL), bf16),
        pltpu.VMEM((tm, D_POOL), bf16),
        pltpu.VMEM((tm, D_MODEL), f32),
        pltpu.VMEM((tm, D_MODEL), bf16),
        pltpu.VMEM((SUBLANES, D_LRU), f32),
        pltpu.VMEM((2, tm, D_MODEL), f32),
        pltpu.VMEM((tm, D_MODEL), bf16),
        pltpu.VMEM((tm, D_FF), bf16),
    ]
    y, c_n, h_n, p_n = pl.pallas_call(
        functools.partial(_layer_stream_kernel, tm, tps, n_tiles, final),
        grid=(n_tiles + 1,),
        in_specs=[x_spec] + _state_specs(1, lambda j: (0, 0, 0)) + [_layer_spec(v, layer) for v in weights],
        out_specs=[y_spec] + _state_specs(1, lambda j: (jnp.maximum(j - 1, 0) // tps, 0, 0)),
        out_shape=(jax.ShapeDtypeStruct(x_tiles.shape, f32),) + _state_shapes(nb),
        scratch_shapes=scratch,
        compiler_params=pltpu.CompilerParams(dimension_semantics=("arbitrary",), vmem_limit_bytes=VMEM_LIMIT),
        name=name,
    )(x_tiles, conv_init, h_init, pool_init, *weights)
    return y.reshape(x.shape), c_n, h_n, p_n


def _ffn_call(x, w, layer, final, name):
    tm = x.shape[0]
    x_spec = pl.BlockSpec((tm, D_MODEL), lambda i: (0, 0))
    weights = (w['vec'], w['w_gu'], w['w_down'])
    return pl.pallas_call(
        functools.partial(_ffn_kernel, final, tm),
        grid=(1,),
        in_specs=[x_spec] + [_layer_spec(v, layer) for v in weights],
        out_specs=x_spec,
        out_shape=jax.ShapeDtypeStruct(x.shape, f32),
        scratch_shapes=[pltpu.VMEM((tm, D_MODEL), bf16), pltpu.VMEM((tm, D_FF), bf16)],
        compiler_params=pltpu.CompilerParams(dimension_semantics=("arbitrary",), vmem_limit_bytes=VMEM_LIMIT),
        name=name,
    )(x, *weights)


def kernel(x_prompt, x_sample, state_conv, state_lru, state_pool, meta_tokens, norm1_g, w_in, conv_w, conv_b,
           lam, w_r, b_r, w_i, b_i, pool_w, pool_scale, w_br_a, w_br_b, w_out, norm2_g, w_gu, w_down, final_g):
    depth = w_in.shape[0]
    ns, ts, _ = x_sample.shape
    vec_rows = [None] * VEC_ROWS
    for row, v in ((V_NORM1, norm1_g), (V_CONV_B, conv_b), (V_BR, b_r), (V_BI, b_i), (V_LAM, lam),
                   (V_POOL_SCALE, pool_scale), (V_NORM2, norm2_g),
                   (V_FINAL, jnp.broadcast_to(final_g, (depth, D_MODEL)))):
        vec_rows[row] = v.astype(f32)
    w = {
        'vec': jnp.stack(vec_rows, axis=1), 'conv_w': conv_w.astype(f32),
        'w_in': w_in.astype(bf16),
        'wri': jnp.concatenate([w_r, w_i], axis=-1).astype(bf16),
        'pool_w': pool_w.astype(bf16),
        'w_br_a': w_br_a.astype(bf16), 'w_br_b': w_br_b.astype(bf16), 'w_out': w_out.astype(bf16),
        'w_gu': w_gu.astype(bf16), 'w_down': w_down.astype(bf16),
    }

    segs_s = [(s * ts, ts, PAST_LEN, True) for s in range(ns)] + [(ns * ts, N_META, 0, False)]
    xs = jnp.concatenate([x_sample.reshape(ns * ts, D_MODEL), meta_tokens.astype(f32)], axis=0)
    xp = x_prompt
    state_lru = state_lru[:, :, None, :]

    outs = {k: [] for k in ('cp', 'hp', 'pp', 'cs', 'hs', 'ps')}
    for l in range(depth):
        last = l == depth - 1
        xs, c_s, h_s, p_s, c_m, h_m, p_m = _mixer_step_call(xs, state_conv, state_lru, state_pool, w, l, segs_s,
                                                            f"mixer_step_l{l}")
        xs = _ffn_call(xs, w, l, last, f"ffn_step_l{l}")
        xp, c_p, h_p, p_p = _layer_stream_call(xp, c_m, h_m, p_m, w, l, last, TM_LAYER, f"layer_prompt_l{l}")

        outs['cp'].append(c_p); outs['hp'].append(h_p[:, 0]); outs['pp'].append(p_p)
        outs['cs'].append(c_s); outs['hs'].append(h_s[:, 0]); outs['ps'].append(p_s)

    y_sample = xs[:ns * ts].reshape(ns, ts, D_MODEL)
    return (xp, y_sample, jnp.stack(outs['cp']), jnp.stack(outs['hp']), jnp.stack(outs['pp']),
            jnp.stack(outs['cs']), jnp.stack(outs['hs']), jnp.stack(outs['ps']))
```

```python
import functools

import jax
import jax.numpy as jnp
from jax import lax
from jax.experimental import pallas as pl
from jax.experimental.pallas import tpu as pltpu

D_MODEL = 1024
D_LRU = 1024
D_POOL = 1024
LRU_HEADS = 8
LRU_BLOCK = D_LRU // LRU_HEADS
LRU_C = 8.0
CONV_W = 4
POOL_WINDOWS = (2, 4, 8, 16)
POOL_GROUP = D_POOL // len(POOL_WINDOWS)
POOL_MAX = 16
D_FF = 2816
N_META = 16
PAST_LEN = 2048
EPS = 1e-6

SUBLANES = 8
BF16_ROWS = 16
HDR = 16
NCH = 256
N_IN_CH = 4 * D_MODEL // NCH
CONV_CH = 0
POOL_CH = D_LRU // NCH
GATE_A_CH = (D_LRU + D_POOL) // NCH
GATE_B_CH = (D_LRU + D_POOL + D_MODEL) // NCH
N_OUT_CH = D_MODEL // NCH
N_FF_CH = D_FF // NCH
TM_LAYER = 256
ROW_CH = 32
NORM_ROWS_MAX = 128
V7X_VMEM_BYTES = 64 * 1024 * 1024
VMEM_LIMIT = V7X_VMEM_BYTES - 4 * 1024 * 1024

V_NORM1, V_CONV_B, V_BR, V_BI, V_LAM, V_POOL_SCALE, V_NORM2, V_FINAL = range(8)
VEC_ROWS = 8
SQ_BRANCH_A, SQ_BRANCH_B, SQ_OUT = range(3)

f32 = jnp.float32
bf16 = jnp.bfloat16

assert POOL_GROUP == NCH and HDR >= POOL_MAX - 1 and HDR % BF16_ROWS == 0
assert D_LRU == D_POOL == D_MODEL


def _dot(a, b):
    return jnp.dot(a, b, preferred_element_type=f32)


def _chunk(c):
    return slice(c * NCH, (c + 1) * NCH)


def _vec(v_ref, row, cols=slice(None)):
    return v_ref[row:row + 1, cols]


def _rmsnorm(x, g):
    y = x * lax.rsqrt(jnp.mean(x * x, axis=-1, keepdims=True) + EPS)
    return y * g


def _norm_chunk(tm):
    return max(c for c in range(BF16_ROWS, NORM_ROWS_MAX + 1, BF16_ROWS) if tm % c == 0)


def _sigmoid(x):
    return 0.5 * jnp.tanh(0.5 * x) + 0.5


def _shift_rows(x, k):
    return pltpu.roll(x, k, axis=0)


def _for_rows(total, chunk, fn):
    n = total // chunk
    assert n * chunk == total
    if n == 1:
        fn(0)
    else:
        def body(k, c):
            fn(pl.multiple_of(k * chunk, chunk))
            return c
        lax.fori_loop(0, n, body, 0)


def _norm_to_bf16(x_ref, g, xn_s, tm):
    nc = _norm_chunk(tm)

    def norm_rows(r0):
        rows = pl.ds(r0, nc)
        xn_s[rows, :] = _rmsnorm(x_ref[rows, :], g).astype(bf16)
    _for_rows(tm, nc, norm_rows)


def _conv_taps(blk, v_ref, cw_ref, cs):
    tap = lambda k: _shift_rows(blk, CONV_W - 1 - k)[HDR:] if k < CONV_W - 1 else blk[HDR:]
    acc = _vec(v_ref, V_CONV_B, cs) + cw_ref[0:1, cs] * tap(0)
    for k in range(1, CONV_W):
        acc = acc + cw_ref[k:k + 1, cs] * tap(k)
    return acc


def _lam_scaled(v_ref):
    lam = _vec(v_ref, V_LAM)
    log_sig_lam = jnp.minimum(lam, 0.0) - jnp.log1p(jnp.exp(-jnp.abs(lam)))
    return LRU_C * log_sig_lam


def _lru_coeffs(r_pre, i_pre, c, lam_c, start_mask=None):
    r = _sigmoid(r_pre)
    i = _sigmoid(i_pre)
    log_a = r * lam_c
    a = jnp.exp(log_a)
    th = jnp.tanh(log_a)
    mult = jnp.sqrt(-2.0 * th / (1.0 - th))
    if start_mask is not None:
        mult = jnp.where(start_mask, 1.0, mult)
    return a, mult * (i * c)


def _scan_block(a, b, h):
    sub = lax.broadcasted_iota(jnp.int32, a.shape, 0)
    for d in (1, 2, 4):
        keep = sub >= d
        a_prev = pltpu.roll(a, d, axis=0)
        b_prev = pltpu.roll(b, d, axis=0)
        b = jnp.where(keep, a * b_prev, 0.0) + b
        a = jnp.where(keep, a * a_prev, a)
    return a * h + b


def _scan_rows(a, b, h):
    outs = []
    for k in range(a.shape[0] // SUBLANES):
        blk = slice(k * SUBLANES, (k + 1) * SUBLANES)
        outs.append(_scan_block(a[blk], b[blk], h))
        h = outs[-1][SUBLANES - 1:SUBLANES, :]
    return jnp.concatenate(outs, axis=0), h


def _pool_diff(blk, w, cnt=None):
    xb = blk[HDR:]
    tot, span = blk, 1
    while span < w:
        tot = tot + _shift_rows(tot, span)
        span *= 2
    tot = tot[HDR:]
    pooled = tot * (1.0 / w) if cnt is None else tot / cnt
    return pooled - xb


def _lru_gate_preact(h, cbf, wri_ref, v_ref):
    hs = slice(h * LRU_BLOCK, (h + 1) * LRU_BLOCK)
    ri = _dot(cbf[:, hs], wri_ref[h])
    return ri[:, :LRU_BLOCK] + _vec(v_ref, V_BR, hs), ri[:, LRU_BLOCK:] + _vec(v_ref, V_BI, hs)


def _pool_project(g, d_s, yb_s, pw_ref, v_ref):
    gs = _chunk(g)
    yb_s[:, gs] = (_dot(d_s[:, gs], pw_ref[g]) * _vec(v_ref, V_POOL_SCALE, gs)).astype(bf16)


def _branch_b(c, gbuf, yb_s, wsq, mb_s):
    gate_b = gbuf[GATE_B_CH - GATE_A_CH + c]
    mb_s[:, _chunk(c)] = _sigmoid(gate_b) * _dot(yb_s[...], wsq[SQ_BRANCH_B][:, _chunk(c)])


def _merge(c, gbuf, ya_s, wsq, mb_s, m_s):
    cs = _chunk(c)
    m_s[:, cs] = (_sigmoid(gbuf[c]) * _dot(ya_s[...], wsq[SQ_BRANCH_A][:, cs]) + mb_s[:, cs]).astype(bf16)


def _mixer_step_kernel(segs, tm, x_ref, convp_ref, hp_ref, poolp_ref, v_ref, cw_ref, win_ref, wri_ref, pw_ref,
                       wa_ref, wb_ref, wo_ref,
                       y_ref, convn_ref, hn_ref, pooln_ref, convm_ref, hm_ref, poolm_ref,
                       xn_s, proj, gbuf, cbuf, cbf, a_s, b_s, ya_s, d_s, yb_s, mb_s, m_s):
    n_hist = GATE_A_CH
    wsq = (wa_ref, wb_ref, wo_ref)
    assert sum(1 for seg in segs if not seg[3]) == 1
    for s, (_, _, _, has_state) in enumerate(segs):
        for c in range(n_hist):
            proj[s, c, 0:HDR, :] = jnp.zeros((HDR, NCH), f32)
        if has_state:
            for c in range(POOL_CH):
                proj[s, CONV_CH + c, HDR - (CONV_W - 1):HDR, :] = convp_ref[s, :, _chunk(c)]
                proj[s, POOL_CH + c, HDR - (POOL_MAX - 1):HDR, :] = poolp_ref[s, :, _chunk(c)]

    _norm_to_bf16(x_ref, _vec(v_ref, V_NORM1), xn_s, tm)

    for c in range(N_IN_CH):
        p = _dot(xn_s[...], win_ref[:, _chunk(c)])
        if c < n_hist:
            for s, (row0, length, _, _) in enumerate(segs):
                proj[s, c, HDR:HDR + length, :] = p[row0:row0 + length]
        else:
            gbuf[c - n_hist] = p

    for s, (row0, length, _, has_state) in enumerate(segs):
        n = min(ROW_CH, length)

        def conv_rows(t0, s=s, row0=row0, n=n):
            rows = pl.ds(row0 + t0, n)
            for c in range(POOL_CH):
                acc = _conv_taps(proj[s, CONV_CH + c, pl.ds(t0, n + HDR), :], v_ref, cw_ref, _chunk(c))
                cbuf[rows, _chunk(c)] = acc
                cbf[rows, _chunk(c)] = acc.astype(bf16)
        _for_rows(length, n, conv_rows)
        conv_out, pool_out, k = (convn_ref, pooln_ref, s) if has_state else (convm_ref, poolm_ref, 0)
        for c in range(POOL_CH):
            conv_out[k, :, _chunk(c)] = proj[s, CONV_CH + c, HDR + length - (CONV_W - 1):HDR + length, :]
            pool_out[k, :, _chunk(c)] = proj[s, POOL_CH + c, HDR + length - (POOL_MAX - 1):HDR + length, :]

    for h in range(LRU_HEADS):
        hs = slice(h * LRU_BLOCK, (h + 1) * LRU_BLOCK)
        a_s[:, hs], b_s[:, hs] = _lru_gate_preact(h, cbf, wri_ref, v_ref)
    lam_c = _lam_scaled(v_ref)

    for s, (row0, length, pos0, has_state) in enumerate(segs):
        n = min(ROW_CH, length)

        def mix_rows(t0, h, s=s, row0=row0, n=n, pos0=pos0):
            rows = pl.ds(row0 + t0, n)
            pos = pos0 + t0 + lax.broadcasted_iota(jnp.int32, (n, 1), 0)
            start = (pos == 0) if pos0 == 0 else None
            a, b = _lru_coeffs(a_s[rows, :], b_s[rows, :], cbuf[rows, :], lam_c, start)
            states, h = _scan_rows(a, b, h)
            ya_s[rows, :] = states.astype(bf16)
            cnt = None if pos0 >= POOL_MAX - 1 else pos + 1
            for g, w in enumerate(POOL_WINDOWS):
                cnt_g = None if cnt is None else jnp.minimum(w, cnt).astype(f32)
                d_s[rows, _chunk(g)] = _pool_diff(proj[s, POOL_CH + g, pl.ds(t0, n + HDR), :], w, cnt_g).astype(bf16)
            return h

        steps = length // n
        h = hp_ref[s] if has_state else jnp.zeros((1, D_LRU), f32)
        if steps == 1:
            h = mix_rows(0, h)
        else:
            h = lax.fori_loop(0, steps, lambda k, h: mix_rows(pl.multiple_of(k * n, n), h), h)
        if has_state:
            hn_ref[s] = h
        else:
            hm_ref[0] = h

    for g in range(len(POOL_WINDOWS)):
        _pool_project(g, d_s, yb_s, pw_ref, v_ref)
    for c in range(N_OUT_CH):
        _branch_b(c, gbuf, yb_s, wsq, mb_s)
        _merge(c, gbuf, ya_s, wsq, mb_s, m_s)
    for c in range(N_OUT_CH):
        y_ref[:, _chunk(c)] = x_ref[:, _chunk(c)] + _dot(m_s[...], wsq[SQ_OUT][:, _chunk(c)])


def _layer_stream_kernel(tm, tiles_per_stream, n_tiles, final,
                         x_ref, convp_ref, hp_ref, poolp_ref, v_ref, cw_ref, win_ref, wri_ref, pw_ref,
                         wa_ref, wb_ref, wo_ref, wgu_ref, wd_ref,
                         y_ref, convn_ref, hn_ref, pooln_ref,
                         xn_s, hist, gbuf, cbuf, cbf, ya_s, d_s, yb_s, mb_s, m_s, hcar, ymid, xn2_s, h_s):
    j = pl.program_id(0)
    new = j % 2
    cur = 1 - new
    n_hist = GATE_A_CH
    nc = _norm_chunk(tm)
    wsq = (wa_ref, wb_ref, wo_ref)

    @pl.when(j == 0)
    def _():
        hist[...] = jnp.zeros(hist.shape, f32)
        hcar[...] = jnp.zeros(hcar.shape, f32)
        ymid[1] = jnp.zeros((tm, D_MODEL), f32)

    for c in range(POOL_CH):
        convn_ref[0, :, _chunk(c)] = hist[CONV_CH + c, HDR - (CONV_W - 1):HDR, :]
        pooln_ref[0, :, _chunk(c)] = hist[POOL_CH + c, HDR - (POOL_MAX - 1):HDR, :]
    hn_ref[0] = hcar[0:1, :]

    @pl.when(jnp.logical_and(j % tiles_per_stream == 0, j < n_tiles))
    def _():
        for c in range(POOL_CH):
            hist[CONV_CH + c] = jnp.zeros((HDR, NCH), f32)
            hist[POOL_CH + c] = jnp.zeros((HDR, NCH), f32)
            hist[CONV_CH + c, HDR - (CONV_W - 1):HDR, :] = convp_ref[0, :, _chunk(c)]
            hist[POOL_CH + c, HDR - (POOL_MAX - 1):HDR, :] = poolp_ref[0, :, _chunk(c)]
        hcar[0:1, :] = hp_ref[0]

    lam_c = _lam_scaled(v_ref)

    def ffn_norm():
        for r0 in range(0, tm, nc):
            xn2_s[r0:r0 + nc, :] = _rmsnorm(ymid[cur, r0:r0 + nc, :], _vec(v_ref, V_NORM2)).astype(bf16)

    def ffn_up(c):
        gate = _dot(xn2_s[...], wgu_ref[:, _chunk(c)])
        up = _dot(xn2_s[...], wgu_ref[:, D_FF + c * NCH:D_FF + (c + 1) * NCH])
        h_s[:, _chunk(c)] = (jax.nn.silu(gate) * up).astype(bf16)

    def ffn_down(c):
        y_ref[:, _chunk(c)] = ymid[cur, :, _chunk(c)] + _dot(h_s[...], wd_ref[:, _chunk(c)])

    def ffn_final_norm():
        for r0 in range(0, tm, nc):
            y_ref[r0:r0 + nc, :] = _rmsnorm(y_ref[r0:r0 + nc, :], _vec(v_ref, V_FINAL))

    def mixer_norm():
        for r0 in range(0, tm, nc):
            xn_s[r0:r0 + nc, :] = _rmsnorm(x_ref[r0:r0 + nc, :], _vec(v_ref, V_NORM1)).astype(bf16)

    def project_with_history(c):
        p = _dot(xn_s[...], win_ref[:, _chunk(c)])
        ext = jnp.concatenate([hist[c], p], axis=0)
        hist[c] = p[tm - HDR:]
        return ext

    def conv_chunk(c):
        acc = _conv_taps(project_with_history(CONV_CH + c), v_ref, cw_ref, _chunk(c))
        cbuf[:, _chunk(c)] = acc
        cbf[:, _chunk(c)] = acc.astype(bf16)

    def pool_chunk(g):
        d_s[:, _chunk(g)] = _pool_diff(project_with_history(POOL_CH + g), POOL_WINDOWS[g]).astype(bf16)

    def gate_chunk(c):
        gbuf[c - n_hist] = _dot(xn_s[...], win_ref[:, _chunk(c)])

    def lru_head(h):
        hs = slice(h * LRU_BLOCK, (h + 1) * LRU_BLOCK)
        r_pre, i_pre = _lru_gate_preact(h, cbf, wri_ref, v_ref)
        a, b = _lru_coeffs(r_pre, i_pre, cbuf[:, hs], lam_c[:, hs])
        states, last = _scan_rows(a, b, hcar[0:1, hs])
        ya_s[:, hs] = states.astype(bf16)
        hcar[0:1, hs] = last

    def project_out(c):
        ymid[new, :, _chunk(c)] = x_ref[:, _chunk(c)] + _dot(m_s[...], wsq[SQ_OUT][:, _chunk(c)])

    filler = [functools.partial(ffn_up, c) for c in range(N_FF_CH)]
    filler += [functools.partial(ffn_down, c) for c in range(N_OUT_CH)]
    if final:
        filler.append(ffn_final_norm)
    others = [functools.partial(pool_chunk, g) for g in range(len(POOL_WINDOWS))]
    others += [functools.partial(gate_chunk, GATE_B_CH + c) for c in range(N_OUT_CH)]
    others += [functools.partial(_pool_project, g, d_s, yb_s, pw_ref, v_ref) for g in range(len(POOL_WINDOWS))]
    others += [functools.partial(_branch_b, c, gbuf, yb_s, wsq, mb_s) for c in range(N_OUT_CH)]
    others += [functools.partial(gate_chunk, GATE_A_CH + c) for c in range(N_OUT_CH)]

    stages = [functools.partial(conv_chunk, c) for c in range(POOL_CH)]
    stages += [functools.partial(lru_head, h) for h in range(LRU_HEADS)]
    first_head = POOL_CH

    ffn_norm()
    mixer_norm()
    for k, stage in enumerate(stages):
        stage()
        lo, hi = (k * len(filler)) // len(stages), ((k + 1) * len(filler)) // len(stages)
        for fn in filler[lo:hi]:
            fn()
        if k >= first_head:
            h = k - first_head
            lo, hi = (h * len(others)) // LRU_HEADS, ((h + 1) * len(others)) // LRU_HEADS
            for fn in others[lo:hi]:
                fn()
    for c in range(N_OUT_CH):
        _merge(c, gbuf, ya_s, wsq, mb_s, m_s)
    for c in range(N_OUT_CH):
        project_out(c)


def _ffn_kernel(final, tm, x_ref, v_ref, wgu_ref, wd_ref, y_ref, xn_s, h_s):
    nc = _norm_chunk(tm)
    _norm_to_bf16(x_ref, _vec(v_ref, V_NORM2), xn_s, tm)

    for c in range(N_FF_CH):
        gate = _dot(xn_s[...], wgu_ref[:, _chunk(c)])
        up = _dot(xn_s[...], wgu_ref[:, D_FF + c * NCH:D_FF + (c + 1) * NCH])
        h_s[:, _chunk(c)] = (jax.nn.silu(gate) * up).astype(bf16)

    for c in range(N_OUT_CH):
        y_ref[:, _chunk(c)] = x_ref[:, _chunk(c)] + _dot(h_s[...], wd_ref[:, _chunk(c)])

    if final:
        def final_rows(r0):
            rows = pl.ds(r0, nc)
            y_ref[rows, :] = _rmsnorm(y_ref[rows, :], _vec(v_ref, V_FINAL))
        _for_rows(tm, nc, final_rows)


def _layer_spec(w, layer):
    nd = w.ndim - 1
    return pl.BlockSpec((None,) + w.shape[1:], lambda *_: (layer,) + (0,) * nd, pipeline_mode=pl.Buffered(1))


def _mixer_weights(w):
    return (w['vec'], w['conv_w'], w['w_in'], w['wri'], w['pool_w'], w['w_br_a'], w['w_br_b'], w['w_out'])


def _state_shapes(n):
    return (jax.ShapeDtypeStruct((n, CONV_W - 1, D_LRU), f32),
            jax.ShapeDtypeStruct((n, 1, D_LRU), f32),
            jax.ShapeDtypeStruct((n, POOL_MAX - 1, D_POOL), f32))


def _state_specs(n, index_map):
    return [pl.BlockSpec((n, CONV_W - 1, D_LRU), index_map), pl.BlockSpec((n, 1, D_LRU), index_map),
            pl.BlockSpec((n, POOL_MAX - 1, D_POOL), index_map)]


def _mixer_step_call(x, state_conv, state_lru, state_pool, w, layer, segs, name):
    tm = x.shape[0]
    ns = state_conv.shape[1]
    max_len = max(seg[1] for seg in segs)
    weights = _mixer_weights(w)
    x_spec = pl.BlockSpec((tm, D_MODEL), lambda i: (0, 0))
    whole = lambda i: (0, 0, 0)
    state_in = [pl.BlockSpec((None,) + s.shape[1:], lambda i: (layer, 0, 0, 0))
                for s in (state_conv, state_lru, state_pool)]
    scratch = [
        pltpu.VMEM((tm, D_MODEL), bf16),
        pltpu.VMEM((len(segs), GATE_A_CH, HDR + max_len, NCH), f32),
        pltpu.VMEM((N_IN_CH - GATE_A_CH, tm, NCH), f32),
        pltpu.VMEM((tm, D_LRU), f32),
        pltpu.VMEM((tm, D_LRU), bf16),
        pltpu.VMEM((tm, D_LRU), f32),
        pltpu.VMEM((tm, D_LRU), f32),
        pltpu.VMEM((tm, D_LRU), bf16),
        pltpu.VMEM((tm, D_POOL), bf16),
        pltpu.VMEM((tm, D_POOL), bf16),
        pltpu.VMEM((tm, D_MODEL), f32),
        pltpu.VMEM((tm, D_MODEL), bf16),
    ]
    return pl.pallas_call(
        functools.partial(_mixer_step_kernel, tuple(segs), tm),
        grid=(1,),
        in_specs=[x_spec] + state_in + [_layer_spec(v, layer) for v in weights],
        out_specs=[x_spec] + _state_specs(ns, whole) + _state_specs(1, whole),
        out_shape=(jax.ShapeDtypeStruct(x.shape, f32),) + _state_shapes(ns) + _state_shapes(1),
        scratch_shapes=scratch,
        compiler_params=pltpu.CompilerParams(dimension_semantics=("arbitrary",), vmem_limit_bytes=VMEM_LIMIT),
        name=name,
    )(x, state_conv, state_lru, state_pool, *weights)


def _layer_stream_call(x, conv_init, h_init, pool_init, w, layer, final, tm, name):
    nb, t, _ = x.shape
    tps = t // tm
    n_tiles = nb * tps
    assert tps * tm == t
    weights = _mixer_weights(w) + (w['w_gu'], w['w_down'])
    x_tiles = x.reshape(n_tiles, tm, D_MODEL)
    x_spec = pl.BlockSpec((None, tm, D_MODEL), lambda j: (jnp.minimum(j, n_tiles - 1), 0, 0))
    y_spec = pl.BlockSpec((None, tm, D_MODEL), lambda j: (jnp.maximum(j - 1, 0), 0, 0))
    scratch = [
        pltpu.VMEM((tm, D_MODEL), bf16),
        pltpu.VMEM((GATE_A_CH, HDR, NCH), f32),
        pltpu.VMEM((N_IN_CH - GATE_A_CH, tm, NCH), f32),
        pltpu.VMEM((tm, D_LRU), f32),
        pltpu.VMEM((tm, D_LRU), bf16),
        pltpu.VMEM((tm, D_LRU), bf16),
        pltpu.VMEM((tm, D_POOL), bf16),
        pltpu.VMEM((tm, D_POOL), bf16),
        pltpu.VMEM((tm, D_MODEL), f32),
        pltpu.VMEM((tm, D_MODEL), bf16),
        pltpu.VMEM((SUBLANES, D_LRU), f32),
        pltpu.VMEM((2, tm, D_MODEL), f32),
        pltpu.VMEM((tm, D_MODEL), bf16),
        pltpu.VMEM((tm, D_FF), bf16),
    ]
    y, c_n, h_n, p_n = pl.pallas_call(
        functools.partial(_layer_stream_kernel, tm, tps, n_tiles, final),
        grid=(n_tiles + 1,),
        in_specs=[x_spec] + _state_specs(1, lambda j: (0, 0, 0)) + [_layer_spec(v, layer) for v in weights],
        out_specs=[y_spec] + _state_specs(1, lambda j: (jnp.maximum(j - 1, 0) // tps, 0, 0)),
        out_shape=(jax.ShapeDtypeStruct(x_tiles.shape, f32),) + _state_shapes(nb),
        scratch_shapes=scratch,
        compiler_params=pltpu.CompilerParams(dimension_semantics=("arbitrary",), vmem_limit_bytes=VMEM_LIMIT),
        name=name,
    )(x_tiles, conv_init, h_init, pool_init, *weights)
    return y.reshape(x.shape), c_n, h_n, p_n


def _ffn_call(x, w, layer, final, rows, name):
    tm = rows
    x_spec = pl.BlockSpec((tm, D_MODEL), lambda i: (0, 0))
    weights = (w['vec'], w['w_gu'], w['w_down'])
    return pl.pallas_call(
        functools.partial(_ffn_kernel, final, tm),
        grid=(1,),
        in_specs=[x_spec] + [_layer_spec(v, layer) for v in weights],
        out_specs=x_spec,
        out_shape=jax.ShapeDtypeStruct((tm, D_MODEL), f32),
        scratch_shapes=[pltpu.VMEM((tm, D_MODEL), bf16), pltpu.VMEM((tm, D_FF), bf16)],
        compiler_params=pltpu.CompilerParams(dimension_semantics=("arbitrary",), vmem_limit_bytes=VMEM_LIMIT),
        name=name,
    )(x, *weights)


def kernel(x_prompt, x_sample, state_conv, state_lru, state_pool, meta_tokens, norm1_g, w_in, conv_w, conv_b,
           lam, w_r, b_r, w_i, b_i, pool_w, pool_scale, w_br_a, w_br_b, w_out, norm2_g, w_gu, w_down, final_g):
    depth = w_in.shape[0]
    ns, ts, _ = x_sample.shape
    vec_rows = [None] * VEC_ROWS
    for row, v in ((V_NORM1, norm1_g), (V_CONV_B, conv_b), (V_BR, b_r), (V_BI, b_i), (V_LAM, lam),
                   (V_POOL_SCALE, pool_scale), (V_NORM2, norm2_g),
                   (V_FINAL, jnp.broadcast_to(final_g, (depth, D_MODEL)))):
        vec_rows[row] = v.astype(f32)
    w = {
        'vec': jnp.stack(vec_rows, axis=1), 'conv_w': conv_w.astype(f32),
        'w_in': w_in.astype(bf16),
        'wri': jnp.concatenate([w_r, w_i], axis=-1).astype(bf16),
        'pool_w': pool_w.astype(bf16),
        'w_br_a': w_br_a.astype(bf16), 'w_br_b': w_br_b.astype(bf16), 'w_out': w_out.astype(bf16),
        'w_gu': w_gu.astype(bf16), 'w_down': w_down.astype(bf16),
    }

    segs_s = [(s * ts, ts, PAST_LEN, True) for s in range(ns)] + [(ns * ts, N_META, 0, False)]
    xs = jnp.concatenate([x_sample.reshape(ns * ts, D_MODEL), meta_tokens.astype(f32)], axis=0)
    xp = x_prompt
    state_lru = state_lru[:, :, None, :]

    outs = {k: [] for k in ('cp', 'hp', 'pp', 'cs', 'hs', 'ps')}
    for l in range(depth):
        last = l == depth - 1
        xs, c_s, h_s, p_s, c_m, h_m, p_m = _mixer_step_call(xs, state_conv, state_lru, state_pool, w, l, segs_s,
                                                            f"mixer_step_l{l}")
        xs = _ffn_call(xs, w, l, last, ns * ts if last else xs.shape[0], f"ffn_step_l{l}")
        xp, c_p, h_p, p_p = _layer_stream_call(xp, c_m, h_m, p_m, w, l, last, TM_LAYER, f"layer_prompt_l{l}")

        outs['cp'].append(c_p); outs['hp'].append(h_p[:, 0]); outs['pp'].append(p_p)
        outs['cs'].append(c_s); outs['hs'].append(h_s[:, 0]); outs['ps'].append(p_s)

    y_sample = xs.reshape(ns, ts, D_MODEL)
    return (xp, y_sample, jnp.stack(outs['cp']), jnp.stack(outs['hp']), jnp.stack(outs['pp']),
            jnp.stack(outs['cs']), jnp.stack(outs['hs']), jnp.stack(outs['ps']))
```

```python
import functools

import jax
import jax.numpy as jnp
from jax import lax
from jax.experimental import pallas as pl
from jax.experimental.pallas import tpu as pltpu

D_MODEL = 1024
D_LRU = 1024
D_POOL = 1024
LRU_HEADS = 8
LRU_BLOCK = D_LRU // LRU_HEADS
LRU_C = 8.0
CONV_W = 4
POOL_WINDOWS = (2, 4, 8, 16)
POOL_GROUP = D_POOL // len(POOL_WINDOWS)
POOL_MAX = 16
D_FF = 2816
N_META = 16
PAST_LEN = 2048
EPS = 1e-6

SUBLANES = 8
BF16_ROWS = 16
HDR = 16
NCH = 256
N_IN_CH = 4 * D_MODEL // NCH
CONV_CH = 0
POOL_CH = D_LRU // NCH
GATE_A_CH = (D_LRU + D_POOL) // NCH
GATE_B_CH = (D_LRU + D_POOL + D_MODEL) // NCH
N_OUT_CH = D_MODEL // NCH
N_FF_CH = D_FF // NCH
TM_LAYER = 256
ROW_CH = 32
NORM_ROWS_MAX = 128
V7X_VMEM_BYTES = 64 * 1024 * 1024
VMEM_LIMIT = V7X_VMEM_BYTES - 4 * 1024 * 1024

V_NORM1, V_CONV_B, V_BR, V_BI, V_LAM, V_POOL_SCALE, V_NORM2, V_FINAL = range(8)
VEC_ROWS = 8
SQ_BRANCH_A, SQ_BRANCH_B, SQ_OUT = range(3)

f32 = jnp.float32
bf16 = jnp.bfloat16

assert POOL_GROUP == NCH and HDR >= POOL_MAX - 1 and HDR % BF16_ROWS == 0
assert D_LRU == D_POOL == D_MODEL


def _dot(a, b):
    return jnp.dot(a, b, preferred_element_type=f32)


def _chunk(c):
    return slice(c * NCH, (c + 1) * NCH)


def _vec(v_ref, row, cols=slice(None)):
    return v_ref[row:row + 1, cols]


def _rmsnorm(x, g):
    y = x * lax.rsqrt(jnp.mean(x * x, axis=-1, keepdims=True) + EPS)
    return y * g


def _norm_chunk(tm):
    return max(c for c in range(BF16_ROWS, NORM_ROWS_MAX + 1, BF16_ROWS) if tm % c == 0)


def _sigmoid(x):
    return 0.5 * jnp.tanh(0.5 * x) + 0.5


def _shift_rows(x, k):
    return pltpu.roll(x, k, axis=0)


def _for_rows(total, chunk, fn):
    n = total // chunk
    assert n * chunk == total
    if n == 1:
        fn(0)
    else:
        def body(k, c):
            fn(pl.multiple_of(k * chunk, chunk))
            return c
        lax.fori_loop(0, n, body, 0)


def _norm_to_bf16(x_ref, g, xn_s, tm):
    nc = _norm_chunk(tm)

    def norm_rows(r0):
        rows = pl.ds(r0, nc)
        xn_s[rows, :] = _rmsnorm(x_ref[rows, :], g).astype(bf16)
    _for_rows(tm, nc, norm_rows)


def _conv_taps(blk, v_ref, cw_ref, cs):
    tap = lambda k: _shift_rows(blk, CONV_W - 1 - k)[HDR:] if k < CONV_W - 1 else blk[HDR:]
    acc = _vec(v_ref, V_CONV_B, cs) + cw_ref[0:1, cs] * tap(0)
    for k in range(1, CONV_W):
        acc = acc + cw_ref[k:k + 1, cs] * tap(k)
    return acc


def _lam_scaled(v_ref):
    lam = _vec(v_ref, V_LAM)
    log_sig_lam = jnp.minimum(lam, 0.0) - jnp.log1p(jnp.exp(-jnp.abs(lam)))
    return LRU_C * log_sig_lam


def _lru_coeffs(r_pre, i_pre, c, lam_c, start_mask=None):
    r = _sigmoid(r_pre)
    i = _sigmoid(i_pre)
    log_a = r * lam_c
    a = jnp.exp(log_a)
    th = jnp.tanh(log_a)
    mult = jnp.sqrt(-2.0 * th / (1.0 - th))
    if start_mask is not None:
        mult = jnp.where(start_mask, 1.0, mult)
    return a, mult * (i * c)


def _scan_block(a, b, h):
    sub = lax.broadcasted_iota(jnp.int32, a.shape, 0)
    for d in (1, 2, 4):
        keep = sub >= d
        a_prev = pltpu.roll(a, d, axis=0)
        b_prev = pltpu.roll(b, d, axis=0)
        b = jnp.where(keep, a * b_prev, 0.0) + b
        a = jnp.where(keep, a * a_prev, a)
    return a * h + b


def _scan_rows(a, b, h):
    outs = []
    for k in range(a.shape[0] // SUBLANES):
        blk = slice(k * SUBLANES, (k + 1) * SUBLANES)
        outs.append(_scan_block(a[blk], b[blk], h))
        h = outs[-1][SUBLANES - 1:SUBLANES, :]
    return jnp.concatenate(outs, axis=0), h


def _pool_diff(blk, w, cnt=None):
    xb = blk[HDR:]
    tot, span = blk, 1
    while span < w:
        tot = tot + _shift_rows(tot, span)
        span *= 2
    tot = tot[HDR:]
    pooled = tot * (1.0 / w) if cnt is None else tot / cnt
    return pooled - xb


def _lru_gate_preact(h, cbf, wri_ref, v_ref):
    hs = slice(h * LRU_BLOCK, (h + 1) * LRU_BLOCK)
    ri = _dot(cbf[:, hs], wri_ref[h])
    return ri[:, :LRU_BLOCK] + _vec(v_ref, V_BR, hs), ri[:, LRU_BLOCK:] + _vec(v_ref, V_BI, hs)


def _pool_project(g, d_s, yb_s, pw_ref, v_ref):
    gs = _chunk(g)
    yb_s[:, gs] = (_dot(d_s[:, gs], pw_ref[g]) * _vec(v_ref, V_POOL_SCALE, gs)).astype(bf16)


def _branch_b(c, gbuf, yb_s, wsq, mb_s):
    gate_b = gbuf[GATE_B_CH - GATE_A_CH + c]
    mb_s[:, _chunk(c)] = _sigmoid(gate_b) * _dot(yb_s[...], wsq[SQ_BRANCH_B][:, _chunk(c)])


def _merge(c, gbuf, ya_s, wsq, mb_s, m_s):
    cs = _chunk(c)
    m_s[:, cs] = (_sigmoid(gbuf[c]) * _dot(ya_s[...], wsq[SQ_BRANCH_A][:, cs]) + mb_s[:, cs]).astype(bf16)


def _mixer_step_kernel(segs, tm, x_ref, convp_ref, hp_ref, poolp_ref, v_ref, cw_ref, win_f32_ref, wri_ref, pw_ref,
                       wa_ref, wb_ref, wo_ref,
                       y_ref, convn_ref, hn_ref, pooln_ref, convm_ref, hm_ref, poolm_ref, win_out,
                       xn_s, proj, gbuf, cbuf, cbf, a_s, b_s, ya_s, d_s, yb_s, mb_s, m_s):
    n_hist = GATE_A_CH
    wsq = (wa_ref, wb_ref, wo_ref)
    assert sum(1 for seg in segs if not seg[3]) == 1
    for s, (_, _, _, has_state) in enumerate(segs):
        for c in range(n_hist):
            proj[s, c, 0:HDR, :] = jnp.zeros((HDR, NCH), f32)
        if has_state:
            for c in range(POOL_CH):
                proj[s, CONV_CH + c, HDR - (CONV_W - 1):HDR, :] = convp_ref[s, :, _chunk(c)]
                proj[s, POOL_CH + c, HDR - (POOL_MAX - 1):HDR, :] = poolp_ref[s, :, _chunk(c)]

    _norm_to_bf16(x_ref, _vec(v_ref, V_NORM1), xn_s, tm)

    for c in range(N_IN_CH):
        w_chunk = win_f32_ref[:, _chunk(c)].astype(bf16)
        win_out[:, _chunk(c)] = w_chunk
        p = _dot(xn_s[...], w_chunk)
        if c < n_hist:
            for s, (row0, length, _, _) in enumerate(segs):
                proj[s, c, HDR:HDR + length, :] = p[row0:row0 + length]
        else:
            gbuf[c - n_hist] = p

    for s, (row0, length, _, has_state) in enumerate(segs):
        n = min(ROW_CH, length)

        def conv_rows(t0, s=s, row0=row0, n=n):
            rows = pl.ds(row0 + t0, n)
            for c in range(POOL_CH):
                acc = _conv_taps(proj[s, CONV_CH + c, pl.ds(t0, n + HDR), :], v_ref, cw_ref, _chunk(c))
                cbuf[rows, _chunk(c)] = acc
                cbf[rows, _chunk(c)] = acc.astype(bf16)
        _for_rows(length, n, conv_rows)
        conv_out, pool_out, k = (convn_ref, pooln_ref, s) if has_state else (convm_ref, poolm_ref, 0)
        for c in range(POOL_CH):
            conv_out[k, :, _chunk(c)] = proj[s, CONV_CH + c, HDR + length - (CONV_W - 1):HDR + length, :]
            pool_out[k, :, _chunk(c)] = proj[s, POOL_CH + c, HDR + length - (POOL_MAX - 1):HDR + length, :]

    for h in range(LRU_HEADS):
        hs = slice(h * LRU_BLOCK, (h + 1) * LRU_BLOCK)
        a_s[:, hs], b_s[:, hs] = _lru_gate_preact(h, cbf, wri_ref, v_ref)
    lam_c = _lam_scaled(v_ref)

    for s, (row0, length, pos0, has_state) in enumerate(segs):
        n = min(ROW_CH, length)

        def mix_rows(t0, h, s=s, row0=row0, n=n, pos0=pos0):
            rows = pl.ds(row0 + t0, n)
            pos = pos0 + t0 + lax.broadcasted_iota(jnp.int32, (n, 1), 0)
            start = (pos == 0) if pos0 == 0 else None
            a, b = _lru_coeffs(a_s[rows, :], b_s[rows, :], cbuf[rows, :], lam_c, start)
            states, h = _scan_rows(a, b, h)
            ya_s[rows, :] = states.astype(bf16)
            cnt = None if pos0 >= POOL_MAX - 1 else pos + 1
            for g, w in enumerate(POOL_WINDOWS):
                cnt_g = None if cnt is None else jnp.minimum(w, cnt).astype(f32)
                d_s[rows, _chunk(g)] = _pool_diff(proj[s, POOL_CH + g, pl.ds(t0, n + HDR), :], w, cnt_g).astype(bf16)
            return h

        steps = length // n
        h = hp_ref[s] if has_state else jnp.zeros((1, D_LRU), f32)
        if steps == 1:
            h = mix_rows(0, h)
        else:
            h = lax.fori_loop(0, steps, lambda k, h: mix_rows(pl.multiple_of(k * n, n), h), h)
        if has_state:
            hn_ref[s] = h
        else:
            hm_ref[0] = h

    for g in range(len(POOL_WINDOWS)):
        _pool_project(g, d_s, yb_s, pw_ref, v_ref)
    for c in range(N_OUT_CH):
        _branch_b(c, gbuf, yb_s, wsq, mb_s)
        _merge(c, gbuf, ya_s, wsq, mb_s, m_s)
    for c in range(N_OUT_CH):
        y_ref[:, _chunk(c)] = x_ref[:, _chunk(c)] + _dot(m_s[...], wsq[SQ_OUT][:, _chunk(c)])


def _layer_stream_kernel(tm, tiles_per_stream, n_tiles, final,
                         x_ref, convp_ref, hp_ref, poolp_ref, v_ref, cw_ref, win_ref, wri_ref, pw_ref,
                         wa_ref, wb_ref, wo_ref, wg_ref, wu_ref, wd_ref,
                         y_ref, convn_ref, hn_ref, pooln_ref,
                         xn_s, hist, gbuf, cbuf, cbf, ya_s, d_s, yb_s, mb_s, m_s, hcar, ymid, xn2_s, h_s):
    j = pl.program_id(0)
    new = j % 2
    cur = 1 - new
    n_hist = GATE_A_CH
    nc = _norm_chunk(tm)
    wsq = (wa_ref, wb_ref, wo_ref)

    @pl.when(j == 0)
    def _():
        hist[...] = jnp.zeros(hist.shape, f32)
        hcar[...] = jnp.zeros(hcar.shape, f32)
        ymid[1] = jnp.zeros((tm, D_MODEL), f32)

    for c in range(POOL_CH):
        convn_ref[0, :, _chunk(c)] = hist[CONV_CH + c, HDR - (CONV_W - 1):HDR, :]
        pooln_ref[0, :, _chunk(c)] = hist[POOL_CH + c, HDR - (POOL_MAX - 1):HDR, :]
    hn_ref[0] = hcar[0:1, :]

    @pl.when(jnp.logical_and(j % tiles_per_stream == 0, j < n_tiles))
    def _():
        for c in range(POOL_CH):
            hist[CONV_CH + c] = jnp.zeros((HDR, NCH), f32)
            hist[POOL_CH + c] = jnp.zeros((HDR, NCH), f32)
            hist[CONV_CH + c, HDR - (CONV_W - 1):HDR, :] = convp_ref[0, :, _chunk(c)]
            hist[POOL_CH + c, HDR - (POOL_MAX - 1):HDR, :] = poolp_ref[0, :, _chunk(c)]
        hcar[0:1, :] = hp_ref[0]

    lam_c = _lam_scaled(v_ref)

    def ffn_norm():
        for r0 in range(0, tm, nc):
            xn2_s[r0:r0 + nc, :] = _rmsnorm(ymid[cur, r0:r0 + nc, :], _vec(v_ref, V_NORM2)).astype(bf16)

    def ffn_up(c):
        gate = _dot(xn2_s[...], wg_ref[:, _chunk(c)])
        up = _dot(xn2_s[...], wu_ref[:, _chunk(c)])
        h_s[:, _chunk(c)] = (jax.nn.silu(gate) * up).astype(bf16)

    def ffn_down(c):
        y_ref[:, _chunk(c)] = ymid[cur, :, _chunk(c)] + _dot(h_s[...], wd_ref[:, _chunk(c)])

    def ffn_final_norm():
        for r0 in range(0, tm, nc):
            y_ref[r0:r0 + nc, :] = _rmsnorm(y_ref[r0:r0 + nc, :], _vec(v_ref, V_FINAL))

    def mixer_norm():
        for r0 in range(0, tm, nc):
            xn_s[r0:r0 + nc, :] = _rmsnorm(x_ref[r0:r0 + nc, :], _vec(v_ref, V_NORM1)).astype(bf16)

    def project_with_history(c):
        p = _dot(xn_s[...], win_ref[:, _chunk(c)])
        ext = jnp.concatenate([hist[c], p], axis=0)
        hist[c] = p[tm - HDR:]
        return ext

    def conv_chunk(c):
        acc = _conv_taps(project_with_history(CONV_CH + c), v_ref, cw_ref, _chunk(c))
        cbuf[:, _chunk(c)] = acc
        cbf[:, _chunk(c)] = acc.astype(bf16)

    def pool_chunk(g):
        d_s[:, _chunk(g)] = _pool_diff(project_with_history(POOL_CH + g), POOL_WINDOWS[g]).astype(bf16)

    def gate_chunk(c):
        gbuf[c - n_hist] = _dot(xn_s[...], win_ref[:, _chunk(c)])

    def lru_head(h):
        hs = slice(h * LRU_BLOCK, (h + 1) * LRU_BLOCK)
        r_pre, i_pre = _lru_gate_preact(h, cbf, wri_ref, v_ref)
        a, b = _lru_coeffs(r_pre, i_pre, cbuf[:, hs], lam_c[:, hs])
        states, last = _scan_rows(a, b, hcar[0:1, hs])
        ya_s[:, hs] = states.astype(bf16)
        hcar[0:1, hs] = last

    def project_out(c):
        ymid[new, :, _chunk(c)] = x_ref[:, _chunk(c)] + _dot(m_s[...], wsq[SQ_OUT][:, _chunk(c)])

    filler = [functools.partial(ffn_up, c) for c in range(N_FF_CH)]
    filler += [functools.partial(ffn_down, c) for c in range(N_OUT_CH)]
    if final:
        filler.append(ffn_final_norm)
    others = [functools.partial(pool_chunk, g) for g in range(len(POOL_WINDOWS))]
    others += [functools.partial(gate_chunk, GATE_B_CH + c) for c in range(N_OUT_CH)]
    others += [functools.partial(_pool_project, g, d_s, yb_s, pw_ref, v_ref) for g in range(len(POOL_WINDOWS))]
    others += [functools.partial(_branch_b, c, gbuf, yb_s, wsq, mb_s) for c in range(N_OUT_CH)]
    others += [functools.partial(gate_chunk, GATE_A_CH + c) for c in range(N_OUT_CH)]

    stages = [functools.partial(conv_chunk, c) for c in range(POOL_CH)]
    stages += [functools.partial(lru_head, h) for h in range(LRU_HEADS)]
    first_head = POOL_CH

    ffn_norm()
    mixer_norm()
    for k, stage in enumerate(stages):
        stage()
        lo, hi = (k * len(filler)) // len(stages), ((k + 1) * len(filler)) // len(stages)
        for fn in filler[lo:hi]:
            fn()
        if k >= first_head:
            h = k - first_head
            lo, hi = (h * len(others)) // LRU_HEADS, ((h + 1) * len(others)) // LRU_HEADS
            for fn in others[lo:hi]:
                fn()
    for c in range(N_OUT_CH):
        _merge(c, gbuf, ya_s, wsq, mb_s, m_s)
    for c in range(N_OUT_CH):
        project_out(c)


def _ffn_cast_kernel(final, tm, x_ref, v_ref, wg_ref, wu_ref, wd_ref, y_ref, wg_out, wu_out, wd_out, xn_s, acc_s):
    c = pl.program_id(0)
    nc = _norm_chunk(tm)

    @pl.when(c == 0)
    def _():
        _norm_to_bf16(x_ref, _vec(v_ref, V_NORM2), xn_s, tm)
        acc_s[...] = x_ref[...]

    wg = wg_ref[...].astype(bf16)
    wu = wu_ref[...].astype(bf16)
    wd = wd_ref[...].astype(bf16)
    wg_out[...] = wg
    wu_out[...] = wu
    wd_out[...] = wd
    hidden = (jax.nn.silu(_dot(xn_s[...], wg)) * _dot(xn_s[...], wu)).astype(bf16)
    acc_s[...] += _dot(hidden, wd)

    @pl.when(c == pl.num_programs(0) - 1)
    def _():
        if final:
            def final_rows(r0):
                rows = pl.ds(r0, nc)
                y_ref[rows, :] = _rmsnorm(acc_s[rows, :], _vec(v_ref, V_FINAL))
            _for_rows(tm, nc, final_rows)
        else:
            y_ref[...] = acc_s[...]


def _layer_spec(w, layer):
    nd = w.ndim - 1
    return pl.BlockSpec((None,) + w.shape[1:], lambda *_: (layer,) + (0,) * nd, pipeline_mode=pl.Buffered(1))


def _mixer_weights(w):
    return (w['vec'], w['conv_w'], w['w_in'], w['wri'], w['pool_w'], w['w_br_a'], w['w_br_b'], w['w_out'])


def _state_shapes(n):
    return (jax.ShapeDtypeStruct((n, CONV_W - 1, D_LRU), f32),
            jax.ShapeDtypeStruct((n, 1, D_LRU), f32),
            jax.ShapeDtypeStruct((n, POOL_MAX - 1, D_POOL), f32))


def _state_specs(n, index_map):
    return [pl.BlockSpec((n, CONV_W - 1, D_LRU), index_map), pl.BlockSpec((n, 1, D_LRU), index_map),
            pl.BlockSpec((n, POOL_MAX - 1, D_POOL), index_map)]


def _mixer_step_call(x, state_conv, state_lru, state_pool, w, layer, segs, name):
    tm = x.shape[0]
    ns = state_conv.shape[1]
    max_len = max(seg[1] for seg in segs)
    weights = _mixer_weights(w)
    x_spec = pl.BlockSpec((tm, D_MODEL), lambda i: (0, 0))
    whole = lambda i: (0, 0, 0)
    state_in = [pl.BlockSpec((None,) + s.shape[1:], lambda i: (layer, 0, 0, 0))
                for s in (state_conv, state_lru, state_pool)]
    scratch = [
        pltpu.VMEM((tm, D_MODEL), bf16),
        pltpu.VMEM((len(segs), GATE_A_CH, HDR + max_len, NCH), f32),
        pltpu.VMEM((N_IN_CH - GATE_A_CH, tm, NCH), f32),
        pltpu.VMEM((tm, D_LRU), f32),
        pltpu.VMEM((tm, D_LRU), bf16),
        pltpu.VMEM((tm, D_LRU), f32),
        pltpu.VMEM((tm, D_LRU), f32),
        pltpu.VMEM((tm, D_LRU), bf16),
        pltpu.VMEM((tm, D_POOL), bf16),
        pltpu.VMEM((tm, D_POOL), bf16),
        pltpu.VMEM((tm, D_MODEL), f32),
        pltpu.VMEM((tm, D_MODEL), bf16),
    ]
    return pl.pallas_call(
        functools.partial(_mixer_step_kernel, tuple(segs), tm),
        grid=(1,),
        in_specs=[x_spec] + state_in + [_layer_spec(v, layer) for v in weights],
        out_specs=([x_spec] + _state_specs(ns, whole) + _state_specs(1, whole)
                   + [pl.BlockSpec(w['w_in'].shape[1:], lambda i: (0, 0))]),
        out_shape=((jax.ShapeDtypeStruct(x.shape, f32),) + _state_shapes(ns) + _state_shapes(1)
                   + (jax.ShapeDtypeStruct(w['w_in'].shape[1:], bf16),)),
        scratch_shapes=scratch,
        compiler_params=pltpu.CompilerParams(dimension_semantics=("arbitrary",), vmem_limit_bytes=VMEM_LIMIT),
        name=name,
    )(x, state_conv, state_lru, state_pool, *weights)


def _whole_spec(w):
    nd = w.ndim
    return pl.BlockSpec(w.shape, lambda *_: (0,) * nd, pipeline_mode=pl.Buffered(1))


def _layer_stream_call(x, conv_init, h_init, pool_init, w, w_in, ffn_weights, layer, final, tm, name):
    nb, t, _ = x.shape
    tps = t // tm
    n_tiles = nb * tps
    assert tps * tm == t
    weights = tuple(w_in if v is w['w_in'] else v for v in _mixer_weights(w))
    weight_specs = [_whole_spec(v) if v is w_in else _layer_spec(v, layer) for v in weights]
    x_tiles = x.reshape(n_tiles, tm, D_MODEL)
    x_spec = pl.BlockSpec((None, tm, D_MODEL), lambda j: (jnp.minimum(j, n_tiles - 1), 0, 0))
    y_spec = pl.BlockSpec((None, tm, D_MODEL), lambda j: (jnp.maximum(j - 1, 0), 0, 0))
    scratch = [
        pltpu.VMEM((tm, D_MODEL), bf16),
        pltpu.VMEM((GATE_A_CH, HDR, NCH), f32),
        pltpu.VMEM((N_IN_CH - GATE_A_CH, tm, NCH), f32),
        pltpu.VMEM((tm, D_LRU), f32),
        pltpu.VMEM((tm, D_LRU), bf16),
        pltpu.VMEM((tm, D_LRU), bf16),
        pltpu.VMEM((tm, D_POOL), bf16),
        pltpu.VMEM((tm, D_POOL), bf16),
        pltpu.VMEM((tm, D_MODEL), f32),
        pltpu.VMEM((tm, D_MODEL), bf16),
        pltpu.VMEM((SUBLANES, D_LRU), f32),
        pltpu.VMEM((2, tm, D_MODEL), f32),
        pltpu.VMEM((tm, D_MODEL), bf16),
        pltpu.VMEM((tm, D_FF), bf16),
    ]
    y, c_n, h_n, p_n = pl.pallas_call(
        functools.partial(_layer_stream_kernel, tm, tps, n_tiles, final),
        grid=(n_tiles + 1,),
        in_specs=([x_spec] + _state_specs(1, lambda j: (0, 0, 0)) + weight_specs
                  + [_whole_spec(v) for v in ffn_weights]),
        out_specs=[y_spec] + _state_specs(1, lambda j: (jnp.maximum(j - 1, 0) // tps, 0, 0)),
        out_shape=(jax.ShapeDtypeStruct(x_tiles.shape, f32),) + _state_shapes(nb),
        scratch_shapes=scratch,
        compiler_params=pltpu.CompilerParams(dimension_semantics=("arbitrary",), vmem_limit_bytes=VMEM_LIMIT),
        name=name,
    )(x_tiles, conv_init, h_init, pool_init, *weights, *ffn_weights)
    return y.reshape(x.shape), c_n, h_n, p_n


def _ffn_cast_call(x, vec, w_gu, w_down, layer, final, rows, name):
    tm = rows
    x_spec = pl.BlockSpec((tm, D_MODEL), lambda c: (0, 0))
    in_specs = [x_spec, _layer_spec(vec, layer),
                pl.BlockSpec((None, D_MODEL, NCH), lambda c: (layer, 0, c)),
                pl.BlockSpec((None, D_MODEL, NCH), lambda c: (layer, 0, N_FF_CH + c)),
                pl.BlockSpec((None, NCH, D_MODEL), lambda c: (layer, c, 0))]
    out_specs = [x_spec, pl.BlockSpec((D_MODEL, NCH), lambda c: (0, c)), pl.BlockSpec((D_MODEL, NCH), lambda c: (0, c)),
                 pl.BlockSpec((NCH, D_MODEL), lambda c: (c, 0))]
    out_shape = (jax.ShapeDtypeStruct((tm, D_MODEL), f32), jax.ShapeDtypeStruct((D_MODEL, D_FF), bf16),
                 jax.ShapeDtypeStruct((D_MODEL, D_FF), bf16), jax.ShapeDtypeStruct((D_FF, D_MODEL), bf16))
    return pl.pallas_call(
        functools.partial(_ffn_cast_kernel, final, tm),
        grid=(N_FF_CH,),
        in_specs=in_specs,
        out_specs=out_specs,
        out_shape=out_shape,
        scratch_shapes=[pltpu.VMEM((tm, D_MODEL), bf16), pltpu.VMEM((tm, D_MODEL), f32)],
        compiler_params=pltpu.CompilerParams(dimension_semantics=("arbitrary",), vmem_limit_bytes=VMEM_LIMIT),
        name=name,
    )(x, vec, w_gu, w_gu, w_down)


def kernel(x_prompt, x_sample, state_conv, state_lru, state_pool, meta_tokens, norm1_g, w_in, conv_w, conv_b,
           lam, w_r, b_r, w_i, b_i, pool_w, pool_scale, w_br_a, w_br_b, w_out, norm2_g, w_gu, w_down, final_g):
    depth = w_in.shape[0]
    ns, ts, _ = x_sample.shape
    vec_rows = [None] * VEC_ROWS
    for row, v in ((V_NORM1, norm1_g), (V_CONV_B, conv_b), (V_BR, b_r), (V_BI, b_i), (V_LAM, lam),
                   (V_POOL_SCALE, pool_scale), (V_NORM2, norm2_g),
                   (V_FINAL, jnp.broadcast_to(final_g, (depth, D_MODEL)))):
        vec_rows[row] = v.astype(f32)
    w = {
        'vec': jnp.stack(vec_rows, axis=1), 'conv_w': conv_w.astype(f32),
        'w_in': w_in.astype(f32),
        'wri': jnp.concatenate([w_r, w_i], axis=-1).astype(bf16),
        'pool_w': pool_w.astype(bf16),
        'w_br_a': w_br_a.astype(bf16), 'w_br_b': w_br_b.astype(bf16), 'w_out': w_out.astype(bf16),
    }
    w_gu, w_down = w_gu.astype(f32), w_down.astype(f32)

    segs_s = [(s * ts, ts, PAST_LEN, True) for s in range(ns)] + [(ns * ts, N_META, 0, False)]
    xs = jnp.concatenate([x_sample.reshape(ns * ts, D_MODEL), meta_tokens.astype(f32)], axis=0)
    xp = x_prompt
    state_lru = state_lru[:, :, None, :]

    outs = {k: [] for k in ('cp', 'hp', 'pp', 'cs', 'hs', 'ps')}
    for l in range(depth):
        last = l == depth - 1
        xs, c_s, h_s, p_s, c_m, h_m, p_m, w_in_l = _mixer_step_call(xs, state_conv, state_lru, state_pool, w, l,
                                                                    segs_s, f"mixer_step_l{l}")
        xs, *ffn_weights = _ffn_cast_call(xs, w['vec'], w_gu, w_down, l, last, ns * ts if last else xs.shape[0],
                                          f"ffn_step_l{l}")
        xp, c_p, h_p, p_p = _layer_stream_call(xp, c_m, h_m, p_m, w, w_in_l, ffn_weights, l, last, TM_LAYER,
                                               f"layer_prompt_l{l}")

        outs['cp'].append(c_p); outs['hp'].append(h_p[:, 0]); outs['pp'].append(p_p)
        outs['cs'].append(c_s); outs['hs'].append(h_s[:, 0]); outs['ps'].append(p_s)

    y_sample = xs.reshape(ns, ts, D_MODEL)
    return (xp, y_sample, jnp.stack(outs['cp']), jnp.stack(outs['hp']), jnp.stack(outs['pp']),
            jnp.stack(outs['cs']), jnp.stack(outs['hs']), jnp.stack(outs['ps']))
```

```python
import functools

import jax
import jax.numpy as jnp
from jax import lax
from jax.experimental import pallas as pl
from jax.experimental.pallas import tpu as pltpu

D_MODEL = 1024
D_LRU = 1024
D_POOL = 1024
LRU_HEADS = 8
LRU_BLOCK = D_LRU // LRU_HEADS
LRU_C = 8.0
CONV_W = 4
POOL_WINDOWS = (2, 4, 8, 16)
POOL_GROUP = D_POOL // len(POOL_WINDOWS)
POOL_MAX = 16
D_FF = 2816
N_META = 16
PAST_LEN = 2048
EPS = 1e-6

SUBLANES = 8
BF16_ROWS = 16
HDR = 16
NCH = 256
N_IN_CH = 4 * D_MODEL // NCH
CONV_CH = 0
POOL_CH = D_LRU // NCH
GATE_A_CH = (D_LRU + D_POOL) // NCH
GATE_B_CH = (D_LRU + D_POOL + D_MODEL) // NCH
N_OUT_CH = D_MODEL // NCH
N_FF_CH = D_FF // NCH
TM_LAYER = 256
ROW_CH = 32
NORM_ROWS_MAX = 128
V7X_VMEM_BYTES = 64 * 1024 * 1024
VMEM_LIMIT = V7X_VMEM_BYTES - 4 * 1024 * 1024

V_NORM1, V_CONV_B, V_BR, V_BI, V_LAM, V_POOL_SCALE, V_NORM2, V_FINAL = range(8)
VEC_ROWS = 8
SQ_BRANCH_A, SQ_BRANCH_B, SQ_OUT = range(3)

f32 = jnp.float32
bf16 = jnp.bfloat16

assert POOL_GROUP == NCH and HDR >= POOL_MAX - 1 and HDR % BF16_ROWS == 0
assert D_LRU == D_POOL == D_MODEL


def _dot(a, b):
    return jnp.dot(a, b, preferred_element_type=f32)


def _chunk(c):
    return slice(c * NCH, (c + 1) * NCH)


def _vec(v_ref, row, cols=slice(None)):
    return v_ref[row:row + 1, cols]


def _rmsnorm(x, g):
    y = x * lax.rsqrt(jnp.mean(x * x, axis=-1, keepdims=True) + EPS)
    return y * g


def _norm_chunk(tm):
    return max(c for c in range(BF16_ROWS, NORM_ROWS_MAX + 1, BF16_ROWS) if tm % c == 0)


def _sigmoid(x):
    return 0.5 * jnp.tanh(0.5 * x) + 0.5


def _shift_rows(x, k):
    return pltpu.roll(x, k, axis=0)


def _for_rows(total, chunk, fn):
    n = total // chunk
    assert n * chunk == total
    if n == 1:
        fn(0)
    else:
        def body(k, c):
            fn(pl.multiple_of(k * chunk, chunk))
            return c
        lax.fori_loop(0, n, body, 0)


def _norm_to_bf16(x_ref, g, xn_s, tm):
    nc = _norm_chunk(tm)

    def norm_rows(r0):
        rows = pl.ds(r0, nc)
        xn_s[rows, :] = _rmsnorm(x_ref[rows, :], g).astype(bf16)
    _for_rows(tm, nc, norm_rows)


def _conv_taps(blk, v_ref, cw_ref, cs):
    tap = lambda k: _shift_rows(blk, CONV_W - 1 - k)[HDR:] if k < CONV_W - 1 else blk[HDR:]
    acc = _vec(v_ref, V_CONV_B, cs) + cw_ref[0:1, cs] * tap(0)
    for k in range(1, CONV_W):
        acc = acc + cw_ref[k:k + 1, cs] * tap(k)
    return acc


def _lam_scaled(v_ref):
    lam = _vec(v_ref, V_LAM)
    log_sig_lam = jnp.minimum(lam, 0.0) - jnp.log1p(jnp.exp(-jnp.abs(lam)))
    return LRU_C * log_sig_lam


def _lru_coeffs(r_pre, i_pre, c, lam_c, start_mask=None):
    r = _sigmoid(r_pre)
    i = _sigmoid(i_pre)
    log_a = r * lam_c
    a = jnp.exp(log_a)
    th = jnp.tanh(log_a)
    mult = jnp.sqrt(-2.0 * th / (1.0 - th))
    if start_mask is not None:
        mult = jnp.where(start_mask, 1.0, mult)
    return a, mult * (i * c)


def _scan_block(a, b, h):
    sub = lax.broadcasted_iota(jnp.int32, a.shape, 0)
    for d in (1, 2, 4):
        keep = sub >= d
        a_prev = pltpu.roll(a, d, axis=0)
        b_prev = pltpu.roll(b, d, axis=0)
        b = jnp.where(keep, a * b_prev, 0.0) + b
        a = jnp.where(keep, a * a_prev, a)
    return a * h + b


def _scan_rows(a, b, h):
    outs = []
    for k in range(a.shape[0] // SUBLANES):
        blk = slice(k * SUBLANES, (k + 1) * SUBLANES)
        outs.append(_scan_block(a[blk], b[blk], h))
        h = outs[-1][SUBLANES - 1:SUBLANES, :]
    return jnp.concatenate(outs, axis=0), h


def _pool_diff(blk, w, cnt=None):
    xb = blk[HDR:]
    tot, span = blk, 1
    while span < w:
        tot = tot + _shift_rows(tot, span)
        span *= 2
    tot = tot[HDR:]
    pooled = tot * (1.0 / w) if cnt is None else tot / cnt
    return pooled - xb


def _lru_gate_preact(h, cbf, wri_ref, v_ref):
    hs = slice(h * LRU_BLOCK, (h + 1) * LRU_BLOCK)
    ri = _dot(cbf[:, hs], wri_ref[h])
    return ri[:, :LRU_BLOCK] + _vec(v_ref, V_BR, hs), ri[:, LRU_BLOCK:] + _vec(v_ref, V_BI, hs)


def _pool_project(g, d_s, yb_s, pw_ref, v_ref):
    gs = _chunk(g)
    yb_s[:, gs] = (_dot(d_s[:, gs], pw_ref[g]) * _vec(v_ref, V_POOL_SCALE, gs)).astype(bf16)


def _branch_b(c, gbuf, yb_s, wsq, mb_s):
    gate_b = gbuf[GATE_B_CH - GATE_A_CH + c]
    mb_s[:, _chunk(c)] = _sigmoid(gate_b) * _dot(yb_s[...], wsq[SQ_BRANCH_B][:, _chunk(c)])


def _merge(c, gbuf, ya_s, wsq, mb_s, m_s):
    cs = _chunk(c)
    m_s[:, cs] = (_sigmoid(gbuf[c]) * _dot(ya_s[...], wsq[SQ_BRANCH_A][:, cs]) + mb_s[:, cs]).astype(bf16)


def _mixer_step_kernel(segs, tm, x_ref, convp_ref, hp_ref, poolp_ref, v_ref, cw_ref, win_f32_ref, wri_ref, pw_ref,
                       wa_ref, wb_ref, wo_ref,
                       y_ref, convn_ref, hn_ref, pooln_ref, convm_ref, hm_ref, poolm_ref, win_out,
                       xn_s, proj, gbuf, cbuf, cbf, a_s, b_s, ya_s, d_s, yb_s, mb_s, m_s):
    n_hist = GATE_A_CH
    wsq = (wa_ref, wb_ref, wo_ref)
    assert sum(1 for seg in segs if not seg[3]) == 1
    for s, (_, _, _, has_state) in enumerate(segs):
        for c in range(n_hist):
            proj[s, c, 0:HDR, :] = jnp.zeros((HDR, NCH), f32)
        if has_state:
            for c in range(POOL_CH):
                proj[s, CONV_CH + c, HDR - (CONV_W - 1):HDR, :] = convp_ref[s, :, _chunk(c)]
                proj[s, POOL_CH + c, HDR - (POOL_MAX - 1):HDR, :] = poolp_ref[s, :, _chunk(c)]

    _norm_to_bf16(x_ref, _vec(v_ref, V_NORM1), xn_s, tm)

    for c in range(N_IN_CH):
        w_chunk = win_f32_ref[:, _chunk(c)].astype(bf16)
        win_out[:, _chunk(c)] = w_chunk
        p = _dot(xn_s[...], w_chunk)
        if c < n_hist:
            for s, (row0, length, _, _) in enumerate(segs):
                proj[s, c, HDR:HDR + length, :] = p[row0:row0 + length]
        else:
            gbuf[c - n_hist] = p

    for s, (row0, length, _, has_state) in enumerate(segs):
        n = min(ROW_CH, length)

        def conv_rows(t0, s=s, row0=row0, n=n):
            rows = pl.ds(row0 + t0, n)
            for c in range(POOL_CH):
                acc = _conv_taps(proj[s, CONV_CH + c, pl.ds(t0, n + HDR), :], v_ref, cw_ref, _chunk(c))
                cbuf[rows, _chunk(c)] = acc
                cbf[rows, _chunk(c)] = acc.astype(bf16)
        _for_rows(length, n, conv_rows)
        conv_out, pool_out, k = (convn_ref, pooln_ref, s) if has_state else (convm_ref, poolm_ref, 0)
        for c in range(POOL_CH):
            conv_out[k, :, _chunk(c)] = proj[s, CONV_CH + c, HDR + length - (CONV_W - 1):HDR + length, :]
            pool_out[k, :, _chunk(c)] = proj[s, POOL_CH + c, HDR + length - (POOL_MAX - 1):HDR + length, :]

    for h in range(LRU_HEADS):
        hs = slice(h * LRU_BLOCK, (h + 1) * LRU_BLOCK)
        a_s[:, hs], b_s[:, hs] = _lru_gate_preact(h, cbf, wri_ref, v_ref)
    lam_c = _lam_scaled(v_ref)

    for s, (row0, length, pos0, has_state) in enumerate(segs):
        n = min(ROW_CH, length)

        def mix_rows(t0, h, s=s, row0=row0, n=n, pos0=pos0):
            rows = pl.ds(row0 + t0, n)
            pos = pos0 + t0 + lax.broadcasted_iota(jnp.int32, (n, 1), 0)
            start = (pos == 0) if pos0 == 0 else None
            a, b = _lru_coeffs(a_s[rows, :], b_s[rows, :], cbuf[rows, :], lam_c, start)
            states, h = _scan_rows(a, b, h)
            ya_s[rows, :] = states.astype(bf16)
            cnt = None if pos0 >= POOL_MAX - 1 else pos + 1
            for g, w in enumerate(POOL_WINDOWS):
                cnt_g = None if cnt is None else jnp.minimum(w, cnt).astype(f32)
                d_s[rows, _chunk(g)] = _pool_diff(proj[s, POOL_CH + g, pl.ds(t0, n + HDR), :], w, cnt_g).astype(bf16)
            return h

        steps = length // n
        h = hp_ref[s] if has_state else jnp.zeros((1, D_LRU), f32)
        if steps == 1:
            h = mix_rows(0, h)
        else:
            h = lax.fori_loop(0, steps, lambda k, h: mix_rows(pl.multiple_of(k * n, n), h), h)
        if has_state:
            hn_ref[s] = h
        else:
            hm_ref[0] = h

    for g in range(len(POOL_WINDOWS)):
        _pool_project(g, d_s, yb_s, pw_ref, v_ref)
    for c in range(N_OUT_CH):
        _branch_b(c, gbuf, yb_s, wsq, mb_s)
        _merge(c, gbuf, ya_s, wsq, mb_s, m_s)
    for c in range(N_OUT_CH):
        y_ref[:, _chunk(c)] = x_ref[:, _chunk(c)] + _dot(m_s[...], wsq[SQ_OUT][:, _chunk(c)])


def _layer_stream_kernel(tm, tiles_per_stream, n_tiles, final,
                         x_ref, convp_ref, hp_ref, poolp_ref, v_ref, cw_ref, win_ref, wri_ref, pw_ref,
                         wa_ref, wb_ref, wo_ref, wg_ref, wu_ref, wd_ref,
                         y_ref, convn_ref, hn_ref, pooln_ref,
                         xn_s, hist, gbuf, cbuf, cbf, ya_s, d_s, yb_s, mb_s, m_s, hcar, ymid, xn2_s, h_s):
    j = pl.program_id(0)
    new = j % 2
    cur = 1 - new
    n_hist = GATE_A_CH
    nc = _norm_chunk(tm)
    wsq = (wa_ref, wb_ref, wo_ref)

    @pl.when(j == 0)
    def _():
        hist[...] = jnp.zeros(hist.shape, f32)
        hcar[...] = jnp.zeros(hcar.shape, f32)
        ymid[1] = jnp.zeros((tm, D_MODEL), f32)

    for c in range(POOL_CH):
        convn_ref[0, :, _chunk(c)] = hist[CONV_CH + c, HDR - (CONV_W - 1):HDR, :]
        pooln_ref[0, :, _chunk(c)] = hist[POOL_CH + c, HDR - (POOL_MAX - 1):HDR, :]
    hn_ref[0] = hcar[0:1, :]

    @pl.when(jnp.logical_and(j % tiles_per_stream == 0, j < n_tiles))
    def _():
        for c in range(POOL_CH):
            hist[CONV_CH + c] = jnp.zeros((HDR, NCH), f32)
            hist[POOL_CH + c] = jnp.zeros((HDR, NCH), f32)
            hist[CONV_CH + c, HDR - (CONV_W - 1):HDR, :] = convp_ref[0, :, _chunk(c)]
            hist[POOL_CH + c, HDR - (POOL_MAX - 1):HDR, :] = poolp_ref[0, :, _chunk(c)]
        hcar[0:1, :] = hp_ref[0]

    lam_c = _lam_scaled(v_ref)

    def ffn_norm():
        for r0 in range(0, tm, nc):
            xn2_s[r0:r0 + nc, :] = _rmsnorm(ymid[cur, r0:r0 + nc, :], _vec(v_ref, V_NORM2)).astype(bf16)

    def ffn_up(c):
        gate = _dot(xn2_s[...], wg_ref[:, _chunk(c)])
        up = _dot(xn2_s[...], wu_ref[:, _chunk(c)])
        h_s[:, _chunk(c)] = (jax.nn.silu(gate) * up).astype(bf16)

    def ffn_down(c):
        y_ref[:, _chunk(c)] = ymid[cur, :, _chunk(c)] + _dot(h_s[...], wd_ref[:, _chunk(c)])

    def ffn_final_norm():
        for r0 in range(0, tm, nc):
            y_ref[r0:r0 + nc, :] = _rmsnorm(y_ref[r0:r0 + nc, :], _vec(v_ref, V_FINAL))

    def mixer_norm():
        for r0 in range(0, tm, nc):
            xn_s[r0:r0 + nc, :] = _rmsnorm(x_ref[r0:r0 + nc, :], _vec(v_ref, V_NORM1)).astype(bf16)

    def project_with_history(c):
        p = _dot(xn_s[...], win_ref[:, _chunk(c)])
        ext = jnp.concatenate([hist[c], p], axis=0)
        hist[c] = p[tm - HDR:]
        return ext

    def conv_chunk(c):
        acc = _conv_taps(project_with_history(CONV_CH + c), v_ref, cw_ref, _chunk(c))
        cbuf[:, _chunk(c)] = acc
        cbf[:, _chunk(c)] = acc.astype(bf16)

    def pool_chunk(g):
        d_s[:, _chunk(g)] = _pool_diff(project_with_history(POOL_CH + g), POOL_WINDOWS[g]).astype(bf16)

    def gate_chunk(c):
        gbuf[c - n_hist] = _dot(xn_s[...], win_ref[:, _chunk(c)])

    def lru_head(h):
        hs = slice(h * LRU_BLOCK, (h + 1) * LRU_BLOCK)
        r_pre, i_pre = _lru_gate_preact(h, cbf, wri_ref, v_ref)
        a, b = _lru_coeffs(r_pre, i_pre, cbuf[:, hs], lam_c[:, hs])
        states, last = _scan_rows(a, b, hcar[0:1, hs])
        ya_s[:, hs] = states.astype(bf16)
        hcar[0:1, hs] = last

    def project_out(c):
        ymid[new, :, _chunk(c)] = x_ref[:, _chunk(c)] + _dot(m_s[...], wsq[SQ_OUT][:, _chunk(c)])

    filler = [functools.partial(ffn_up, c) for c in range(N_FF_CH)]
    filler += [functools.partial(ffn_down, c) for c in range(N_OUT_CH)]
    if final:
        filler.append(ffn_final_norm)
    others = [functools.partial(pool_chunk, g) for g in range(len(POOL_WINDOWS))]
    others += [functools.partial(gate_chunk, GATE_B_CH + c) for c in range(N_OUT_CH)]
    others += [functools.partial(_pool_project, g, d_s, yb_s, pw_ref, v_ref) for g in range(len(POOL_WINDOWS))]
    others += [functools.partial(_branch_b, c, gbuf, yb_s, wsq, mb_s) for c in range(N_OUT_CH)]
    others += [functools.partial(gate_chunk, GATE_A_CH + c) for c in range(N_OUT_CH)]

    stages = [functools.partial(conv_chunk, c) for c in range(POOL_CH)]
    stages += [functools.partial(lru_head, h) for h in range(LRU_HEADS)]
    first_head = POOL_CH

    ffn_norm()
    mixer_norm()
    for k, stage in enumerate(stages):
        stage()
        lo, hi = (k * len(filler)) // len(stages), ((k + 1) * len(filler)) // len(stages)
        for fn in filler[lo:hi]:
            fn()
        if k >= first_head:
            h = k - first_head
            lo, hi = (h * len(others)) // LRU_HEADS, ((h + 1) * len(others)) // LRU_HEADS
            for fn in others[lo:hi]:
                fn()
    for c in range(N_OUT_CH):
        _merge(c, gbuf, ya_s, wsq, mb_s, m_s)
    for c in range(N_OUT_CH):
        project_out(c)


def _ffn_cast_kernel(final, tm, x_ref, v_ref, wg_ref, wu_ref, wd_ref, y_ref, wg_out, wu_out, wd_out, xn_s, acc_s):
    c = pl.program_id(0)
    nc = _norm_chunk(tm)

    @pl.when(c == 0)
    def _():
        _norm_to_bf16(x_ref, _vec(v_ref, V_NORM2), xn_s, tm)
        acc_s[...] = x_ref[...]

    wg = wg_ref[...].astype(bf16)
    wu = wu_ref[...].astype(bf16)
    wd = wd_ref[...].astype(bf16)
    wg_out[...] = wg
    wu_out[...] = wu
    wd_out[...] = wd
    hidden = (jax.nn.silu(_dot(xn_s[...], wg)) * _dot(xn_s[...], wu)).astype(bf16)
    acc_s[...] += _dot(hidden, wd)

    @pl.when(c == pl.num_programs(0) - 1)
    def _():
        if final:
            def final_rows(r0):
                rows = pl.ds(r0, nc)
                y_ref[rows, :] = _rmsnorm(acc_s[rows, :], _vec(v_ref, V_FINAL))
            _for_rows(tm, nc, final_rows)
        else:
            y_ref[...] = acc_s[...]


def _layer_spec(w, layer):
    nd = w.ndim - 1
    return pl.BlockSpec((None,) + w.shape[1:], lambda *_: (layer,) + (0,) * nd, pipeline_mode=pl.Buffered(1))


def _mixer_weights(w):
    return (w['vec'], w['conv_w'], w['w_in'], w['wri'], w['pool_w'], w['w_br_a'], w['w_br_b'], w['w_out'])


def _state_shapes(n):
    return (jax.ShapeDtypeStruct((n, CONV_W - 1, D_LRU), f32),
            jax.ShapeDtypeStruct((n, 1, D_LRU), f32),
            jax.ShapeDtypeStruct((n, POOL_MAX - 1, D_POOL), f32))


def _state_specs(n, index_map):
    return [pl.BlockSpec((n, CONV_W - 1, D_LRU), index_map), pl.BlockSpec((n, 1, D_LRU), index_map),
            pl.BlockSpec((n, POOL_MAX - 1, D_POOL), index_map)]


def _stacked_state_outputs(depth, n_total, n_block, layer, block_index):
    shapes = tuple(jax.ShapeDtypeStruct((depth,) + s.shape, f32) for s in _state_shapes(n_total))
    specs = [pl.BlockSpec((None, n_block) + s.shape[2:], lambda *ids: (layer, block_index(*ids), 0, 0)) for s in shapes]
    return shapes, specs


def _without_refs(kernel, start, count):
    def wrapped(*refs):
        return kernel(*refs[:start], *refs[start + count:])
    return wrapped


def _carry_in(prev):
    if prev is None:
        return (), [], 0
    return tuple(prev), [pl.BlockSpec(memory_space=pl.ANY)] * len(prev), len(prev)


def _mixer_step_call(x, state_conv, state_lru, state_pool, w, layer, segs, prev_states, name):
    tm = x.shape[0]
    depth, ns = state_conv.shape[:2]
    carried, carried_specs, n_carried = _carry_in(prev_states)
    stacked_shapes, stacked_specs = _stacked_state_outputs(depth, ns, ns, layer, lambda i: 0)
    max_len = max(seg[1] for seg in segs)
    weights = _mixer_weights(w)
    x_spec = pl.BlockSpec((tm, D_MODEL), lambda i: (0, 0))
    whole = lambda i: (0, 0, 0)
    state_in = [pl.BlockSpec((None,) + s.shape[1:], lambda i: (layer, 0, 0, 0))
                for s in (state_conv, state_lru, state_pool)]
    scratch = [
        pltpu.VMEM((tm, D_MODEL), bf16),
        pltpu.VMEM((len(segs), GATE_A_CH, HDR + max_len, NCH), f32),
        pltpu.VMEM((N_IN_CH - GATE_A_CH, tm, NCH), f32),
        pltpu.VMEM((tm, D_LRU), f32),
        pltpu.VMEM((tm, D_LRU), bf16),
        pltpu.VMEM((tm, D_LRU), f32),
        pltpu.VMEM((tm, D_LRU), f32),
        pltpu.VMEM((tm, D_LRU), bf16),
        pltpu.VMEM((tm, D_POOL), bf16),
        pltpu.VMEM((tm, D_POOL), bf16),
        pltpu.VMEM((tm, D_MODEL), f32),
        pltpu.VMEM((tm, D_MODEL), bf16),
    ]
    n_in = 4 + len(weights)
    return pl.pallas_call(
        _without_refs(functools.partial(_mixer_step_kernel, tuple(segs), tm), n_in, n_carried),
        grid=(1,),
        in_specs=[x_spec] + state_in + [_layer_spec(v, layer) for v in weights] + carried_specs,
        out_specs=([x_spec] + stacked_specs + _state_specs(1, whole)
                   + [pl.BlockSpec(w['w_in'].shape[1:], lambda i: (0, 0))]),
        out_shape=((jax.ShapeDtypeStruct(x.shape, f32),) + stacked_shapes + _state_shapes(1)
                   + (jax.ShapeDtypeStruct(w['w_in'].shape[1:], bf16),)),
        input_output_aliases={n_in + k: 1 + k for k in range(n_carried)},
        scratch_shapes=scratch,
        compiler_params=pltpu.CompilerParams(dimension_semantics=("arbitrary",), vmem_limit_bytes=VMEM_LIMIT),
        name=name,
    )(x, state_conv, state_lru, state_pool, *weights, *carried)


def _whole_spec(w):
    nd = w.ndim
    return pl.BlockSpec(w.shape, lambda *_: (0,) * nd, pipeline_mode=pl.Buffered(1))


def _layer_stream_call(x, conv_init, h_init, pool_init, w, w_in, ffn_weights, layer, depth, prev_states, final, tm,
                       name):
    nb, t, _ = x.shape
    tps = t // tm
    n_tiles = nb * tps
    assert tps * tm == t
    weights = tuple(w_in if v is w['w_in'] else v for v in _mixer_weights(w))
    weight_specs = [_whole_spec(v) if v is w_in else _layer_spec(v, layer) for v in weights]
    x_tiles = x.reshape(n_tiles, tm, D_MODEL)
    x_spec = pl.BlockSpec((None, tm, D_MODEL), lambda j: (jnp.minimum(j, n_tiles - 1), 0, 0))
    y_spec = pl.BlockSpec((None, tm, D_MODEL), lambda j: (jnp.maximum(j - 1, 0), 0, 0))
    scratch = [
        pltpu.VMEM((tm, D_MODEL), bf16),
        pltpu.VMEM((GATE_A_CH, HDR, NCH), f32),
        pltpu.VMEM((N_IN_CH - GATE_A_CH, tm, NCH), f32),
        pltpu.VMEM((tm, D_LRU), f32),
        pltpu.VMEM((tm, D_LRU), bf16),
        pltpu.VMEM((tm, D_LRU), bf16),
        pltpu.VMEM((tm, D_POOL), bf16),
        pltpu.VMEM((tm, D_POOL), bf16),
        pltpu.VMEM((tm, D_MODEL), f32),
        pltpu.VMEM((tm, D_MODEL), bf16),
        pltpu.VMEM((SUBLANES, D_LRU), f32),
        pltpu.VMEM((2, tm, D_MODEL), f32),
        pltpu.VMEM((tm, D_MODEL), bf16),
        pltpu.VMEM((tm, D_FF), bf16),
    ]
    carried, carried_specs, n_carried = _carry_in(prev_states)
    stacked_shapes, stacked_specs = _stacked_state_outputs(depth, nb, 1, layer,
                                                           lambda j: jnp.maximum(j - 1, 0) // tps)
    n_in = 4 + len(weights) + len(ffn_weights)
    y, c_n, h_n, p_n = pl.pallas_call(
        _without_refs(functools.partial(_layer_stream_kernel, tm, tps, n_tiles, final), n_in, n_carried),
        grid=(n_tiles + 1,),
        in_specs=([x_spec] + _state_specs(1, lambda j: (0, 0, 0)) + weight_specs
                  + [_whole_spec(v) for v in ffn_weights] + carried_specs),
        out_specs=[y_spec] + stacked_specs,
        out_shape=(jax.ShapeDtypeStruct(x_tiles.shape, f32),) + stacked_shapes,
        input_output_aliases={n_in + k: 1 + k for k in range(n_carried)},
        scratch_shapes=scratch,
        compiler_params=pltpu.CompilerParams(dimension_semantics=("arbitrary",), vmem_limit_bytes=VMEM_LIMIT),
        name=name,
    )(x_tiles, conv_init, h_init, pool_init, *weights, *ffn_weights, *carried)
    return y.reshape(x.shape), c_n, h_n, p_n


def _ffn_cast_call(x, vec, w_gu, w_down, layer, final, rows, name):
    tm = rows
    x_spec = pl.BlockSpec((tm, D_MODEL), lambda c: (0, 0))
    in_specs = [x_spec, _layer_spec(vec, layer),
                pl.BlockSpec((None, D_MODEL, NCH), lambda c: (layer, 0, c)),
                pl.BlockSpec((None, D_MODEL, NCH), lambda c: (layer, 0, N_FF_CH + c)),
                pl.BlockSpec((None, NCH, D_MODEL), lambda c: (layer, c, 0))]
    out_specs = [x_spec, pl.BlockSpec((D_MODEL, NCH), lambda c: (0, c)), pl.BlockSpec((D_MODEL, NCH), lambda c: (0, c)),
                 pl.BlockSpec((NCH, D_MODEL), lambda c: (c, 0))]
    out_shape = (jax.ShapeDtypeStruct((tm, D_MODEL), f32), jax.ShapeDtypeStruct((D_MODEL, D_FF), bf16),
                 jax.ShapeDtypeStruct((D_MODEL, D_FF), bf16), jax.ShapeDtypeStruct((D_FF, D_MODEL), bf16))
    return pl.pallas_call(
        functools.partial(_ffn_cast_kernel, final, tm),
        grid=(N_FF_CH,),
        in_specs=in_specs,
        out_specs=out_specs,
        out_shape=out_shape,
        scratch_shapes=[pltpu.VMEM((tm, D_MODEL), bf16), pltpu.VMEM((tm, D_MODEL), f32)],
        compiler_params=pltpu.CompilerParams(dimension_semantics=("arbitrary",), vmem_limit_bytes=VMEM_LIMIT),
        name=name,
    )(x, vec, w_gu, w_gu, w_down)


def kernel(x_prompt, x_sample, state_conv, state_lru, state_pool, meta_tokens, norm1_g, w_in, conv_w, conv_b,
           lam, w_r, b_r, w_i, b_i, pool_w, pool_scale, w_br_a, w_br_b, w_out, norm2_g, w_gu, w_down, final_g):
    depth = w_in.shape[0]
    ns, ts, _ = x_sample.shape
    vec_rows = [None] * VEC_ROWS
    for row, v in ((V_NORM1, norm1_g), (V_CONV_B, conv_b), (V_BR, b_r), (V_BI, b_i), (V_LAM, lam),
                   (V_POOL_SCALE, pool_scale), (V_NORM2, norm2_g),
                   (V_FINAL, jnp.broadcast_to(final_g, (depth, D_MODEL)))):
        vec_rows[row] = v.astype(f32)
    w = {
        'vec': jnp.stack(vec_rows, axis=1), 'conv_w': conv_w.astype(f32),
        'w_in': w_in.astype(f32),
        'wri': jnp.concatenate([w_r, w_i], axis=-1).astype(bf16),
        'pool_w': pool_w.astype(bf16),
        'w_br_a': w_br_a.astype(bf16), 'w_br_b': w_br_b.astype(bf16), 'w_out': w_out.astype(bf16),
    }
    w_gu, w_down = w_gu.astype(f32), w_down.astype(f32)

    segs_s = [(s * ts, ts, PAST_LEN, True) for s in range(ns)] + [(ns * ts, N_META, 0, False)]
    xs = jnp.concatenate([x_sample.reshape(ns * ts, D_MODEL), meta_tokens.astype(f32)], axis=0)
    xp = x_prompt
    state_lru = state_lru[:, :, None, :]

    run_states = new_states = None
    for l in range(depth):
        last = l == depth - 1
        xs, *run_states, c_m, h_m, p_m, w_in_l = _mixer_step_call(xs, state_conv, state_lru, state_pool, w, l,
                                                                  segs_s, run_states, f"mixer_step_l{l}")
        xs, *ffn_weights = _ffn_cast_call(xs, w['vec'], w_gu, w_down, l, last, ns * ts if last else xs.shape[0],
                                          f"ffn_step_l{l}")
        xp, *new_states = _layer_stream_call(xp, c_m, h_m, p_m, w, w_in_l, ffn_weights, l, depth, new_states, last,
                                             TM_LAYER, f"layer_prompt_l{l}")

    y_sample = xs.reshape(ns, ts, D_MODEL)
    c_p, h_p, p_p = new_states
    c_s, h_s, p_s = run_states
    return (xp, y_sample, c_p, h_p[:, :, 0], p_p, c_s, h_s[:, :, 0], p_s)
```

```python
import functools

import jax
import jax.numpy as jnp
from jax import lax
from jax.experimental import pallas as pl
from jax.experimental.pallas import tpu as pltpu

D_MODEL = 1024
D_LRU = 1024
D_POOL = 1024
LRU_HEADS = 8
LRU_BLOCK = D_LRU // LRU_HEADS
LRU_C = 8.0
CONV_W = 4
POOL_WINDOWS = (2, 4, 8, 16)
POOL_GROUP = D_POOL // len(POOL_WINDOWS)
POOL_MAX = 16
D_FF = 2816
N_META = 16
PAST_LEN = 2048
EPS = 1e-6

SUBLANES = 8
BF16_ROWS = 16
HDR = 16
NCH = 256
N_IN_CH = 4 * D_MODEL // NCH
CONV_CH = 0
POOL_CH = D_LRU // NCH
GATE_A_CH = (D_LRU + D_POOL) // NCH
GATE_B_CH = (D_LRU + D_POOL + D_MODEL) // NCH
N_OUT_CH = D_MODEL // NCH
N_FF_CH = D_FF // NCH
TM_LAYER = 256
ROW_CH = 32
NORM_ROWS_MAX = 128
V7X_VMEM_BYTES = 64 * 1024 * 1024
VMEM_LIMIT = V7X_VMEM_BYTES - 4 * 1024 * 1024

V_NORM1, V_CONV_B, V_BR, V_BI, V_LAM, V_POOL_SCALE, V_NORM2, V_FINAL = range(8)
VEC_ROWS = 8
SQ_BRANCH_A, SQ_BRANCH_B, SQ_OUT = range(3)

f32 = jnp.float32
bf16 = jnp.bfloat16

assert POOL_GROUP == NCH and HDR >= POOL_MAX - 1 and HDR % BF16_ROWS == 0
assert D_LRU == D_POOL == D_MODEL


def _dot(a, b):
    return jnp.dot(a, b, preferred_element_type=f32)


def _chunk(c):
    return slice(c * NCH, (c + 1) * NCH)


def _vec(v_ref, row, cols=slice(None)):
    return v_ref[row:row + 1, cols]


def _rmsnorm(x, g):
    y = x * lax.rsqrt(jnp.mean(x * x, axis=-1, keepdims=True) + EPS)
    return y * g


def _norm_chunk(tm):
    return max(c for c in range(BF16_ROWS, NORM_ROWS_MAX + 1, BF16_ROWS) if tm % c == 0)


def _sigmoid(x):
    return 0.5 * jnp.tanh(0.5 * x) + 0.5


def _shift_rows(x, k):
    return pltpu.roll(x, k, axis=0)


def _for_rows(total, chunk, fn):
    n = total // chunk
    assert n * chunk == total
    if n == 1:
        fn(0)
    else:
        def body(k, c):
            fn(pl.multiple_of(k * chunk, chunk))
            return c
        lax.fori_loop(0, n, body, 0)


def _norm_to_bf16(x_ref, g, xn_s, tm):
    nc = _norm_chunk(tm)

    def norm_rows(r0):
        rows = pl.ds(r0, nc)
        xn_s[rows, :] = _rmsnorm(x_ref[rows, :], g).astype(bf16)
    _for_rows(tm, nc, norm_rows)


def _conv_taps(blk, v_ref, cw_ref, cs):
    tap = lambda k: _shift_rows(blk, CONV_W - 1 - k)[HDR:] if k < CONV_W - 1 else blk[HDR:]
    acc = _vec(v_ref, V_CONV_B, cs) + cw_ref[0:1, cs] * tap(0)
    for k in range(1, CONV_W):
        acc = acc + cw_ref[k:k + 1, cs] * tap(k)
    return acc


def _lam_scaled(v_ref):
    lam = _vec(v_ref, V_LAM)
    log_sig_lam = jnp.minimum(lam, 0.0) - jnp.log1p(jnp.exp(-jnp.abs(lam)))
    return LRU_C * log_sig_lam


def _lru_coeffs(r_pre, i_pre, c, lam_c, start_mask=None):
    r = _sigmoid(r_pre)
    i = _sigmoid(i_pre)
    log_a = r * lam_c
    a = jnp.exp(log_a)
    th = jnp.tanh(log_a)
    mult = jnp.sqrt(-2.0 * th / (1.0 - th))
    if start_mask is not None:
        mult = jnp.where(start_mask, 1.0, mult)
    return a, mult * (i * c)


def _scan_block(a, b, h):
    sub = lax.broadcasted_iota(jnp.int32, a.shape, 0)
    for d in (1, 2, 4):
        keep = sub >= d
        a_prev = pltpu.roll(a, d, axis=0)
        b_prev = pltpu.roll(b, d, axis=0)
        b = jnp.where(keep, a * b_prev, 0.0) + b
        a = jnp.where(keep, a * a_prev, a)
    return a * h + b


def _scan_rows(a, b, h):
    outs = []
    for k in range(a.shape[0] // SUBLANES):
        blk = slice(k * SUBLANES, (k + 1) * SUBLANES)
        outs.append(_scan_block(a[blk], b[blk], h))
        h = outs[-1][SUBLANES - 1:SUBLANES, :]
    return jnp.concatenate(outs, axis=0), h


def _pool_diff(blk, w, cnt=None):
    xb = blk[HDR:]
    tot, span = blk, 1
    while span < w:
        tot = tot + _shift_rows(tot, span)
        span *= 2
    tot = tot[HDR:]
    pooled = tot * (1.0 / w) if cnt is None else tot / cnt
    return pooled - xb


def _lru_gate_preact(h, cbf, wri_ref, v_ref):
    hs = slice(h * LRU_BLOCK, (h + 1) * LRU_BLOCK)
    ri = _dot(cbf[:, hs], wri_ref[h])
    return ri[:, :LRU_BLOCK] + _vec(v_ref, V_BR, hs), ri[:, LRU_BLOCK:] + _vec(v_ref, V_BI, hs)


def _pool_project(g, d_s, yb_s, pw_ref, v_ref):
    gs = _chunk(g)
    yb_s[:, gs] = (_dot(d_s[:, gs], pw_ref[g]) * _vec(v_ref, V_POOL_SCALE, gs)).astype(bf16)


def _branch_b(c, gbuf, yb_s, wsq, mb_s):
    gate_b = gbuf[GATE_B_CH - GATE_A_CH + c]
    mb_s[:, _chunk(c)] = _sigmoid(gate_b) * _dot(yb_s[...], wsq[SQ_BRANCH_B][:, _chunk(c)])


def _merge(c, gbuf, ya_s, wsq, mb_s, m_s):
    cs = _chunk(c)
    m_s[:, cs] = (_sigmoid(gbuf[c]) * _dot(ya_s[...], wsq[SQ_BRANCH_A][:, cs]) + mb_s[:, cs]).astype(bf16)


def _mixer_step_kernel(segs, tm, x_ref, convp_ref, hp_ref, poolp_ref, v_ref, cw_ref, win_f32_ref, wri_ref, pw_ref,
                       wa_ref, wb_ref, wo_ref,
                       y_ref, convn_ref, hn_ref, pooln_ref, convm_ref, hm_ref, poolm_ref, win_out,
                       xn_s, proj, gbuf, cbuf, cbf, a_s, b_s, ya_s, d_s, yb_s, mb_s, m_s):
    n_hist = GATE_A_CH
    wsq = (wa_ref, wb_ref, wo_ref)
    assert sum(1 for seg in segs if not seg[3]) == 1
    for s, (_, _, _, has_state) in enumerate(segs):
        for c in range(n_hist):
            proj[s, c, 0:HDR, :] = jnp.zeros((HDR, NCH), f32)
        if has_state:
            for c in range(POOL_CH):
                proj[s, CONV_CH + c, HDR - (CONV_W - 1):HDR, :] = convp_ref[s, :, _chunk(c)]
                proj[s, POOL_CH + c, HDR - (POOL_MAX - 1):HDR, :] = poolp_ref[s, :, _chunk(c)]

    _norm_to_bf16(x_ref, _vec(v_ref, V_NORM1), xn_s, tm)

    for c in range(N_IN_CH):
        w_chunk = win_f32_ref[:, _chunk(c)].astype(bf16)
        win_out[:, _chunk(c)] = w_chunk
        p = _dot(xn_s[...], w_chunk)
        if c < n_hist:
            for s, (row0, length, _, _) in enumerate(segs):
                proj[s, c, HDR:HDR + length, :] = p[row0:row0 + length]
        else:
            gbuf[c - n_hist] = p

    for s, (row0, length, _, has_state) in enumerate(segs):
        n = min(ROW_CH, length)

        def conv_rows(t0, s=s, row0=row0, n=n):
            rows = pl.ds(row0 + t0, n)
            for c in range(POOL_CH):
                acc = _conv_taps(proj[s, CONV_CH + c, pl.ds(t0, n + HDR), :], v_ref, cw_ref, _chunk(c))
                cbuf[rows, _chunk(c)] = acc
                cbf[rows, _chunk(c)] = acc.astype(bf16)
        _for_rows(length, n, conv_rows)
        conv_out, pool_out, k = (convn_ref, pooln_ref, s) if has_state else (convm_ref, poolm_ref, 0)
        for c in range(POOL_CH):
            conv_out[k, :, _chunk(c)] = proj[s, CONV_CH + c, HDR + length - (CONV_W - 1):HDR + length, :]
            pool_out[k, :, _chunk(c)] = proj[s, POOL_CH + c, HDR + length - (POOL_MAX - 1):HDR + length, :]

    for h in range(LRU_HEADS):
        hs = slice(h * LRU_BLOCK, (h + 1) * LRU_BLOCK)
        a_s[:, hs], b_s[:, hs] = _lru_gate_preact(h, cbf, wri_ref, v_ref)
    lam_c = _lam_scaled(v_ref)

    for s, (row0, length, pos0, has_state) in enumerate(segs):
        n = min(ROW_CH, length)

        def mix_rows(t0, h, s=s, row0=row0, n=n, pos0=pos0):
            rows = pl.ds(row0 + t0, n)
            pos = pos0 + t0 + lax.broadcasted_iota(jnp.int32, (n, 1), 0)
            start = (pos == 0) if pos0 == 0 else None
            a, b = _lru_coeffs(a_s[rows, :], b_s[rows, :], cbuf[rows, :], lam_c, start)
            states, h = _scan_rows(a, b, h)
            ya_s[rows, :] = states.astype(bf16)
            cnt = None if pos0 >= POOL_MAX - 1 else pos + 1
            for g, w in enumerate(POOL_WINDOWS):
                cnt_g = None if cnt is None else jnp.minimum(w, cnt).astype(f32)
                d_s[rows, _chunk(g)] = _pool_diff(proj[s, POOL_CH + g, pl.ds(t0, n + HDR), :], w, cnt_g).astype(bf16)
            return h

        steps = length // n
        h = hp_ref[s] if has_state else jnp.zeros((1, D_LRU), f32)
        if steps == 1:
            h = mix_rows(0, h)
        else:
            h = lax.fori_loop(0, steps, lambda k, h: mix_rows(pl.multiple_of(k * n, n), h), h)
        if has_state:
            hn_ref[s] = h
        else:
            hm_ref[0] = h

    for g in range(len(POOL_WINDOWS)):
        _pool_project(g, d_s, yb_s, pw_ref, v_ref)
    for c in range(N_OUT_CH):
        _branch_b(c, gbuf, yb_s, wsq, mb_s)
        _merge(c, gbuf, ya_s, wsq, mb_s, m_s)
    for c in range(N_OUT_CH):
        y_ref[:, _chunk(c)] = x_ref[:, _chunk(c)] + _dot(m_s[...], wsq[SQ_OUT][:, _chunk(c)])


def _layer_stream_kernel(tm, tiles_per_stream, n_tiles, final,
                         x_ref, convp_ref, hp_ref, poolp_ref, v_ref, cw_ref, win_ref, wri_ref, pw_ref,
                         wa_ref, wb_ref, wo_ref, wg_ref, wu_ref, wd_ref,
                         y_ref, convn_ref, hn_ref, pooln_ref,
                         xn_s, hist, gbuf, cbuf, cbf, ya_s, d_s, yb_s, mb_s, m_s, hcar, ymid, xn2_s, h_s):
    j = pl.program_id(0)
    new = j % 2
    cur = 1 - new
    n_hist = GATE_A_CH
    nc = _norm_chunk(tm)
    wsq = (wa_ref, wb_ref, wo_ref)

    @pl.when(j == 0)
    def _():
        hist[...] = jnp.zeros(hist.shape, f32)
        hcar[...] = jnp.zeros(hcar.shape, f32)
        ymid[1] = jnp.zeros((tm, D_MODEL), f32)

    @pl.when(j % tiles_per_stream == 0)
    def _():
        for c in range(POOL_CH):
            convn_ref[0, :, _chunk(c)] = hist[CONV_CH + c, HDR - (CONV_W - 1):HDR, :]
            pooln_ref[0, :, _chunk(c)] = hist[POOL_CH + c, HDR - (POOL_MAX - 1):HDR, :]
        hn_ref[0] = hcar[0:1, :]

    @pl.when(jnp.logical_and(j % tiles_per_stream == 0, j < n_tiles))
    def _():
        for c in range(POOL_CH):
            hist[CONV_CH + c] = jnp.zeros((HDR, NCH), f32)
            hist[POOL_CH + c] = jnp.zeros((HDR, NCH), f32)
            hist[CONV_CH + c, HDR - (CONV_W - 1):HDR, :] = convp_ref[0, :, _chunk(c)]
            hist[POOL_CH + c, HDR - (POOL_MAX - 1):HDR, :] = poolp_ref[0, :, _chunk(c)]
        hcar[0:1, :] = hp_ref[0]

    lam_c = _lam_scaled(v_ref)

    def ffn_norm():
        for r0 in range(0, tm, nc):
            xn2_s[r0:r0 + nc, :] = _rmsnorm(ymid[cur, r0:r0 + nc, :], _vec(v_ref, V_NORM2)).astype(bf16)

    def ffn_up(c):
        gate = _dot(xn2_s[...], wg_ref[:, _chunk(c)])
        up = _dot(xn2_s[...], wu_ref[:, _chunk(c)])
        h_s[:, _chunk(c)] = (jax.nn.silu(gate) * up).astype(bf16)

    def ffn_down(c):
        y_ref[:, _chunk(c)] = ymid[cur, :, _chunk(c)] + _dot(h_s[...], wd_ref[:, _chunk(c)])

    def ffn_final_norm():
        for r0 in range(0, tm, nc):
            y_ref[r0:r0 + nc, :] = _rmsnorm(y_ref[r0:r0 + nc, :], _vec(v_ref, V_FINAL))

    def mixer_norm():
        for r0 in range(0, tm, nc):
            xn_s[r0:r0 + nc, :] = _rmsnorm(x_ref[r0:r0 + nc, :], _vec(v_ref, V_NORM1)).astype(bf16)

    def project_with_history(c):
        p = _dot(xn_s[...], win_ref[:, _chunk(c)])
        ext = jnp.concatenate([hist[c], p], axis=0)
        hist[c] = p[tm - HDR:]
        return ext

    def conv_chunk(c):
        acc = _conv_taps(project_with_history(CONV_CH + c), v_ref, cw_ref, _chunk(c))
        cbuf[:, _chunk(c)] = acc
        cbf[:, _chunk(c)] = acc.astype(bf16)

    def pool_chunk(g):
        d_s[:, _chunk(g)] = _pool_diff(project_with_history(POOL_CH + g), POOL_WINDOWS[g]).astype(bf16)

    def gate_chunk(c):
        gbuf[c - n_hist] = _dot(xn_s[...], win_ref[:, _chunk(c)])

    def lru_head(h):
        hs = slice(h * LRU_BLOCK, (h + 1) * LRU_BLOCK)
        r_pre, i_pre = _lru_gate_preact(h, cbf, wri_ref, v_ref)
        a, b = _lru_coeffs(r_pre, i_pre, cbuf[:, hs], lam_c[:, hs])
        states, last = _scan_rows(a, b, hcar[0:1, hs])
        ya_s[:, hs] = states.astype(bf16)
        hcar[0:1, hs] = last

    def project_out(c):
        ymid[new, :, _chunk(c)] = x_ref[:, _chunk(c)] + _dot(m_s[...], wsq[SQ_OUT][:, _chunk(c)])

    filler = [functools.partial(ffn_up, c) for c in range(N_FF_CH)]
    filler += [functools.partial(ffn_down, c) for c in range(N_OUT_CH)]
    if final:
        filler.append(ffn_final_norm)
    others = [functools.partial(pool_chunk, g) for g in range(len(POOL_WINDOWS))]
    others += [functools.partial(gate_chunk, GATE_B_CH + c) for c in range(N_OUT_CH)]
    others += [functools.partial(_pool_project, g, d_s, yb_s, pw_ref, v_ref) for g in range(len(POOL_WINDOWS))]
    others += [functools.partial(_branch_b, c, gbuf, yb_s, wsq, mb_s) for c in range(N_OUT_CH)]
    others += [functools.partial(gate_chunk, GATE_A_CH + c) for c in range(N_OUT_CH)]

    stages = [functools.partial(conv_chunk, c) for c in range(POOL_CH)]
    stages += [functools.partial(lru_head, h) for h in range(LRU_HEADS)]
    first_head = POOL_CH

    ffn_norm()
    mixer_norm()
    for k, stage in enumerate(stages):
        stage()
        lo, hi = (k * len(filler)) // len(stages), ((k + 1) * len(filler)) // len(stages)
        for fn in filler[lo:hi]:
            fn()
        if k >= first_head:
            h = k - first_head
            lo, hi = (h * len(others)) // LRU_HEADS, ((h + 1) * len(others)) // LRU_HEADS
            for fn in others[lo:hi]:
                fn()
    for c in range(N_OUT_CH):
        _merge(c, gbuf, ya_s, wsq, mb_s, m_s)
    for c in range(N_OUT_CH):
        project_out(c)


def _ffn_cast_kernel(final, tm, x_ref, v_ref, wg_ref, wu_ref, wd_ref, y_ref, wg_out, wu_out, wd_out, xn_s, acc_s):
    c = pl.program_id(0)
    nc = _norm_chunk(tm)

    @pl.when(c == 0)
    def _():
        _norm_to_bf16(x_ref, _vec(v_ref, V_NORM2), xn_s, tm)
        acc_s[...] = x_ref[...]

    wg = wg_ref[...].astype(bf16)
    wu = wu_ref[...].astype(bf16)
    wd = wd_ref[...].astype(bf16)
    wg_out[...] = wg
    wu_out[...] = wu
    wd_out[...] = wd
    hidden = (jax.nn.silu(_dot(xn_s[...], wg)) * _dot(xn_s[...], wu)).astype(bf16)
    acc_s[...] += _dot(hidden, wd)

    @pl.when(c == pl.num_programs(0) - 1)
    def _():
        if final:
            def final_rows(r0):
                rows = pl.ds(r0, nc)
                y_ref[rows, :] = _rmsnorm(acc_s[rows, :], _vec(v_ref, V_FINAL))
            _for_rows(tm, nc, final_rows)
        else:
            y_ref[...] = acc_s[...]


def _layer_spec(w, layer):
    nd = w.ndim - 1
    return pl.BlockSpec((None,) + w.shape[1:], lambda *_: (layer,) + (0,) * nd, pipeline_mode=pl.Buffered(1))


def _mixer_weights(w):
    return (w['vec'], w['conv_w'], w['w_in'], w['wri'], w['pool_w'], w['w_br_a'], w['w_br_b'], w['w_out'])


def _state_shapes(n):
    return (jax.ShapeDtypeStruct((n, CONV_W - 1, D_LRU), f32),
            jax.ShapeDtypeStruct((n, 1, D_LRU), f32),
            jax.ShapeDtypeStruct((n, POOL_MAX - 1, D_POOL), f32))


def _state_specs(n, index_map):
    return [pl.BlockSpec((n, CONV_W - 1, D_LRU), index_map), pl.BlockSpec((n, 1, D_LRU), index_map),
            pl.BlockSpec((n, POOL_MAX - 1, D_POOL), index_map)]


def _mixer_step_call(x, state_conv, state_lru, state_pool, w, layer, segs, name):
    tm = x.shape[0]
    ns = state_conv.shape[1]
    max_len = max(seg[1] for seg in segs)
    weights = _mixer_weights(w)
    x_spec = pl.BlockSpec((tm, D_MODEL), lambda i: (0, 0))
    whole = lambda i: (0, 0, 0)
    state_in = [pl.BlockSpec((None,) + s.shape[1:], lambda i: (layer, 0, 0, 0))
                for s in (state_conv, state_lru, state_pool)]
    scratch = [
        pltpu.VMEM((tm, D_MODEL), bf16),
        pltpu.VMEM((len(segs), GATE_A_CH, HDR + max_len, NCH), f32),
        pltpu.VMEM((N_IN_CH - GATE_A_CH, tm, NCH), f32),
        pltpu.VMEM((tm, D_LRU), f32),
        pltpu.VMEM((tm, D_LRU), bf16),
        pltpu.VMEM((tm, D_LRU), f32),
        pltpu.VMEM((tm, D_LRU), f32),
        pltpu.VMEM((tm, D_LRU), bf16),
        pltpu.VMEM((tm, D_POOL), bf16),
        pltpu.VMEM((tm, D_POOL), bf16),
        pltpu.VMEM((tm, D_MODEL), f32),
        pltpu.VMEM((tm, D_MODEL), bf16),
    ]
    return pl.pallas_call(
        functools.partial(_mixer_step_kernel, tuple(segs), tm),
        grid=(1,),
        in_specs=[x_spec] + state_in + [_layer_spec(v, layer) for v in weights],
        out_specs=([x_spec] + _state_specs(ns, whole) + _state_specs(1, whole)
                   + [pl.BlockSpec(w['w_in'].shape[1:], lambda i: (0, 0))]),
        out_shape=((jax.ShapeDtypeStruct(x.shape, f32),) + _state_shapes(ns) + _state_shapes(1)
                   + (jax.ShapeDtypeStruct(w['w_in'].shape[1:], bf16),)),
        scratch_shapes=scratch,
        compiler_params=pltpu.CompilerParams(dimension_semantics=("arbitrary",), vmem_limit_bytes=VMEM_LIMIT),
        name=name,
    )(x, state_conv, state_lru, state_pool, *weights)


def _whole_spec(w):
    nd = w.ndim
    return pl.BlockSpec(w.shape, lambda *_: (0,) * nd, pipeline_mode=pl.Buffered(1))


def _layer_stream_call(x, conv_init, h_init, pool_init, w, w_in, ffn_weights, layer, final, tm, name):
    nb, t, _ = x.shape
    tps = t // tm
    n_tiles = nb * tps
    assert tps * tm == t
    weights = tuple(w_in if v is w['w_in'] else v for v in _mixer_weights(w))
    weight_specs = [_whole_spec(v) if v is w_in else _layer_spec(v, layer) for v in weights]
    x_tiles = x.reshape(n_tiles, tm, D_MODEL)
    x_spec = pl.BlockSpec((None, tm, D_MODEL), lambda j: (jnp.minimum(j, n_tiles - 1), 0, 0))
    y_spec = pl.BlockSpec((None, tm, D_MODEL), lambda j: (jnp.maximum(j - 1, 0), 0, 0))
    scratch = [
        pltpu.VMEM((tm, D_MODEL), bf16),
        pltpu.VMEM((GATE_A_CH, HDR, NCH), f32),
        pltpu.VMEM((N_IN_CH - GATE_A_CH, tm, NCH), f32),
        pltpu.VMEM((tm, D_LRU), f32),
        pltpu.VMEM((tm, D_LRU), bf16),
        pltpu.VMEM((tm, D_LRU), bf16),
        pltpu.VMEM((tm, D_POOL), bf16),
        pltpu.VMEM((tm, D_POOL), bf16),
        pltpu.VMEM((tm, D_MODEL), f32),
        pltpu.VMEM((tm, D_MODEL), bf16),
        pltpu.VMEM((SUBLANES, D_LRU), f32),
        pltpu.VMEM((2, tm, D_MODEL), f32),
        pltpu.VMEM((tm, D_MODEL), bf16),
        pltpu.VMEM((tm, D_FF), bf16),
    ]
    y, c_n, h_n, p_n = pl.pallas_call(
        functools.partial(_layer_stream_kernel, tm, tps, n_tiles, final),
        grid=(n_tiles + 1,),
        in_specs=([x_spec] + _state_specs(1, lambda j: (0, 0, 0)) + weight_specs
                  + [_whole_spec(v) for v in ffn_weights]),
        out_specs=[y_spec] + _state_specs(1, lambda j: (jnp.maximum(j - 1, 0) // tps, 0, 0)),
        out_shape=(jax.ShapeDtypeStruct(x_tiles.shape, f32),) + _state_shapes(nb),
        scratch_shapes=scratch,
        compiler_params=pltpu.CompilerParams(dimension_semantics=("arbitrary",), vmem_limit_bytes=VMEM_LIMIT),
        name=name,
    )(x_tiles, conv_init, h_init, pool_init, *weights, *ffn_weights)
    return y.reshape(x.shape), c_n, h_n, p_n


def _ffn_cast_call(x, vec, w_gu, w_down, layer, final, rows, name):
    tm = rows
    x_spec = pl.BlockSpec((tm, D_MODEL), lambda c: (0, 0))
    in_specs = [x_spec, _layer_spec(vec, layer),
                pl.BlockSpec((None, D_MODEL, NCH), lambda c: (layer, 0, c)),
                pl.BlockSpec((None, D_MODEL, NCH), lambda c: (layer, 0, N_FF_CH + c)),
                pl.BlockSpec((None, NCH, D_MODEL), lambda c: (layer, c, 0))]
    out_specs = [x_spec, pl.BlockSpec((D_MODEL, NCH), lambda c: (0, c)), pl.BlockSpec((D_MODEL, NCH), lambda c: (0, c)),
                 pl.BlockSpec((NCH, D_MODEL), lambda c: (c, 0))]
    out_shape = (jax.ShapeDtypeStruct((tm, D_MODEL), f32), jax.ShapeDtypeStruct((D_MODEL, D_FF), bf16),
                 jax.ShapeDtypeStruct((D_MODEL, D_FF), bf16), jax.ShapeDtypeStruct((D_FF, D_MODEL), bf16))
    return pl.pallas_call(
        functools.partial(_ffn_cast_kernel, final, tm),
        grid=(N_FF_CH,),
        in_specs=in_specs,
        out_specs=out_specs,
        out_shape=out_shape,
        scratch_shapes=[pltpu.VMEM((tm, D_MODEL), bf16), pltpu.VMEM((tm, D_MODEL), f32)],
        compiler_params=pltpu.CompilerParams(dimension_semantics=("arbitrary",), vmem_limit_bytes=VMEM_LIMIT),
        name=name,
    )(x, vec, w_gu, w_gu, w_down)


def kernel(x_prompt, x_sample, state_conv, state_lru, state_pool, meta_tokens, norm1_g, w_in, conv_w, conv_b,
           lam, w_r, b_r, w_i, b_i, pool_w, pool_scale, w_br_a, w_br_b, w_out, norm2_g, w_gu, w_down, final_g):
    depth = w_in.shape[0]
    ns, ts, _ = x_sample.shape
    vec_rows = [None] * VEC_ROWS
    for row, v in ((V_NORM1, norm1_g), (V_CONV_B, conv_b), (V_BR, b_r), (V_BI, b_i), (V_LAM, lam),
                   (V_POOL_SCALE, pool_scale), (V_NORM2, norm2_g),
                   (V_FINAL, jnp.broadcast_to(final_g, (depth, D_MODEL)))):
        vec_rows[row] = v.astype(f32)
    w = {
        'vec': jnp.stack(vec_rows, axis=1), 'conv_w': conv_w.astype(f32),
        'w_in': w_in.astype(f32),
        'wri': jnp.concatenate([w_r, w_i], axis=-1).astype(bf16),
        'pool_w': pool_w.astype(bf16),
        'w_br_a': w_br_a.astype(bf16), 'w_br_b': w_br_b.astype(bf16), 'w_out': w_out.astype(bf16),
    }
    w_gu, w_down = w_gu.astype(f32), w_down.astype(f32)

    segs_s = [(s * ts, ts, PAST_LEN, True) for s in range(ns)] + [(ns * ts, N_META, 0, False)]
    xs = jnp.concatenate([x_sample.reshape(ns * ts, D_MODEL), meta_tokens.astype(f32)], axis=0)
    xp = x_prompt
    state_lru = state_lru[:, :, None, :]

    outs = {k: [] for k in ('cp', 'hp', 'pp', 'cs', 'hs', 'ps')}
    for l in range(depth):
        last = l == depth - 1
        xs, c_s, h_s, p_s, c_m, h_m, p_m, w_in_l = _mixer_step_call(xs, state_conv, state_lru, state_pool, w, l,
                                                                    segs_s, f"mixer_step_l{l}")
        xs, *ffn_weights = _ffn_cast_call(xs, w['vec'], w_gu, w_down, l, last, ns * ts if last else xs.shape[0],
                                          f"ffn_step_l{l}")
        xp, c_p, h_p, p_p = _layer_stream_call(xp, c_m, h_m, p_m, w, w_in_l, ffn_weights, l, last, TM_LAYER,
                                               f"layer_prompt_l{l}")

        outs['cp'].append(c_p); outs['hp'].append(h_p[:, 0]); outs['pp'].append(p_p)
        outs['cs'].append(c_s); outs['hs'].append(h_s[:, 0]); outs['ps'].append(p_s)

    y_sample = xs.reshape(ns, ts, D_MODEL)
    return (xp, y_sample, jnp.stack(outs['cp']), jnp.stack(outs['hp']), jnp.stack(outs['pp']),
            jnp.stack(outs['cs']), jnp.stack(outs['hs']), jnp.stack(outs['ps']))
```

```python
import functools

import jax
import jax.numpy as jnp
from jax import lax
from jax.experimental import pallas as pl
from jax.experimental.pallas import tpu as pltpu

D_MODEL = 1024
D_LRU = 1024
D_POOL = 1024
LRU_HEADS = 8
LRU_BLOCK = D_LRU // LRU_HEADS
LRU_C = 8.0
CONV_W = 4
POOL_WINDOWS = (2, 4, 8, 16)
POOL_GROUP = D_POOL // len(POOL_WINDOWS)
POOL_MAX = 16
D_FF = 2816
N_META = 16
PAST_LEN = 2048
EPS = 1e-6

SUBLANES = 8
BF16_ROWS = 16
HDR = 16
NCH = 256
N_IN_CH = 4 * D_MODEL // NCH
CONV_CH = 0
POOL_CH = D_LRU // NCH
GATE_A_CH = (D_LRU + D_POOL) // NCH
GATE_B_CH = (D_LRU + D_POOL + D_MODEL) // NCH
N_OUT_CH = D_MODEL // NCH
N_FF_CH = D_FF // NCH
TM_LAYER = 256
ROW_CH = 32
NORM_ROWS_MAX = 128
V7X_VMEM_BYTES = 64 * 1024 * 1024
VMEM_LIMIT = V7X_VMEM_BYTES - 4 * 1024 * 1024

V_NORM1, V_CONV_B, V_BR, V_BI, V_LAM, V_POOL_SCALE, V_NORM2, V_FINAL = range(8)
VEC_ROWS = 8
SQ_BRANCH_A, SQ_BRANCH_B, SQ_OUT = range(3)

f32 = jnp.float32
bf16 = jnp.bfloat16

assert POOL_GROUP == NCH and HDR >= POOL_MAX - 1 and HDR % BF16_ROWS == 0
assert D_LRU == D_POOL == D_MODEL


def _dot(a, b):
    return jnp.dot(a, b, preferred_element_type=f32)


def _chunk(c):
    return slice(c * NCH, (c + 1) * NCH)


def _vec(v_ref, row, cols=slice(None)):
    return v_ref[row:row + 1, cols]


def _rmsnorm(x, g):
    y = x * lax.rsqrt(jnp.mean(x * x, axis=-1, keepdims=True) + EPS)
    return y * g


def _norm_chunk(tm):
    return max(c for c in range(BF16_ROWS, NORM_ROWS_MAX + 1, BF16_ROWS) if tm % c == 0)


def _sigmoid(x):
    return 0.5 * jnp.tanh(0.5 * x) + 0.5


def _shift_rows(x, k):
    return pltpu.roll(x, k, axis=0)


def _for_rows(total, chunk, fn):
    n = total // chunk
    assert n * chunk == total
    if n == 1:
        fn(0)
    else:
        def body(k, c):
            fn(pl.multiple_of(k * chunk, chunk))
            return c
        lax.fori_loop(0, n, body, 0)


def _norm_to_bf16(x_ref, g, xn_s, tm):
    nc = _norm_chunk(tm)

    def norm_rows(r0):
        rows = pl.ds(r0, nc)
        xn_s[rows, :] = _rmsnorm(x_ref[rows, :], g).astype(bf16)
    _for_rows(tm, nc, norm_rows)


def _conv_taps(blk, v_ref, cw_ref, cs):
    tap = lambda k: _shift_rows(blk, CONV_W - 1 - k)[HDR:] if k < CONV_W - 1 else blk[HDR:]
    acc = _vec(v_ref, V_CONV_B, cs) + cw_ref[0:1, cs] * tap(0)
    for k in range(1, CONV_W):
        acc = acc + cw_ref[k:k + 1, cs] * tap(k)
    return acc


def _lam_scaled(v_ref):
    lam = _vec(v_ref, V_LAM)
    log_sig_lam = jnp.minimum(lam, 0.0) - jnp.log1p(jnp.exp(-jnp.abs(lam)))
    return LRU_C * log_sig_lam


def _lru_coeffs(r_pre, i_pre, c, lam_c, start_mask=None):
    r = _sigmoid(r_pre)
    i = _sigmoid(i_pre)
    log_a = r * lam_c
    a = jnp.exp(log_a)
    th = jnp.tanh(log_a)
    mult = jnp.sqrt(-2.0 * th / (1.0 - th))
    if start_mask is not None:
        mult = jnp.where(start_mask, 1.0, mult)
    return a, mult * (i * c)


def _scan_block(a, b, h):
    sub = lax.broadcasted_iota(jnp.int32, a.shape, 0)
    for d in (1, 2, 4):
        keep = sub >= d
        a_prev = pltpu.roll(a, d, axis=0)
        b_prev = pltpu.roll(b, d, axis=0)
        b = jnp.where(keep, a * b_prev, 0.0) + b
        a = jnp.where(keep, a * a_prev, a)
    return a * h + b


def _scan_rows(a, b, h):
    outs = []
    for k in range(a.shape[0] // SUBLANES):
        blk = slice(k * SUBLANES, (k + 1) * SUBLANES)
        outs.append(_scan_block(a[blk], b[blk], h))
        h = outs[-1][SUBLANES - 1:SUBLANES, :]
    return jnp.concatenate(outs, axis=0), h


def _pool_diff(blk, w, cnt=None):
    xb = blk[HDR:]
    tot, span = blk, 1
    while span < w:
        tot = tot + _shift_rows(tot, span)
        span *= 2
    tot = tot[HDR:]
    pooled = tot * (1.0 / w) if cnt is None else tot / cnt
    return pooled - xb


def _lru_gate_preact(h, cbf, wri_ref, v_ref):
    hs = slice(h * LRU_BLOCK, (h + 1) * LRU_BLOCK)
    ri = _dot(cbf[:, hs], wri_ref[h])
    return ri[:, :LRU_BLOCK] + _vec(v_ref, V_BR, hs), ri[:, LRU_BLOCK:] + _vec(v_ref, V_BI, hs)


def _pool_project(g, d_s, yb_s, pw_ref, v_ref):
    gs = _chunk(g)
    yb_s[:, gs] = (_dot(d_s[:, gs], pw_ref[g]) * _vec(v_ref, V_POOL_SCALE, gs)).astype(bf16)


def _branch_b(c, gbuf, yb_s, wsq, mb_s):
    gate_b = gbuf[GATE_B_CH - GATE_A_CH + c]
    mb_s[:, _chunk(c)] = _sigmoid(gate_b) * _dot(yb_s[...], wsq[SQ_BRANCH_B][:, _chunk(c)])


def _merge(c, gbuf, ya_s, wsq, mb_s, m_s):
    cs = _chunk(c)
    m_s[:, cs] = (_sigmoid(gbuf[c]) * _dot(ya_s[...], wsq[SQ_BRANCH_A][:, cs]) + mb_s[:, cs]).astype(bf16)


def _mixer_step_kernel(segs, tm, x_ref, convp_ref, hp_ref, poolp_ref, v_ref, cw_ref, win_f32_ref, wri_ref, pw_ref,
                       wa_ref, wb_ref, wo_ref,
                       y_ref, convn_ref, hn_ref, pooln_ref, convm_ref, hm_ref, poolm_ref, win_out,
                       xn_s, proj, gbuf, cbuf, cbf, a_s, b_s, ya_s, d_s, yb_s, mb_s, m_s):
    n_hist = GATE_A_CH
    wsq = (wa_ref, wb_ref, wo_ref)
    assert sum(1 for seg in segs if not seg[3]) == 1
    for s, (_, _, _, has_state) in enumerate(segs):
        for c in range(n_hist):
            proj[s, c, 0:HDR, :] = jnp.zeros((HDR, NCH), f32)
        if has_state:
            for c in range(POOL_CH):
                proj[s, CONV_CH + c, HDR - (CONV_W - 1):HDR, :] = convp_ref[s, :, _chunk(c)]
                proj[s, POOL_CH + c, HDR - (POOL_MAX - 1):HDR, :] = poolp_ref[s, :, _chunk(c)]

    _norm_to_bf16(x_ref, _vec(v_ref, V_NORM1), xn_s, tm)

    for c in range(N_IN_CH):
        w_chunk = win_f32_ref[:, _chunk(c)].astype(bf16)
        win_out[:, _chunk(c)] = w_chunk
        p = _dot(xn_s[...], w_chunk)
        if c < n_hist:
            for s, (row0, length, _, _) in enumerate(segs):
                proj[s, c, HDR:HDR + length, :] = p[row0:row0 + length]
        else:
            gbuf[c - n_hist] = p

    for s, (row0, length, _, has_state) in enumerate(segs):
        n = min(ROW_CH, length)

        def conv_rows(t0, s=s, row0=row0, n=n):
            rows = pl.ds(row0 + t0, n)
            for c in range(POOL_CH):
                acc = _conv_taps(proj[s, CONV_CH + c, pl.ds(t0, n + HDR), :], v_ref, cw_ref, _chunk(c))
                cbuf[rows, _chunk(c)] = acc
                cbf[rows, _chunk(c)] = acc.astype(bf16)
        _for_rows(length, n, conv_rows)
        conv_out, pool_out, k = (convn_ref, pooln_ref, s) if has_state else (convm_ref, poolm_ref, 0)
        for c in range(POOL_CH):
            conv_out[k, :, _chunk(c)] = proj[s, CONV_CH + c, HDR + length - (CONV_W - 1):HDR + length, :]
            pool_out[k, :, _chunk(c)] = proj[s, POOL_CH + c, HDR + length - (POOL_MAX - 1):HDR + length, :]

    for h in range(LRU_HEADS):
        hs = slice(h * LRU_BLOCK, (h + 1) * LRU_BLOCK)
        a_s[:, hs], b_s[:, hs] = _lru_gate_preact(h, cbf, wri_ref, v_ref)
    lam_c = _lam_scaled(v_ref)

    for s, (row0, length, pos0, has_state) in enumerate(segs):
        n = min(ROW_CH, length)

        def mix_rows(t0, h, s=s, row0=row0, n=n, pos0=pos0):
            rows = pl.ds(row0 + t0, n)
            pos = pos0 + t0 + lax.broadcasted_iota(jnp.int32, (n, 1), 0)
            start = (pos == 0) if pos0 == 0 else None
            a, b = _lru_coeffs(a_s[rows, :], b_s[rows, :], cbuf[rows, :], lam_c, start)
            states, h = _scan_rows(a, b, h)
            ya_s[rows, :] = states.astype(bf16)
            cnt = None if pos0 >= POOL_MAX - 1 else pos + 1
            for g, w in enumerate(POOL_WINDOWS):
                cnt_g = None if cnt is None else jnp.minimum(w, cnt).astype(f32)
                d_s[rows, _chunk(g)] = _pool_diff(proj[s, POOL_CH + g, pl.ds(t0, n + HDR), :], w, cnt_g).astype(bf16)
            return h

        steps = length // n
        h = hp_ref[s] if has_state else jnp.zeros((1, D_LRU), f32)
        if steps == 1:
            h = mix_rows(0, h)
        else:
            h = lax.fori_loop(0, steps, lambda k, h: mix_rows(pl.multiple_of(k * n, n), h), h)
        if has_state:
            hn_ref[s] = h
        else:
            hm_ref[0] = h

    for g in range(len(POOL_WINDOWS)):
        _pool_project(g, d_s, yb_s, pw_ref, v_ref)
    for c in range(N_OUT_CH):
        _branch_b(c, gbuf, yb_s, wsq, mb_s)
        _merge(c, gbuf, ya_s, wsq, mb_s, m_s)
    for c in range(N_OUT_CH):
        y_ref[:, _chunk(c)] = x_ref[:, _chunk(c)] + _dot(m_s[...], wsq[SQ_OUT][:, _chunk(c)])


def _layer_stream_kernel(tm, tiles_per_stream, n_tiles, final,
                         x_ref, convp_ref, hp_ref, poolp_ref, v_ref, cw_ref, win_ref, wri_ref, pw_ref,
                         wa_ref, wb_ref, wo_ref, wg_ref, wu_ref, wd_ref,
                         y_ref, convn_ref, hn_ref, pooln_ref,
                         xn_s, hist, gbuf, cbuf, cbf, ya_s, d_s, yb_s, mb_s, m_s, hcar, ymid, xn2_s, h_s):
    j = pl.program_id(0)
    new = j % 2
    cur = 1 - new
    n_hist = GATE_A_CH
    nc = _norm_chunk(tm)
    wsq = (wa_ref, wb_ref, wo_ref)

    @pl.when(j == 0)
    def _():
        hist[...] = jnp.zeros(hist.shape, f32)
        hcar[...] = jnp.zeros(hcar.shape, f32)
        ymid[1] = jnp.zeros((tm, D_MODEL), f32)

    for c in range(POOL_CH):
        convn_ref[0, :, _chunk(c)] = hist[CONV_CH + c, HDR - (CONV_W - 1):HDR, :]
        pooln_ref[0, :, _chunk(c)] = hist[POOL_CH + c, HDR - (POOL_MAX - 1):HDR, :]
    hn_ref[0] = hcar[0:1, :]

    @pl.when(jnp.logical_and(j % tiles_per_stream == 0, j < n_tiles))
    def _():
        for c in range(POOL_CH):
            hist[CONV_CH + c] = jnp.zeros((HDR, NCH), f32)
            hist[POOL_CH + c] = jnp.zeros((HDR, NCH), f32)
            hist[CONV_CH + c, HDR - (CONV_W - 1):HDR, :] = convp_ref[0, :, _chunk(c)]
            hist[POOL_CH + c, HDR - (POOL_MAX - 1):HDR, :] = poolp_ref[0, :, _chunk(c)]
        hcar[0:1, :] = hp_ref[0]

    lam_c = _lam_scaled(v_ref)

    def ffn_norm():
        for r0 in range(0, tm, nc):
            xn2_s[r0:r0 + nc, :] = _rmsnorm(ymid[cur, r0:r0 + nc, :], _vec(v_ref, V_NORM2)).astype(bf16)

    def ffn_up(cs):
        gate = _dot(xn2_s[...], wg_ref[:, cs])
        up = _dot(xn2_s[...], wu_ref[:, cs])
        h_s[:, cs] = (jax.nn.silu(gate) * up).astype(bf16)

    def ffn_down(cs):
        y_ref[:, cs] = ymid[cur, :, cs] + _dot(h_s[...], wd_ref[:, cs])

    def ffn_final_norm():
        for r0 in range(0, tm, nc):
            y_ref[r0:r0 + nc, :] = _rmsnorm(y_ref[r0:r0 + nc, :], _vec(v_ref, V_FINAL))

    def mixer_norm():
        for r0 in range(0, tm, nc):
            xn_s[r0:r0 + nc, :] = _rmsnorm(x_ref[r0:r0 + nc, :], _vec(v_ref, V_NORM1)).astype(bf16)

    def project_with_history(c):
        p = _dot(xn_s[...], win_ref[:, _chunk(c)])
        ext = jnp.concatenate([hist[c], p], axis=0)
        hist[c] = p[tm - HDR:]
        return ext

    def conv_chunk(c):
        acc = _conv_taps(project_with_history(CONV_CH + c), v_ref, cw_ref, _chunk(c))
        cbuf[:, _chunk(c)] = acc
        cbf[:, _chunk(c)] = acc.astype(bf16)

    def pool_chunk(g):
        d_s[:, _chunk(g)] = _pool_diff(project_with_history(POOL_CH + g), POOL_WINDOWS[g]).astype(bf16)

    def gate_chunk(c):
        gbuf[c - n_hist] = _dot(xn_s[...], win_ref[:, _chunk(c)])

    def lru_head(h):
        hs = slice(h * LRU_BLOCK, (h + 1) * LRU_BLOCK)
        r_pre, i_pre = _lru_gate_preact(h, cbf, wri_ref, v_ref)
        a, b = _lru_coeffs(r_pre, i_pre, cbuf[:, hs], lam_c[:, hs])
        states, last = _scan_rows(a, b, hcar[0:1, hs])
        ya_s[:, hs] = states.astype(bf16)
        hcar[0:1, hs] = last

    def project_out(c):
        ymid[new, :, _chunk(c)] = x_ref[:, _chunk(c)] + _dot(m_s[...], wsq[SQ_OUT][:, _chunk(c)])

    wide = 2 * NCH
    filler = [functools.partial(ffn_up, slice(lo, min(lo + wide, D_FF))) for lo in range(0, D_FF, wide)]
    filler += [functools.partial(ffn_down, slice(lo, lo + wide)) for lo in range(0, D_MODEL, wide)]
    if final:
        filler.append(ffn_final_norm)
    others = [functools.partial(pool_chunk, g) for g in range(len(POOL_WINDOWS))]
    others += [functools.partial(gate_chunk, GATE_B_CH + c) for c in range(N_OUT_CH)]
    others += [functools.partial(_pool_project, g, d_s, yb_s, pw_ref, v_ref) for g in range(len(POOL_WINDOWS))]
    others += [functools.partial(_branch_b, c, gbuf, yb_s, wsq, mb_s) for c in range(N_OUT_CH)]
    others += [functools.partial(gate_chunk, GATE_A_CH + c) for c in range(N_OUT_CH)]

    stages = [functools.partial(conv_chunk, c) for c in range(POOL_CH)]
    stages += [functools.partial(lru_head, h) for h in range(LRU_HEADS)]
    first_head = POOL_CH

    ffn_norm()
    mixer_norm()
    for k, stage in enumerate(stages):
        stage()
        lo, hi = (k * len(filler)) // len(stages), ((k + 1) * len(filler)) // len(stages)
        for fn in filler[lo:hi]:
            fn()
        if k >= first_head:
            h = k - first_head
            lo, hi = (h * len(others)) // LRU_HEADS, ((h + 1) * len(others)) // LRU_HEADS
            for fn in others[lo:hi]:
                fn()
    for c in range(N_OUT_CH):
        _merge(c, gbuf, ya_s, wsq, mb_s, m_s)
    for c in range(N_OUT_CH):
        project_out(c)


def _ffn_cast_kernel(final, tm, x_ref, v_ref, wg_ref, wu_ref, wd_ref, y_ref, wg_out, wu_out, wd_out, xn_s, acc_s):
    c = pl.program_id(0)
    nc = _norm_chunk(tm)

    @pl.when(c == 0)
    def _():
        _norm_to_bf16(x_ref, _vec(v_ref, V_NORM2), xn_s, tm)
        acc_s[...] = x_ref[...]

    wg = wg_ref[...].astype(bf16)
    wu = wu_ref[...].astype(bf16)
    wd = wd_ref[...].astype(bf16)
    wg_out[...] = wg
    wu_out[...] = wu
    wd_out[...] = wd
    hidden = (jax.nn.silu(_dot(xn_s[...], wg)) * _dot(xn_s[...], wu)).astype(bf16)
    acc_s[...] += _dot(hidden, wd)

    @pl.when(c == pl.num_programs(0) - 1)
    def _():
        if final:
            def final_rows(r0):
                rows = pl.ds(r0, nc)
                y_ref[rows, :] = _rmsnorm(acc_s[rows, :], _vec(v_ref, V_FINAL))
            _for_rows(tm, nc, final_rows)
        else:
            y_ref[...] = acc_s[...]


def _layer_spec(w, layer):
    nd = w.ndim - 1
    return pl.BlockSpec((None,) + w.shape[1:], lambda *_: (layer,) + (0,) * nd, pipeline_mode=pl.Buffered(1))


def _mixer_weights(w):
    return (w['vec'], w['conv_w'], w['w_in'], w['wri'], w['pool_w'], w['w_br_a'], w['w_br_b'], w['w_out'])


def _state_shapes(n):
    return (jax.ShapeDtypeStruct((n, CONV_W - 1, D_LRU), f32),
            jax.ShapeDtypeStruct((n, 1, D_LRU), f32),
            jax.ShapeDtypeStruct((n, POOL_MAX - 1, D_POOL), f32))


def _state_specs(n, index_map):
    return [pl.BlockSpec((n, CONV_W - 1, D_LRU), index_map), pl.BlockSpec((n, 1, D_LRU), index_map),
            pl.BlockSpec((n, POOL_MAX - 1, D_POOL), index_map)]


def _mixer_step_call(x, state_conv, state_lru, state_pool, w, layer, segs, name):
    tm = x.shape[0]
    ns = state_conv.shape[1]
    max_len = max(seg[1] for seg in segs)
    weights = _mixer_weights(w)
    x_spec = pl.BlockSpec((tm, D_MODEL), lambda i: (0, 0))
    whole = lambda i: (0, 0, 0)
    state_in = [pl.BlockSpec((None,) + s.shape[1:], lambda i: (layer, 0, 0, 0))
                for s in (state_conv, state_lru, state_pool)]
    scratch = [
        pltpu.VMEM((tm, D_MODEL), bf16),
        pltpu.VMEM((len(segs), GATE_A_CH, HDR + max_len, NCH), f32),
        pltpu.VMEM((N_IN_CH - GATE_A_CH, tm, NCH), f32),
        pltpu.VMEM((tm, D_LRU), f32),
        pltpu.VMEM((tm, D_LRU), bf16),
        pltpu.VMEM((tm, D_LRU), f32),
        pltpu.VMEM((tm, D_LRU), f32),
        pltpu.VMEM((tm, D_LRU), bf16),
        pltpu.VMEM((tm, D_POOL), bf16),
        pltpu.VMEM((tm, D_POOL), bf16),
        pltpu.VMEM((tm, D_MODEL), f32),
        pltpu.VMEM((tm, D_MODEL), bf16),
    ]
    return pl.pallas_call(
        functools.partial(_mixer_step_kernel, tuple(segs), tm),
        grid=(1,),
        in_specs=[x_spec] + state_in + [_layer_spec(v, layer) for v in weights],
        out_specs=([x_spec] + _state_specs(ns, whole) + _state_specs(1, whole)
                   + [pl.BlockSpec(w['w_in'].shape[1:], lambda i: (0, 0))]),
        out_shape=((jax.ShapeDtypeStruct(x.shape, f32),) + _state_shapes(ns) + _state_shapes(1)
                   + (jax.ShapeDtypeStruct(w['w_in'].shape[1:], bf16),)),
        scratch_shapes=scratch,
        compiler_params=pltpu.CompilerParams(dimension_semantics=("arbitrary",), vmem_limit_bytes=VMEM_LIMIT),
        name=name,
    )(x, state_conv, state_lru, state_pool, *weights)


def _whole_spec(w):
    nd = w.ndim
    return pl.BlockSpec(w.shape, lambda *_: (0,) * nd, pipeline_mode=pl.Buffered(1))


def _layer_stream_call(x, conv_init, h_init, pool_init, w, w_in, ffn_weights, layer, final, tm, name):
    nb, t, _ = x.shape
    tps = t // tm
    n_tiles = nb * tps
    assert tps * tm == t
    weights = tuple(w_in if v is w['w_in'] else v for v in _mixer_weights(w))
    weight_specs = [_whole_spec(v) if v is w_in else _layer_spec(v, layer) for v in weights]
    x_tiles = x.reshape(n_tiles, tm, D_MODEL)
    x_spec = pl.BlockSpec((None, tm, D_MODEL), lambda j: (jnp.minimum(j, n_tiles - 1), 0, 0))
    y_spec = pl.BlockSpec((None, tm, D_MODEL), lambda j: (jnp.maximum(j - 1, 0), 0, 0))
    scratch = [
        pltpu.VMEM((tm, D_MODEL), bf16),
        pltpu.VMEM((GATE_A_CH, HDR, NCH), f32),
        pltpu.VMEM((N_IN_CH - GATE_A_CH, tm, NCH), f32),
        pltpu.VMEM((tm, D_LRU), f32),
        pltpu.VMEM((tm, D_LRU), bf16),
        pltpu.VMEM((tm, D_LRU), bf16),
        pltpu.VMEM((tm, D_POOL), bf16),
        pltpu.VMEM((tm, D_POOL), bf16),
        pltpu.VMEM((tm, D_MODEL), f32),
        pltpu.VMEM((tm, D_MODEL), bf16),
        pltpu.VMEM((SUBLANES, D_LRU), f32),
        pltpu.VMEM((2, tm, D_MODEL), f32),
        pltpu.VMEM((tm, D_MODEL), bf16),
        pltpu.VMEM((tm, D_FF), bf16),
    ]
    y, c_n, h_n, p_n = pl.pallas_call(
        functools.partial(_layer_stream_kernel, tm, tps, n_tiles, final),
        grid=(n_tiles + 1,),
        in_specs=([x_spec] + _state_specs(1, lambda j: (0, 0, 0)) + weight_specs
                  + [_whole_spec(v) for v in ffn_weights]),
        out_specs=[y_spec] + _state_specs(1, lambda j: (jnp.maximum(j - 1, 0) // tps, 0, 0)),
        out_shape=(jax.ShapeDtypeStruct(x_tiles.shape, f32),) + _state_shapes(nb),
        scratch_shapes=scratch,
        compiler_params=pltpu.CompilerParams(dimension_semantics=("arbitrary",), vmem_limit_bytes=VMEM_LIMIT),
        name=name,
    )(x_tiles, conv_init, h_init, pool_init, *weights, *ffn_weights)
    return y.reshape(x.shape), c_n, h_n, p_n


def _ffn_cast_call(x, vec, w_gu, w_down, layer, final, rows, name):
    tm = rows
    x_spec = pl.BlockSpec((tm, D_MODEL), lambda c: (0, 0))
    in_specs = [x_spec, _layer_spec(vec, layer),
                pl.BlockSpec((None, D_MODEL, NCH), lambda c: (layer, 0, c)),
                pl.BlockSpec((None, D_MODEL, NCH), lambda c: (layer, 0, N_FF_CH + c)),
                pl.BlockSpec((None, NCH, D_MODEL), lambda c: (layer, c, 0))]
    out_specs = [x_spec, pl.BlockSpec((D_MODEL, NCH), lambda c: (0, c)), pl.BlockSpec((D_MODEL, NCH), lambda c: (0, c)),
                 pl.BlockSpec((NCH, D_MODEL), lambda c: (c, 0))]
    out_shape = (jax.ShapeDtypeStruct((tm, D_MODEL), f32), jax.ShapeDtypeStruct((D_MODEL, D_FF), bf16),
                 jax.ShapeDtypeStruct((D_MODEL, D_FF), bf16), jax.ShapeDtypeStruct((D_FF, D_MODEL), bf16))
    return pl.pallas_call(
        functools.partial(_ffn_cast_kernel, final, tm),
        grid=(N_FF_CH,),
        in_specs=in_specs,
        out_specs=out_specs,
        out_shape=out_shape,
        scratch_shapes=[pltpu.VMEM((tm, D_MODEL), bf16), pltpu.VMEM((tm, D_MODEL), f32)],
        compiler_params=pltpu.CompilerParams(dimension_semantics=("arbitrary",), vmem_limit_bytes=VMEM_LIMIT),
        name=name,
    )(x, vec, w_gu, w_gu, w_down)


def kernel(x_prompt, x_sample, state_conv, state_lru, state_pool, meta_tokens, norm1_g, w_in, conv_w, conv_b,
           lam, w_r, b_r, w_i, b_i, pool_w, pool_scale, w_br_a, w_br_b, w_out, norm2_g, w_gu, w_down, final_g):
    depth = w_in.shape[0]
    ns, ts, _ = x_sample.shape
    vec_rows = [None] * VEC_ROWS
    for row, v in ((V_NORM1, norm1_g), (V_CONV_B, conv_b), (V_BR, b_r), (V_BI, b_i), (V_LAM, lam),
                   (V_POOL_SCALE, pool_scale), (V_NORM2, norm2_g),
                   (V_FINAL, jnp.broadcast_to(final_g, (depth, D_MODEL)))):
        vec_rows[row] = v.astype(f32)
    w = {
        'vec': jnp.stack(vec_rows, axis=1), 'conv_w': conv_w.astype(f32),
        'w_in': w_in.astype(f32),
        'wri': jnp.concatenate([w_r, w_i], axis=-1).astype(bf16),
        'pool_w': pool_w.astype(bf16),
        'w_br_a': w_br_a.astype(bf16), 'w_br_b': w_br_b.astype(bf16), 'w_out': w_out.astype(bf16),
    }
    w_gu, w_down = w_gu.astype(f32), w_down.astype(f32)

    segs_s = [(s * ts, ts, PAST_LEN, True) for s in range(ns)] + [(ns * ts, N_META, 0, False)]
    xs = jnp.concatenate([x_sample.reshape(ns * ts, D_MODEL), meta_tokens.astype(f32)], axis=0)
    xp = x_prompt
    state_lru = state_lru[:, :, None, :]

    outs = {k: [] for k in ('cp', 'hp', 'pp', 'cs', 'hs', 'ps')}
    for l in range(depth):
        last = l == depth - 1
        xs, c_s, h_s, p_s, c_m, h_m, p_m, w_in_l = _mixer_step_call(xs, state_conv, state_lru, state_pool, w, l,
                                                                    segs_s, f"mixer_step_l{l}")
        xs, *ffn_weights = _ffn_cast_call(xs, w['vec'], w_gu, w_down, l, last, ns * ts if last else xs.shape[0],
                                          f"ffn_step_l{l}")
        xp, c_p, h_p, p_p = _layer_stream_call(xp, c_m, h_m, p_m, w, w_in_l, ffn_weights, l, last, TM_LAYER,
                                               f"layer_prompt_l{l}")

        outs['cp'].append(c_p); outs['hp'].append(h_p[:, 0]); outs['pp'].append(p_p)
        outs['cs'].append(c_s); outs['hs'].append(h_s[:, 0]); outs['ps'].append(p_s)

    y_sample = xs.reshape(ns, ts, D_MODEL)
    return (xp, y_sample, jnp.stack(outs['cp']), jnp.stack(outs['hp']), jnp.stack(outs['pp']),
            jnp.stack(outs['cs']), jnp.stack(outs['hs']), jnp.stack(outs['ps']))
```
